```python
import math
import jax, jax.numpy as jnp
from jax import lax
import numpy as np

D_MODEL = 1024
BATCH = 8
SEQ = 4096
DEPTH = 1

D_CONV = 512
CONV_K = 3
N_HEADS = 8
HEAD_DIM = 64
D_ATTN = N_HEADS * HEAD_DIM
N_IDX_HEADS = 8
IDX_DIM = 64
TOPK_MAX = 256
Q_BLOCK = 128
N_BUCKETS = 32
MAX_EXACT = 16
MAX_DISTANCE = 128
D_FF = -(-8 * D_MODEL // (3 * 256)) * 256
EPS = 1e-6

PROJ_SIZES = (
    D_CONV, D_CONV, D_CONV,
    D_ATTN, HEAD_DIM, HEAD_DIM,
    N_IDX_HEADS * IDX_DIM, IDX_DIM,
    N_IDX_HEADS,
    D_MODEL, D_MODEL,
)
N_PROJ = int(sum(PROJ_SIZES))
PROJ_SPLITS = [int(v) for v in np.cumsum(PROJ_SIZES)[:-1]]

kernel_name = "hybrid_shortconv_dsa_gated_block"


def rms_norm(x, g):
    xf = x.astype(jnp.float32)
    y = xf * lax.rsqrt(jnp.mean(xf * xf, axis=-1, keepdims=True) + EPS)
    return (y * g.astype(jnp.float32)).astype(x.dtype)


def rel_bucket(dist):
    n = jnp.maximum(dist, 0)
    large = MAX_EXACT + (jnp.log(jnp.maximum(n, 1).astype(jnp.float32) / MAX_EXACT)
                         / math.log(MAX_DISTANCE / MAX_EXACT)
                         * (N_BUCKETS - MAX_EXACT)).astype(jnp.int32)
    large = jnp.minimum(large, N_BUCKETS - 1)
    return jnp.where(n < MAX_EXACT, n, large)


def short_conv_mixer(gate_b, gate_c, x_a, conv_w):
    u = gate_c * x_a
    rhs = conv_w[:, None, :].astype(u.dtype)
    y = lax.conv_general_dilated(u, rhs, window_strides=(1,), padding=[(CONV_K - 1, 0)],
                                 dimension_numbers=("NWC", "WIO", "NWC"),
                                 feature_group_count=D_CONV)
    return gate_b * y


def dsa_attention(q, k, v, iq, ik, iw, rel_bias):
    bsz, seq_len = q.shape[0], q.shape[1]
    k_top = min(TOPK_MAX, seq_len // 4)
    n_blk = seq_len // Q_BLOCK
    iw = iw * (N_IDX_HEADS ** -0.5 * IDX_DIM ** -0.5)
    s_pos = jnp.arange(seq_len, dtype=jnp.int32)
    t_blocks = s_pos.reshape(n_blk, Q_BLOCK)

    def to_blocks(a):
        return jnp.moveaxis(a.reshape(bsz, n_blk, Q_BLOCK, *a.shape[2:]), 1, 0)

    gather = jax.vmap(lambda arr, ind: arr[ind])

    def block_fn(args):
        q_b, iq_b, iw_b, t_b = args
        dots = jnp.einsum("bthd,bsd->bths", iq_b, ik)
        score = jnp.einsum("bth,bths->bts", iw_b, jax.nn.relu(dots)).astype(jnp.float32)
        causal = s_pos[None, :] <= t_b[:, None]
        score = jnp.where(causal[None], score, -jnp.inf)
        _, idx = lax.top_k(score, k_top)
        valid = idx <= t_b[None, :, None]
        k_sel = gather(k, idx)
        v_sel = gather(v, idx)
        logits = jnp.einsum("bthd,btkd->bhtk", q_b, k_sel).astype(jnp.float32) * (HEAD_DIM ** -0.5)
        bias = rel_bias[rel_bucket(t_b[None, :, None] - idx)]
        logits = logits + jnp.moveaxis(bias, -1, 1).astype(jnp.float32)
        logits = jnp.where(valid[:, None], logits, -jnp.inf)
        p = jax.nn.softmax(logits, axis=-1).astype(v.dtype)
        o = jnp.einsum("bhtk,btkd->bthd", p, v_sel)
        return o.reshape(bsz, Q_BLOCK, D_ATTN)

    out = lax.map(block_fn, (to_blocks(q), to_blocks(iq), to_blocks(iw), t_blocks))
    return jnp.moveaxis(out, 0, 1).reshape(bsz, seq_len, D_ATTN)


def setup_inputs(seed: int = 0) -> dict:
    key = jax.random.key(seed)
    ks = jax.random.split(key, 16)
    f32 = jnp.float32
    nrm = lambda k, shape, scale: jax.random.normal(k, shape, f32) * scale
    return {
        "x": nrm(ks[0], (BATCH, SEQ, D_MODEL), 1.0),
        "g_mix": 1.0 + nrm(ks[1], (DEPTH, D_MODEL), 0.02),
        "w_in": nrm(ks[2], (DEPTH, D_MODEL, N_PROJ), D_MODEL ** -0.5),
        "b_gate": nrm(ks[3], (DEPTH, 2 * D_MODEL), 0.02),
        "conv_w": nrm(ks[4], (DEPTH, CONV_K, D_CONV), CONV_K ** -0.5),
        "w_branch_a": nrm(ks[5], (DEPTH, D_CONV, D_MODEL), D_CONV ** -0.5),
        "w_branch_b": nrm(ks[6], (DEPTH, D_ATTN, D_MODEL), D_ATTN ** -0.5),
        "w_out": nrm(ks[7], (DEPTH, D_MODEL, D_MODEL), D_MODEL ** -0.5),
        "rel_bias": nrm(ks[8], (N_BUCKETS, N_HEADS), 0.5),
        "g_ffn": 1.0 + nrm(ks[9], (DEPTH, D_MODEL), 0.02),
        "w_ffn_gate": nrm(ks[10], (DEPTH, D_MODEL, D_FF), D_MODEL ** -0.5),
        "w_ffn_up": nrm(ks[11], (DEPTH, D_MODEL, D_FF), D_MODEL ** -0.5),
        "w_ffn_down": nrm(ks[12], (DEPTH, D_FF, D_MODEL), D_FF ** -0.5),
        "g_final": 1.0 + nrm(ks[13], (D_MODEL,), 0.02),
    }


def reference(x, g_mix, w_in, b_gate, conv_w, w_branch_a, w_branch_b, w_out, rel_bias,
              g_ffn, w_ffn_gate, w_ffn_up, w_ffn_down, g_final):
    bsz, seq_len, _ = x.shape
    for layer in range(DEPTH):
        h = rms_norm(x, g_mix[layer])
        proj = h @ w_in[layer]
        (c_b, c_c, c_x, q, k, v, iq, ik, iw, g_a, g_b) = jnp.split(proj, PROJ_SPLITS, axis=-1)
        y_a = short_conv_mixer(c_b, c_c, c_x, conv_w[layer]) @ w_branch_a[layer]
        attn = dsa_attention(q.reshape(bsz, seq_len, N_HEADS, HEAD_DIM), k, v,
                             iq.reshape(bsz, seq_len, N_IDX_HEADS, IDX_DIM), ik, iw, rel_bias)
        y_b = attn @ w_branch_b[layer]
        gates = jax.nn.sigmoid(jnp.concatenate([g_a, g_b], axis=-1) + b_gate[layer])
        merged = gates[..., :D_MODEL] * y_a + gates[..., D_MODEL:] * y_b
        x = x + merged @ w_out[layer]
        h2 = rms_norm(x, g_ffn[layer])
        x = x + (jax.nn.silu(h2 @ w_ffn_gate[layer]) * (h2 @ w_ffn_up[layer])) @ w_ffn_down[layer]
    return rms_norm(x, g_final)
```

```python
import functools
import math

import numpy as np
import jax
import jax.numpy as jnp
from jax import lax
from jax.experimental import pallas as pl
from jax.experimental.pallas import tpu as pltpu

D_MODEL = 1024
D_CONV = 512
CONV_K = 3
N_HEADS = 8
HEAD_DIM = 64
D_ATTN = N_HEADS * HEAD_DIM
N_IDX_HEADS = 8
IDX_DIM = 64
TOPK_MAX = 256
N_BUCKETS = 32
MAX_EXACT = 16
MAX_DISTANCE = 128
EPS = 1e-6

F32 = jnp.float32
BF16 = jnp.bfloat16
I32 = jnp.int32

INT_MIN = -(2 ** 31)
NEG = -1e30

TM1 = 512
TQ = 256
TK = 256
TM3 = 256
VMEM_LIMIT = 56 * 1024 * 1024


def _bucket_steps():
    n = np.arange(2 * MAX_DISTANCE)
    large = MAX_EXACT + (np.log(np.maximum(n, 1).astype(np.float32) / MAX_EXACT)
                         / math.log(MAX_DISTANCE / MAX_EXACT)
                         * (N_BUCKETS - MAX_EXACT)).astype(np.int32)
    large = np.minimum(large, N_BUCKETS - 1)
    b = np.where(n < MAX_EXACT, n, large)
    assert np.all(b[MAX_DISTANCE:] == N_BUCKETS - 1) and np.all(np.diff(b) >= 0)
    steps = [(int(i), int(b[i])) for i in range(1, len(b)) if b[i] != b[i - 1]]
    return int(b[0]), steps


def _rms(x, g):
    return x * lax.rsqrt(jnp.mean(x * x, axis=-1, keepdims=True) + EPS) * g


def _proj_kernel(x_ref, g_ref, wn_ref, wt_ref, cw_ref,
                 mix_ref, k_ref, ik_ref, qT_ref, iqT_ref, vT_ref, iwT_ref, ubuf_ref):
    i = pl.program_id(1)
    x = x_ref[0]
    h = _rms(x, g_ref[...]).astype(BF16)
    pn = jnp.dot(h, wn_ref[...], preferred_element_type=F32)
    pt = lax.dot_general(wt_ref[...], h, (((1,), (1,)), ((), ())),
                         preferred_element_type=F32)

    c_b = pn[:, 0:D_CONV]
    u = pn[:, D_CONV:2 * D_CONV] * pn[:, 2 * D_CONV:3 * D_CONV]

    @pl.when(i == 0)
    def _():
        ubuf_ref[0:8, :] = jnp.zeros((8, D_CONV), F32)

    ubuf_ref[8:8 + TM1, :] = u
    u1 = ubuf_ref[7:7 + TM1, :]
    u2 = ubuf_ref[6:6 + TM1, :]
    cw = cw_ref[...]
    y = cw[0:1, :] * u2 + cw[1:2, :] * u1 + cw[2:3, :] * u
    mix_ref[0] = (c_b * y).astype(BF16)
    ubuf_ref[0:8, :] = u[TM1 - 8:TM1, :]

    k_ref[0] = pn[:, 3 * D_CONV:3 * D_CONV + HEAD_DIM].astype(BF16)
    ik_ref[0] = pn[:, 3 * D_CONV + HEAD_DIM:3 * D_CONV + HEAD_DIM + IDX_DIM].astype(BF16)

    qT_ref[0] = pt[0:D_ATTN, :].astype(BF16)
    iqT_ref[0] = pt[D_ATTN:2 * D_ATTN, :].astype(BF16)
    vT = pt[2 * D_ATTN:2 * D_ATTN + HEAD_DIM, :].astype(BF16)
    for j in range(TM1 // TK):
        vT_ref[0, j] = vT[:, j * TK:(j + 1) * TK]
    iw = pt[2 * D_ATTN + HEAD_DIM:2 * D_ATTN + HEAD_DIM + N_IDX_HEADS, :]
    iwT_ref[0] = iw * (N_IDX_HEADS ** -0.5 * IDX_DIM ** -0.5)


def _dsa_kernel(rb_ref, iqT_ref, iwT_ref, ik_ref, qT_ref, k_ref, vT_ref,
                o_ref, key_ref, negm_ref, bias_ref, oT_ref, *, k_top):
    b = pl.program_id(0)
    qi = pl.program_id(1)
    nchunks = qi + 1

    @pl.when((b == 0) & (qi == 0))
    def _():
        b0, steps = _bucket_steps()
        row = lax.broadcasted_iota(I32, (TK, TQ), 0)
        col = lax.broadcasted_iota(I32, (TK, TQ), 1)
        for off in range(2):
            dist = col - row + off * TK
            for hh in range(N_HEADS):
                val = jnp.full((TK, TQ), rb_ref[b0, hh], F32)
                for n0, bk in steps:
                    val = jnp.where(dist >= n0, rb_ref[bk, hh], val)
                bias_ref[off, hh] = val - rb_ref[N_BUCKETS - 1, hh]

    w = iwT_ref[0]

    def score_chunk(c, carry):
        ikc = ik_ref[0, pl.ds(pl.multiple_of(c * TK, TK), TK), :]
        acc = jnp.zeros((TK, TQ), F32)
        for hh in range(N_IDX_HEADS):
            d = jnp.dot(ikc, iqT_ref[0, hh * IDX_DIM:(hh + 1) * IDX_DIM, :],
                        preferred_element_type=F32)
            acc = acc + w[hh:hh + 1, :] * jnp.maximum(d, 0.0)
        bits = pltpu.bitcast(acc, I32)
        key = bits ^ ((bits >> 31) & 0x7FFFFFFF)
        row = lax.broadcasted_iota(I32, (TK, TQ), 0) + c * TK
        col = lax.broadcasted_iota(I32, (TK, TQ), 1) + qi * TQ
        key_ref[c] = jnp.where(row <= col, key, INT_MIN)
        return carry

    lax.fori_loop(0, nchunks, score_chunk, 0)

    def count(pred):
        def body(c, acc):
            m = pred(key_ref[c], c).astype(I32)
            return acc + jnp.sum(m.reshape(TK // 8, 8, TQ), axis=0)
        acc = lax.fori_loop(0, nchunks, body, jnp.zeros((8, TQ), I32))
        return jnp.sum(acc, axis=0, keepdims=True)

    def thr_step(it, thr):
        cand = thr + jnp.left_shift(jnp.int32(1), 31 - it)
        cnt = count(lambda kc, c: kc >= cand)
        return jnp.where(cnt >= k_top, cand, thr)

    thr = lax.fori_loop(0, 32, thr_step, jnp.full((1, TQ), INT_MIN, I32))
    thr = jnp.maximum(thr, INT_MIN + 1)
    cnt_ge = count(lambda kc, c: kc >= thr)
    cnt_gt = count(lambda kc, c: kc > thr)
    room = k_top - cnt_gt

    def key_index(c):
        return lax.broadcasted_iota(I32, (TK, TQ), 0) + c * TK

    def tie_search(_):
        def step(it, lim):
            cand = lim + jnp.left_shift(jnp.int32(1), 12 - it)
            cnt = count(lambda kc, c: (kc == thr) & (key_index(c) < cand))
            return jnp.where(cnt <= room, cand, lim)
        return lax.fori_loop(0, 13, step, jnp.zeros((1, TQ), I32))

    tie_lim = lax.cond(jnp.max(cnt_ge) > k_top, tie_search,
                       lambda _: jnp.full((1, TQ), 2 ** 13, I32), 0)

    def mask_chunk(c, carry):
        kc = key_ref[c]
        sel = (kc > thr) | ((kc == thr) & (key_index(c) < tie_lim))
        negm_ref[c] = jnp.where(sel, 0.0, NEG).astype(F32)
        return carry

    lax.fori_loop(0, nchunks, mask_chunk, 0)

    for hh in range(N_HEADS):
        qT_h = qT_ref[0, hh * HEAD_DIM:(hh + 1) * HEAD_DIM, :]

        def attn_step(c, carry, with_bias):
            m, l, acc = carry
            kc = k_ref[0, pl.ds(pl.multiple_of(c * TK, TK), TK), :]
            lt = jnp.dot(kc, qT_h, preferred_element_type=F32) + negm_ref[c]
            if with_bias:
                lt = lt + bias_ref[qi - c, hh]
            m_new = jnp.maximum(m, jnp.max(lt, axis=0, keepdims=True))
            alpha = jnp.exp(m - m_new)
            p = jnp.exp(lt - m_new)
            l = alpha * l + jnp.sum(p, axis=0, keepdims=True)
            acc = alpha * acc + jnp.dot(vT_ref[0, c], p.astype(BF16),
                                        preferred_element_type=F32)
            return m_new, l, acc

        carry = (jnp.full((1, TQ), NEG, F32), jnp.zeros((1, TQ), F32),
                 jnp.zeros((HEAD_DIM, TQ), F32))
        near0 = jnp.maximum(qi - 1, 0)
        carry = lax.fori_loop(0, near0, functools.partial(attn_step, with_bias=False), carry)
        carry = lax.fori_loop(near0, nchunks, functools.partial(attn_step, with_bias=True), carry)
        _, l, acc = carry
        oT_ref[hh * HEAD_DIM:(hh + 1) * HEAD_DIM, :] = acc / l

    o_ref[0] = oT_ref[...].T.astype(BF16)


def _merge_kernel(x_ref, mix_ref, att_ref, gmix_ref, wg_ref, bg_ref, wa_ref, wb_ref, wo_ref,
                  gffn_ref, wfg_ref, wfu_ref, wfd_ref, gfin_ref, o_ref, *, final_norm):
    x = x_ref[0]
    h = _rms(x, gmix_ref[...]).astype(BF16)
    gates = jax.nn.sigmoid(jnp.dot(h, wg_ref[...], preferred_element_type=F32) + bg_ref[...])
    y_a = jnp.dot(mix_ref[0], wa_ref[...], preferred_element_type=F32)
    y_b = jnp.dot(att_ref[0], wb_ref[...], preferred_element_type=F32)
    merged = gates[:, :D_MODEL] * y_a + gates[:, D_MODEL:] * y_b
    x1 = x + jnp.dot(merged.astype(BF16), wo_ref[...], preferred_element_type=F32)
    h2 = _rms(x1, gffn_ref[...]).astype(BF16)
    fg = jnp.dot(h2, wfg_ref[...], preferred_element_type=F32)
    fu = jnp.dot(h2, wfu_ref[...], preferred_element_type=F32)
    a = (jax.nn.silu(fg) * fu).astype(BF16)
    x2 = x1 + jnp.dot(a, wfd_ref[...], preferred_element_type=F32)
    o_ref[0] = _rms(x2, gfin_ref[...]) if final_norm else x2


def _const_spec(shape):
    nd = len(shape)
    return pl.BlockSpec(shape, lambda *_: (0,) * nd, pipeline_mode=pl.Buffered(1))


def kernel(x, g_mix, w_in, b_gate, conv_w, w_branch_a, w_branch_b, w_out, rel_bias, g_ffn,
           w_ffn_gate, w_ffn_up, w_ffn_down, g_final):
    bsz, seq, d = x.shape
    assert d == D_MODEL and seq % TM1 == 0 and seq % TQ == 0 and seq % TM3 == 0 and TQ == TK
    k_top = min(TOPK_MAX, seq // 4)
    nck = seq // TK
    d_ff = w_ffn_gate.shape[-1]
    cparams = functools.partial(pltpu.CompilerParams, vmem_limit_bytes=VMEM_LIMIT)

    for layer in range(g_mix.shape[0]):
        w = w_in[layer]
        o = np.cumsum([0, D_CONV, D_CONV, D_CONV, D_ATTN, HEAD_DIM, HEAD_DIM,
                       N_IDX_HEADS * IDX_DIM, IDX_DIM, N_IDX_HEADS, D_MODEL, D_MODEL])
        w_cb, w_cc, w_cx, w_q, w_k, w_v, w_iq, w_ik, w_iw, w_ga, w_gb = [
            w[:, int(o[j]):int(o[j + 1])] for j in range(11)]
        w_nat = jnp.concatenate([w_cb, w_cc, w_cx, w_k, w_ik], axis=1).astype(BF16)
        w_tr = jnp.concatenate([w_q * (HEAD_DIM ** -0.5), w_iq, w_v, w_iw], axis=1).T.astype(BF16)
        w_g = jnp.concatenate([w_ga, w_gb], axis=1).astype(BF16)
        n_nat, n_tr = w_nat.shape[1], w_tr.shape[0]

        mix, k_n, ik_n, qT, iqT, vT, iwT = pl.pallas_call(
            _proj_kernel,
            grid=(bsz, seq // TM1),
            in_specs=[
                pl.BlockSpec((1, TM1, D_MODEL), lambda b, i: (b, i, 0)),
                _const_spec((1, D_MODEL)),
                _const_spec((D_MODEL, n_nat)),
                _const_spec((n_tr, D_MODEL)),
                _const_spec((CONV_K, D_CONV)),
            ],
            out_specs=[
                pl.BlockSpec((1, TM1, D_CONV), lambda b, i: (b, i, 0)),
                pl.BlockSpec((1, TM1, HEAD_DIM), lambda b, i: (b, i, 0)),
                pl.BlockSpec((1, TM1, IDX_DIM), lambda b, i: (b, i, 0)),
                pl.BlockSpec((1, D_ATTN, TM1), lambda b, i: (b, 0, i)),
                pl.BlockSpec((1, N_IDX_HEADS * IDX_DIM, TM1), lambda b, i: (b, 0, i)),
                pl.BlockSpec((1, TM1 // TK, HEAD_DIM, TK), lambda b, i: (b, i, 0, 0)),
                pl.BlockSpec((1, N_IDX_HEADS, TM1), lambda b, i: (b, 0, i)),
            ],
            out_shape=[
                jax.ShapeDtypeStruct((bsz, seq, D_CONV), BF16),
                jax.ShapeDtypeStruct((bsz, seq, HEAD_DIM), BF16),
                jax.ShapeDtypeStruct((bsz, seq, IDX_DIM), BF16),
                jax.ShapeDtypeStruct((bsz, D_ATTN, seq), BF16),
                jax.ShapeDtypeStruct((bsz, N_IDX_HEADS * IDX_DIM, seq), BF16),
                jax.ShapeDtypeStruct((bsz, nck, HEAD_DIM, TK), BF16),
                jax.ShapeDtypeStruct((bsz, N_IDX_HEADS, seq), F32),
            ],
            scratch_shapes=[pltpu.VMEM((TM1 + 8, D_CONV), F32)],
            compiler_params=cparams(dimension_semantics=("arbitrary", "arbitrary")),
            name="proj",
        )(x, g_mix[layer][None, :], w_nat, w_tr, conv_w[layer])

        attn = pl.pallas_call(
            functools.partial(_dsa_kernel, k_top=k_top),
            grid=(bsz, seq // TQ),
            in_specs=[
                pl.BlockSpec(memory_space=pltpu.SMEM),
                pl.BlockSpec((1, N_IDX_HEADS * IDX_DIM, TQ), lambda b, i: (b, 0, i)),
                pl.BlockSpec((1, N_IDX_HEADS, TQ), lambda b, i: (b, 0, i)),
                pl.BlockSpec((1, seq, IDX_DIM), lambda b, i: (b, 0, 0)),
                pl.BlockSpec((1, D_ATTN, TQ), lambda b, i: (b, 0, i)),
                pl.BlockSpec((1, seq, HEAD_DIM), lambda b, i: (b, 0, 0)),
                pl.BlockSpec((1, nck, HEAD_DIM, TK), lambda b, i: (b, 0, 0, 0)),
            ],
            out_specs=pl.BlockSpec((1, TQ, D_ATTN), lambda b, i: (b, i, 0)),
            out_shape=jax.ShapeDtypeStruct((bsz, seq, D_ATTN), BF16),
            scratch_shapes=[
                pltpu.VMEM((nck, TK, TQ), I32),
                pltpu.VMEM((nck, TK, TQ), F32),
                pltpu.VMEM((2, N_HEADS, TK, TQ), F32),
                pltpu.VMEM((D_ATTN, TQ), F32),
            ],
            compiler_params=cparams(dimension_semantics=("arbitrary", "arbitrary")),
            name="dsa",
        )(rel_bias, iqT, iwT, ik_n, qT, k_n, vT)

        x = pl.pallas_call(
            functools.partial(_merge_kernel, final_norm=layer == g_mix.shape[0] - 1),
            grid=(bsz, seq // TM3),
            in_specs=[
                pl.BlockSpec((1, TM3, D_MODEL), lambda b, i: (b, i, 0)),
                pl.BlockSpec((1, TM3, D_CONV), lambda b, i: (b, i, 0)),
                pl.BlockSpec((1, TM3, D_ATTN), lambda b, i: (b, i, 0)),
                _const_spec((1, D_MODEL)),
                _const_spec((D_MODEL, 2 * D_MODEL)),
                _const_spec((1, 2 * D_MODEL)),
                _const_spec((D_CONV, D_MODEL)),
                _const_spec((D_ATTN, D_MODEL)),
                _const_spec((D_MODEL, D_MODEL)),
                _const_spec((1, D_MODEL)),
                _const_spec((D_MODEL, d_ff)),
                _const_spec((D_MODEL, d_ff)),
                _const_spec((d_ff, D_MODEL)),
                _const_spec((1, D_MODEL)),
            ],
            out_specs=pl.BlockSpec((1, TM3, D_MODEL), lambda b, i: (b, i, 0)),
            out_shape=jax.ShapeDtypeStruct((bsz, seq, D_MODEL), F32),
            compiler_params=cparams(dimension_semantics=("arbitrary", "arbitrary")),
            name="merge",
        )(x, mix, attn, g_mix[layer][None, :], w_g, b_gate[layer][None, :],
          w_branch_a[layer].astype(BF16), w_branch_b[layer].astype(BF16),
          w_out[layer].astype(BF16), g_ffn[layer][None, :],
          w_ffn_gate[layer].astype(BF16), w_ffn_up[layer].astype(BF16),
          w_ffn_down[layer].astype(BF16),
          g_final[None, :])
    return x
```

```python
import functools
import math

import numpy as np
import jax
import jax.numpy as jnp
from jax import lax
from jax.experimental import pallas as pl
from jax.experimental.pallas import tpu as pltpu

D_MODEL = 1024
D_CONV = 512
CONV_K = 3
N_HEADS = 8
HEAD_DIM = 64
D_ATTN = N_HEADS * HEAD_DIM
N_IDX_HEADS = 8
IDX_DIM = 64
TOPK_MAX = 256
N_BUCKETS = 32
MAX_EXACT = 16
MAX_DISTANCE = 128
EPS = 1e-6

F32 = jnp.float32
BF16 = jnp.bfloat16
I32 = jnp.int32

INT_MIN = -(2 ** 31)
NEG = -1e30

TM1 = 512
TQ = 256
TK = 256
TM3 = 256
V_EXTRA = 16
VMEM_LIMIT = 56 * 1024 * 1024


def _bucket_steps():
    n = np.arange(2 * MAX_DISTANCE)
    large = MAX_EXACT + (np.log(np.maximum(n, 1).astype(np.float32) / MAX_EXACT)
                         / math.log(MAX_DISTANCE / MAX_EXACT)
                         * (N_BUCKETS - MAX_EXACT)).astype(np.int32)
    large = np.minimum(large, N_BUCKETS - 1)
    b = np.where(n < MAX_EXACT, n, large)
    assert np.all(b[MAX_DISTANCE:] == N_BUCKETS - 1) and np.all(np.diff(b) >= 0)
    steps = [(int(i), int(b[i])) for i in range(1, len(b)) if b[i] != b[i - 1]]
    return int(b[0]), steps


def _rms(x, g):
    return x * lax.rsqrt(jnp.mean(x * x, axis=-1, keepdims=True) + EPS) * g


def _proj_kernel(x_ref, g_ref, wn_ref, wt_ref, cw_ref,
                 mix_ref, k_ref, ik_ref, qT_ref, iqT_ref, vT_ref, iwT_ref, ubuf_ref):
    i = pl.program_id(1)
    x = x_ref[0]
    h = _rms(x, g_ref[...]).astype(BF16)
    pn = jnp.dot(h, wn_ref[...], preferred_element_type=F32)
    pt = lax.dot_general(wt_ref[...], h, (((1,), (1,)), ((), ())),
                         preferred_element_type=F32)

    c_b = pn[:, 0:D_CONV]
    u = pn[:, D_CONV:2 * D_CONV] * pn[:, 2 * D_CONV:3 * D_CONV]

    @pl.when(i == 0)
    def _():
        ubuf_ref[0:8, :] = jnp.zeros((8, D_CONV), F32)

    ubuf_ref[8:8 + TM1, :] = u
    u1 = ubuf_ref[7:7 + TM1, :]
    u2 = ubuf_ref[6:6 + TM1, :]
    cw = cw_ref[...]
    y = cw[0:1, :] * u2 + cw[1:2, :] * u1 + cw[2:3, :] * u
    mix_ref[0] = (c_b * y).astype(BF16)
    ubuf_ref[0:8, :] = u[TM1 - 8:TM1, :]

    k_ref[0] = pn[:, 3 * D_CONV:3 * D_CONV + HEAD_DIM].astype(BF16)
    ik_ref[0] = pn[:, 3 * D_CONV + HEAD_DIM:3 * D_CONV + HEAD_DIM + IDX_DIM].astype(BF16)

    qT_ref[0] = pt[0:D_ATTN, :].astype(BF16)
    iqT_ref[0] = pt[D_ATTN:2 * D_ATTN, :].astype(BF16)
    vT = pt[2 * D_ATTN:2 * D_ATTN + HEAD_DIM, :].astype(BF16)
    ones_row = (lax.broadcasted_iota(I32, (V_EXTRA, TM1), 0) == 0).astype(F32).astype(BF16)
    vT = jnp.concatenate([vT, ones_row], axis=0)
    for j in range(TM1 // TK):
        vT_ref[0, j] = vT[:, j * TK:(j + 1) * TK]
    iw = pt[2 * D_ATTN + HEAD_DIM:2 * D_ATTN + HEAD_DIM + N_IDX_HEADS, :]
    iwT_ref[0] = iw * (N_IDX_HEADS ** -0.5 * IDX_DIM ** -0.5)


def _dsa_kernel(rb_ref, iqT_ref, iwT_ref, ik_ref, qT_ref, k_ref, vT_ref,
                o_ref, key_ref, negm_ref, bias_ref, oT_ref, m_ref, l_ref, lt_ref, *, k_top):
    b = pl.program_id(0)
    qi = pl.program_id(1)
    nchunks = qi + 1

    @pl.when((b == 0) & (qi == 0))
    def _():
        b0, steps = _bucket_steps()
        row = lax.broadcasted_iota(I32, (TK, TQ), 0)
        col = lax.broadcasted_iota(I32, (TK, TQ), 1)
        for off in range(2):
            dist = col - row + off * TK
            for hh in range(N_HEADS):
                val = jnp.full((TK, TQ), rb_ref[b0, hh], F32)
                for n0, bk in steps:
                    val = jnp.where(dist >= n0, rb_ref[bk, hh], val)
                bias_ref[off, hh] = val - rb_ref[N_BUCKETS - 1, hh]

    w = iwT_ref[0]

    def score_chunk(c, carry):
        ikc = ik_ref[0, pl.ds(pl.multiple_of(c * TK, TK), TK), :]
        acc = jnp.zeros((TK, TQ), F32)
        for hh in range(N_IDX_HEADS):
            d = jnp.dot(ikc, iqT_ref[0, hh * IDX_DIM:(hh + 1) * IDX_DIM, :],
                        preferred_element_type=F32)
            acc = acc + w[hh:hh + 1, :] * jnp.maximum(d, 0.0)
        bits = pltpu.bitcast(acc, I32)
        key = bits ^ ((bits >> 31) & 0x7FFFFFFF)
        row = lax.broadcasted_iota(I32, (TK, TQ), 0) + c * TK
        col = lax.broadcasted_iota(I32, (TK, TQ), 1) + qi * TQ
        key_ref[c] = jnp.where(row <= col, key, INT_MIN)
        return carry

    lax.fori_loop(0, nchunks, score_chunk, 0)

    def count(pred):
        def body(c, acc):
            m = pred(key_ref[c], c).astype(I32)
            return acc + jnp.sum(m.reshape(TK // 8, 8, TQ), axis=0)
        acc = lax.fori_loop(0, nchunks, body, jnp.zeros((8, TQ), I32))
        return jnp.sum(acc, axis=0, keepdims=True)

    def thr_step(it, thr):
        cand = thr + jnp.left_shift(jnp.int32(1), 31 - it)
        cnt = count(lambda kc, c: kc >= cand)
        return jnp.where(cnt >= k_top, cand, thr)

    thr = lax.fori_loop(0, 32, thr_step, jnp.full((1, TQ), INT_MIN, I32))
    thr = jnp.maximum(thr, INT_MIN + 1)
    cnt_ge = count(lambda kc, c: kc >= thr)
    cnt_gt = count(lambda kc, c: kc > thr)
    room = k_top - cnt_gt

    def key_index(c):
        return lax.broadcasted_iota(I32, (TK, TQ), 0) + c * TK

    def tie_search(_):
        def step(it, lim):
            cand = lim + jnp.left_shift(jnp.int32(1), 12 - it)
            cnt = count(lambda kc, c: (kc == thr) & (key_index(c) < cand))
            return jnp.where(cnt <= room, cand, lim)
        return lax.fori_loop(0, 13, step, jnp.zeros((1, TQ), I32))

    tie_lim = lax.cond(jnp.max(cnt_ge) > k_top, tie_search,
                       lambda _: jnp.full((1, TQ), 2 ** 13, I32), 0)

    def mask_chunk(c, carry):
        kc = key_ref[c]
        sel = (kc > thr) | ((kc == thr) & (key_index(c) < tie_lim))
        negm_ref[c] = jnp.where(sel, 0.0, NEG).astype(F32)
        return carry

    lax.fori_loop(0, nchunks, mask_chunk, 0)

    m_ref[...] = jnp.full((N_HEADS, TQ), NEG, F32)
    l_ref[...] = jnp.zeros((N_HEADS, TQ), F32)
    oT_ref[...] = jnp.zeros((D_ATTN, TQ), F32)

    def attn_step(c, carry, with_bias):
        kc = k_ref[0, pl.ds(pl.multiple_of(c * TK, TK), TK), :]
        vc = vT_ref[0, c]
        nm = negm_ref[c]
        m_old = m_ref[...]
        m_rows = []
        for hh in range(N_HEADS):
            rows = slice(hh * HEAD_DIM, (hh + 1) * HEAD_DIM)
            lt = jnp.dot(kc, qT_ref[0, rows, :], preferred_element_type=F32) + nm
            if with_bias:
                lt = lt + bias_ref[qi - c, hh]
            lt_ref[hh] = lt
            m_rows.append(jnp.max(lt, axis=0, keepdims=True))
        m_new = jnp.maximum(m_old, jnp.concatenate(m_rows, axis=0))
        alpha = jnp.exp(m_old - m_new)
        m_ref[...] = m_new
        l_rows = []
        for hh in range(N_HEADS):
            rows = slice(hh * HEAD_DIM, (hh + 1) * HEAD_DIM)
            p = jnp.exp(lt_ref[hh] - m_new[hh:hh + 1, :]).astype(BF16)
            pv = jnp.dot(vc, p, preferred_element_type=F32)
            oT_ref[rows, :] = alpha[hh:hh + 1, :] * oT_ref[rows, :] + pv[0:HEAD_DIM, :]
            l_rows.append(pv[HEAD_DIM:HEAD_DIM + 1, :])
        l_ref[...] = alpha * l_ref[...] + jnp.concatenate(l_rows, axis=0)
        return carry

    near0 = jnp.maximum(qi - 1, 0)
    lax.fori_loop(0, near0, functools.partial(attn_step, with_bias=False), 0)
    lax.fori_loop(near0, nchunks, functools.partial(attn_step, with_bias=True), 0)
    for hh in range(N_HEADS):
        rows = slice(hh * HEAD_DIM, (hh + 1) * HEAD_DIM)
        oT_ref[rows, :] = oT_ref[rows, :] / l_ref[hh:hh + 1, :]

    o_ref[0] = oT_ref[...].T.astype(BF16)


def _merge_kernel(x_ref, mix_ref, att_ref, gmix_ref, wg_ref, bg_ref, wa_ref, wb_ref, wo_ref,
                  gffn_ref, wfg_ref, wfu_ref, wfd_ref, gfin_ref, o_ref, *, final_norm):
    x = x_ref[0]
    h = _rms(x, gmix_ref[...]).astype(BF16)
    gates = jax.nn.sigmoid(jnp.dot(h, wg_ref[...], preferred_element_type=F32) + bg_ref[...])
    y_a = jnp.dot(mix_ref[0], wa_ref[...], preferred_element_type=F32)
    y_b = jnp.dot(att_ref[0], wb_ref[...], preferred_element_type=F32)
    merged = gates[:, :D_MODEL] * y_a + gates[:, D_MODEL:] * y_b
    x1 = x + jnp.dot(merged.astype(BF16), wo_ref[...], preferred_element_type=F32)
    h2 = _rms(x1, gffn_ref[...]).astype(BF16)
    fg = jnp.dot(h2, wfg_ref[...], preferred_element_type=F32)
    fu = jnp.dot(h2, wfu_ref[...], preferred_element_type=F32)
    a = (jax.nn.silu(fg) * fu).astype(BF16)
    x2 = x1 + jnp.dot(a, wfd_ref[...], preferred_element_type=F32)
    o_ref[0] = _rms(x2, gfin_ref[...]) if final_norm else x2


def _const_spec(shape):
    nd = len(shape)
    return pl.BlockSpec(shape, lambda *_: (0,) * nd, pipeline_mode=pl.Buffered(1))


def kernel(x, g_mix, w_in, b_gate, conv_w, w_branch_a, w_branch_b, w_out, rel_bias, g_ffn,
           w_ffn_gate, w_ffn_up, w_ffn_down, g_final):
    bsz, seq, d = x.shape
    assert d == D_MODEL and seq % TM1 == 0 and seq % TQ == 0 and seq % TM3 == 0 and TQ == TK
    k_top = min(TOPK_MAX, seq // 4)
    nck = seq // TK
    d_ff = w_ffn_gate.shape[-1]
    cparams = functools.partial(pltpu.CompilerParams, vmem_limit_bytes=VMEM_LIMIT)

    for layer in range(g_mix.shape[0]):
        w = w_in[layer]
        o = np.cumsum([0, D_CONV, D_CONV, D_CONV, D_ATTN, HEAD_DIM, HEAD_DIM,
                       N_IDX_HEADS * IDX_DIM, IDX_DIM, N_IDX_HEADS, D_MODEL, D_MODEL])
        w_cb, w_cc, w_cx, w_q, w_k, w_v, w_iq, w_ik, w_iw, w_ga, w_gb = [
            w[:, int(o[j]):int(o[j + 1])] for j in range(11)]
        w_nat = jnp.concatenate([w_cb, w_cc, w_cx, w_k, w_ik], axis=1).astype(BF16)
        w_tr = jnp.concatenate([w_q * (HEAD_DIM ** -0.5), w_iq, w_v, w_iw], axis=1).T.astype(BF16)
        w_g = jnp.concatenate([w_ga, w_gb], axis=1).astype(BF16)
        n_nat, n_tr = w_nat.shape[1], w_tr.shape[0]

        mix, k_n, ik_n, qT, iqT, vT, iwT = pl.pallas_call(
            _proj_kernel,
            grid=(bsz, seq // TM1),
            in_specs=[
                pl.BlockSpec((1, TM1, D_MODEL), lambda b, i: (b, i, 0)),
                _const_spec((1, D_MODEL)),
                _const_spec((D_MODEL, n_nat)),
                _const_spec((n_tr, D_MODEL)),
                _const_spec((CONV_K, D_CONV)),
            ],
            out_specs=[
                pl.BlockSpec((1, TM1, D_CONV), lambda b, i: (b, i, 0)),
                pl.BlockSpec((1, TM1, HEAD_DIM), lambda b, i: (b, i, 0)),
                pl.BlockSpec((1, TM1, IDX_DIM), lambda b, i: (b, i, 0)),
                pl.BlockSpec((1, D_ATTN, TM1), lambda b, i: (b, 0, i)),
                pl.BlockSpec((1, N_IDX_HEADS * IDX_DIM, TM1), lambda b, i: (b, 0, i)),
                pl.BlockSpec((1, TM1 // TK, HEAD_DIM + V_EXTRA, TK), lambda b, i: (b, i, 0, 0)),
                pl.BlockSpec((1, N_IDX_HEADS, TM1), lambda b, i: (b, 0, i)),
            ],
            out_shape=[
                jax.ShapeDtypeStruct((bsz, seq, D_CONV), BF16),
                jax.ShapeDtypeStruct((bsz, seq, HEAD_DIM), BF16),
                jax.ShapeDtypeStruct((bsz, seq, IDX_DIM), BF16),
                jax.ShapeDtypeStruct((bsz, D_ATTN, seq), BF16),
                jax.ShapeDtypeStruct((bsz, N_IDX_HEADS * IDX_DIM, seq), BF16),
                jax.ShapeDtypeStruct((bsz, nck, HEAD_DIM + V_EXTRA, TK), BF16),
                jax.ShapeDtypeStruct((bsz, N_IDX_HEADS, seq), F32),
            ],
            scratch_shapes=[pltpu.VMEM((TM1 + 8, D_CONV), F32)],
            compiler_params=cparams(dimension_semantics=("arbitrary", "arbitrary")),
            name="proj",
        )(x, g_mix[layer][None, :], w_nat, w_tr, conv_w[layer])

        attn = pl.pallas_call(
            functools.partial(_dsa_kernel, k_top=k_top),
            grid=(bsz, seq // TQ),
            in_specs=[
                pl.BlockSpec(memory_space=pltpu.SMEM),
                pl.BlockSpec((1, N_IDX_HEADS * IDX_DIM, TQ), lambda b, i: (b, 0, i)),
                pl.BlockSpec((1, N_IDX_HEADS, TQ), lambda b, i: (b, 0, i)),
                pl.BlockSpec((1, seq, IDX_DIM), lambda b, i: (b, 0, 0)),
                pl.BlockSpec((1, D_ATTN, TQ), lambda b, i: (b, 0, i)),
                pl.BlockSpec((1, seq, HEAD_DIM), lambda b, i: (b, 0, 0)),
                pl.BlockSpec((1, nck, HEAD_DIM + V_EXTRA, TK), lambda b, i: (b, 0, 0, 0)),
            ],
            out_specs=pl.BlockSpec((1, TQ, D_ATTN), lambda b, i: (b, i, 0)),
            out_shape=jax.ShapeDtypeStruct((bsz, seq, D_ATTN), BF16),
            scratch_shapes=[
                pltpu.VMEM((nck, TK, TQ), I32),
                pltpu.VMEM((nck, TK, TQ), F32),
                pltpu.VMEM((2, N_HEADS, TK, TQ), F32),
                pltpu.VMEM((D_ATTN, TQ), F32),
                pltpu.VMEM((N_HEADS, TQ), F32),
                pltpu.VMEM((N_HEADS, TQ), F32),
                pltpu.VMEM((N_HEADS, TK, TQ), F32),
            ],
            compiler_params=cparams(dimension_semantics=("arbitrary", "arbitrary")),
            name="dsa",
        )(rel_bias, iqT, iwT, ik_n, qT, k_n, vT)

        x = pl.pallas_call(
            functools.partial(_merge_kernel, final_norm=layer == g_mix.shape[0] - 1),
            grid=(bsz, seq // TM3),
            in_specs=[
                pl.BlockSpec((1, TM3, D_MODEL), lambda b, i: (b, i, 0)),
                pl.BlockSpec((1, TM3, D_CONV), lambda b, i: (b, i, 0)),
                pl.BlockSpec((1, TM3, D_ATTN), lambda b, i: (b, i, 0)),
                _const_spec((1, D_MODEL)),
                _const_spec((D_MODEL, 2 * D_MODEL)),
                _const_spec((1, 2 * D_MODEL)),
                _const_spec((D_CONV, D_MODEL)),
                _const_spec((D_ATTN, D_MODEL)),
                _const_spec((D_MODEL, D_MODEL)),
                _const_spec((1, D_MODEL)),
                _const_spec((D_MODEL, d_ff)),
                _const_spec((D_MODEL, d_ff)),
                _const_spec((d_ff, D_MODEL)),
                _const_spec((1, D_MODEL)),
            ],
            out_specs=pl.BlockSpec((1, TM3, D_MODEL), lambda b, i: (b, i, 0)),
            out_shape=jax.ShapeDtypeStruct((bsz, seq, D_MODEL), F32),
            compiler_params=cparams(dimension_semantics=("arbitrary", "arbitrary")),
            name="merge",
        )(x, mix, attn, g_mix[layer][None, :], w_g, b_gate[layer][None, :],
          w_branch_a[layer].astype(BF16), w_branch_b[layer].astype(BF16),
          w_out[layer].astype(BF16), g_ffn[layer][None, :],
          w_ffn_gate[layer].astype(BF16), w_ffn_up[layer].astype(BF16),
          w_ffn_down[layer].astype(BF16),
          g_final[None, :])
    return x
```

```python
import functools
import math

import numpy as np
import jax
import jax.numpy as jnp
from jax import lax
from jax.experimental import pallas as pl
from jax.experimental.pallas import tpu as pltpu

D_MODEL = 1024
D_CONV = 512
CONV_K = 3
N_HEADS = 8
HEAD_DIM = 64
D_ATTN = N_HEADS * HEAD_DIM
N_IDX_HEADS = 8
IDX_DIM = 64
TOPK_MAX = 256
N_BUCKETS = 32
MAX_EXACT = 16
MAX_DISTANCE = 128
EPS = 1e-6

F32 = jnp.float32
BF16 = jnp.bfloat16
I32 = jnp.int32
I16 = jnp.int16

INT_MIN = -(2 ** 31)
I16_MIN = -(2 ** 15)
NEG = -1e30
LOG2E = math.log2(math.e)

TM1 = 512
TQ = 256
TK = 256
TM3 = 256
V_EXTRA = 16
VMEM_LIMIT = 56 * 1024 * 1024


def _bucket_steps():
    n = np.arange(2 * MAX_DISTANCE)
    large = MAX_EXACT + (np.log(np.maximum(n, 1).astype(np.float32) / MAX_EXACT)
                         / math.log(MAX_DISTANCE / MAX_EXACT)
                         * (N_BUCKETS - MAX_EXACT)).astype(np.int32)
    large = np.minimum(large, N_BUCKETS - 1)
    b = np.where(n < MAX_EXACT, n, large)
    assert np.all(b[MAX_DISTANCE:] == N_BUCKETS - 1) and np.all(np.diff(b) >= 0)
    steps = [(int(i), int(b[i])) for i in range(1, len(b)) if b[i] != b[i - 1]]
    return int(b[0]), steps


def _fold_rows(a, rows):
    parts = [a[i:i + rows, :] for i in range(0, a.shape[0], rows)]
    while len(parts) > 1:
        parts = [parts[i] + parts[i + 1] for i in range(0, len(parts) - 1, 2)] + (
            [parts[-1]] if len(parts) % 2 else [])
    return parts[0]


def _rms(x, g):
    return x * lax.rsqrt(jnp.mean(x * x, axis=-1, keepdims=True) + EPS) * g


def _proj_kernel(x_ref, g_ref, wn_ref, wt_ref, cw_ref,
                 mix_ref, k_ref, ik_ref, qT_ref, iqT_ref, vT_ref, iwT_ref, ubuf_ref):
    i = pl.program_id(1)
    x = x_ref[0]
    h = _rms(x, g_ref[...]).astype(BF16)
    pn = jnp.dot(h, wn_ref[...], preferred_element_type=F32)
    pt = lax.dot_general(wt_ref[...], h, (((1,), (1,)), ((), ())),
                         preferred_element_type=F32)

    c_b = pn[:, 0:D_CONV]
    u = pn[:, D_CONV:2 * D_CONV] * pn[:, 2 * D_CONV:3 * D_CONV]

    @pl.when(i == 0)
    def _():
        ubuf_ref[0:8, :] = jnp.zeros((8, D_CONV), F32)

    ubuf_ref[8:8 + TM1, :] = u
    u1 = ubuf_ref[7:7 + TM1, :]
    u2 = ubuf_ref[6:6 + TM1, :]
    cw = cw_ref[...]
    y = cw[0:1, :] * u2 + cw[1:2, :] * u1 + cw[2:3, :] * u
    mix_ref[0] = (c_b * y).astype(BF16)
    ubuf_ref[0:8, :] = u[TM1 - 8:TM1, :]

    k_ref[0] = pn[:, 3 * D_CONV:3 * D_CONV + HEAD_DIM].astype(BF16)
    ik_ref[0] = pn[:, 3 * D_CONV + HEAD_DIM:3 * D_CONV + HEAD_DIM + IDX_DIM].astype(BF16)

    qT_ref[0] = pt[0:D_ATTN, :].astype(BF16)
    iqT_ref[0] = pt[D_ATTN:2 * D_ATTN, :].astype(BF16)
    vT = pt[2 * D_ATTN:2 * D_ATTN + HEAD_DIM, :].astype(BF16)
    ones_row = (lax.broadcasted_iota(I32, (V_EXTRA, TM1), 0) == 0).astype(F32).astype(BF16)
    vT = jnp.concatenate([vT, ones_row], axis=0)
    for j in range(TM1 // TK):
        vT_ref[0, j] = vT[:, j * TK:(j + 1) * TK]
    iw = pt[2 * D_ATTN + HEAD_DIM:2 * D_ATTN + HEAD_DIM + N_IDX_HEADS, :]
    iwT_ref[0] = iw * (N_IDX_HEADS ** -0.5 * IDX_DIM ** -0.5)


def _dsa_kernel(rb_ref, iqT_ref, iwT_ref, ik_ref, qT_ref, k_ref, vT_ref,
                o_ref, key_ref, khi_ref, klo_ref, add_ref, bias_ref, oT_ref, m_ref, l_ref, al_ref,
                lta_ref, ltb_ref, *, k_top):
    b = pl.program_id(0)
    qi = pl.program_id(1)
    nchunks = qi + 1
    nck = key_ref.shape[0]

    @pl.when((b == 0) & (qi == 0))
    def _():
        b0, steps = _bucket_steps()
        row = lax.broadcasted_iota(I32, (TK, TQ), 0)
        col = lax.broadcasted_iota(I32, (TK, TQ), 1)
        for off in range(2):
            dist = col - row + off * TK
            for hh in range(N_HEADS):
                val = jnp.full((TK, TQ), rb_ref[b0, hh], F32)
                for n0, bk in steps:
                    val = jnp.where(dist >= n0, rb_ref[bk, hh], val)
                bias_ref[off, hh] = (val - rb_ref[N_BUCKETS - 1, hh]) * LOG2E

    w = iwT_ref[0]

    def score_chunk(c, carry):
        ikc = ik_ref[0, pl.ds(pl.multiple_of(c * TK, TK), TK), :]
        acc = jnp.zeros((TK, TQ), F32)
        for hh in range(N_IDX_HEADS):
            d = jnp.dot(ikc, iqT_ref[0, hh * IDX_DIM:(hh + 1) * IDX_DIM, :],
                        preferred_element_type=F32)
            acc = acc + w[hh:hh + 1, :] * jnp.maximum(d, 0.0)
        bits = pltpu.bitcast(acc, I32)
        key = bits ^ ((bits >> 31) & 0x7FFFFFFF)
        row = lax.broadcasted_iota(I32, (TK, TQ), 0) + c * TK
        col = lax.broadcasted_iota(I32, (TK, TQ), 1) + qi * TQ
        key = jnp.where(row <= col, key, INT_MIN)
        key_ref[c] = key
        khi_ref[c] = (key >> 16).astype(I16)
        return carry

    lax.fori_loop(0, nchunks, score_chunk, 0)

    def count(pred):
        def body(c, acc):
            m = pred(key_ref[c], c).astype(I32)
            return acc + jnp.sum(m.reshape(TK // 8, 8, TQ), axis=0)
        acc = lax.fori_loop(0, nchunks, body, jnp.zeros((8, TQ), I32))
        return jnp.sum(acc, axis=0, keepdims=True)

    def count16(ref, pred):
        def body(c, acc):
            m = jnp.where(pred(ref[c]), jnp.int16(1), jnp.int16(0))
            return acc + _fold_rows(m, 16)
        acc = lax.fori_loop(0, nchunks, body, jnp.zeros((16, TQ), I16))
        return jnp.sum(acc.astype(I32), axis=0, keepdims=True)

    def select16(ref, want):
        def step(it, v):
            cand = v + jnp.left_shift(jnp.int32(1), 15 - it)
            cand16 = cand.astype(I16)
            cnt = count16(ref, lambda kc: kc >= cand16)
            return jnp.where(cnt >= want, cand, v)
        return lax.fori_loop(0, 16, step, jnp.full((1, TQ), I16_MIN, I32))

    thr_hi = select16(khi_ref, k_top)
    thr_hi16 = thr_hi.astype(I16)
    room_hi = k_top - count16(khi_ref, lambda kc: kc > thr_hi16)

    def low_chunk(c, carry):
        lo = ((key_ref[c] & 0xFFFF) - 2 ** 15).astype(I16)
        klo_ref[c] = jnp.where(khi_ref[c] == thr_hi16, lo, jnp.int16(I16_MIN))
        return carry

    lax.fori_loop(0, nchunks, low_chunk, 0)
    thr_lo = select16(klo_ref, room_hi)
    thr = thr_hi * 2 ** 16 + (thr_lo + 2 ** 15)
    thr = jnp.maximum(thr, INT_MIN + 1)
    cnt_ge = count(lambda kc, c: kc >= thr)
    cnt_gt = count(lambda kc, c: kc > thr)
    room = k_top - cnt_gt

    def key_index(c):
        return lax.broadcasted_iota(I32, (TK, TQ), 0) + c * TK

    def tie_search(_):
        def step(it, lim):
            cand = lim + jnp.left_shift(jnp.int32(1), 12 - it)
            cnt = count(lambda kc, c: (kc == thr) & (key_index(c) < cand))
            return jnp.where(cnt <= room, cand, lim)
        return lax.fori_loop(0, 13, step, jnp.zeros((1, TQ), I32))

    tie_lim = lax.cond(jnp.max(cnt_ge) > k_top, tie_search,
                       lambda _: jnp.full((1, TQ), 2 ** 13, I32), 0)

    def mask_chunk(c, carry):
        kc = key_ref[c]
        sel = (kc > thr) | ((kc == thr) & (key_index(c) < tie_lim))
        add_ref[c] = jnp.where(sel, 0.0, NEG).astype(F32)
        return carry

    lax.fori_loop(0, nchunks, mask_chunk, 0)
    for off in range(2):
        @pl.when(qi >= off)
        def _():
            nm = add_ref[qi - off]
            for hh in range(N_HEADS):
                add_ref[nck + off * N_HEADS + hh] = nm + bias_ref[off, hh]

    def pipe_step(c1, c2, dst_ref, src_ref):
        m_cur = m_ref[...]
        m_rows, l_rows = [], []
        if c1 is not None:
            kc = k_ref[0, pl.ds(pl.multiple_of(c1 * TK, TK), TK), :]
            near = c1 >= qi - 1
        if c2 is not None:
            vc = vT_ref[0, c2]
            alpha = al_ref[...]
        for hh in range(N_HEADS):
            rows = slice(hh * HEAD_DIM, (hh + 1) * HEAD_DIM)
            if c1 is not None:
                tile = jnp.where(near, nck + (qi - c1) * N_HEADS + hh, c1)
                lt = jnp.dot(kc, qT_ref[0, rows, :], preferred_element_type=F32) + add_ref[tile]
                dst_ref[hh] = lt
                m_rows.append(jnp.max(lt, axis=0, keepdims=True))
            if c2 is not None:
                p = jnp.exp2(src_ref[hh] - m_cur[hh:hh + 1, :]).astype(BF16)
                pv = jnp.dot(vc, p, preferred_element_type=F32)
                oT_ref[rows, :] = alpha[hh:hh + 1, :] * oT_ref[rows, :] + pv[0:HEAD_DIM, :]
                l_rows.append(pv[HEAD_DIM:HEAD_DIM + 1, :])
        if c2 is not None:
            l_ref[...] = alpha * l_ref[...] + jnp.concatenate(l_rows, axis=0)
        if c1 is not None:
            m_new = jnp.maximum(m_cur, jnp.concatenate(m_rows, axis=0))
            al_ref[...] = jnp.exp2(m_cur - m_new)
            m_ref[...] = m_new

    m_ref[...] = jnp.full((N_HEADS, TQ), NEG, F32)
    l_ref[...] = jnp.zeros((N_HEADS, TQ), F32)
    oT_ref[...] = jnp.zeros((D_ATTN, TQ), F32)
    pipe_step(0, None, lta_ref, None)
    n_steps = nchunks - 1

    def two_steps(j, carry):
        c = 2 * j
        pipe_step(c + 1, c, ltb_ref, lta_ref)
        pipe_step(c + 2, c + 1, lta_ref, ltb_ref)
        return carry

    lax.fori_loop(0, n_steps // 2, two_steps, 0)

    @pl.when(n_steps % 2 == 1)
    def _():
        pipe_step(n_steps, n_steps - 1, ltb_ref, lta_ref)
        pipe_step(None, n_steps, None, ltb_ref)

    @pl.when(n_steps % 2 == 0)
    def _():
        pipe_step(None, n_steps, None, lta_ref)

    for hh in range(N_HEADS):
        rows = slice(hh * HEAD_DIM, (hh + 1) * HEAD_DIM)
        oT_ref[rows, :] = oT_ref[rows, :] / l_ref[hh:hh + 1, :]

    o_ref[0] = oT_ref[...].T.astype(BF16)


def _merge_kernel(x_ref, mix_ref, att_ref, gmix_ref, wg_ref, bg_ref, wa_ref, wb_ref, wo_ref,
                  gffn_ref, wfg_ref, wfu_ref, wfd_ref, gfin_ref, o_ref, *, final_norm):
    x = x_ref[0]
    h = _rms(x, gmix_ref[...]).astype(BF16)
    gates = jax.nn.sigmoid(jnp.dot(h, wg_ref[...], preferred_element_type=F32) + bg_ref[...])
    y_a = jnp.dot(mix_ref[0], wa_ref[...], preferred_element_type=F32)
    y_b = jnp.dot(att_ref[0], wb_ref[...], preferred_element_type=F32)
    merged = gates[:, :D_MODEL] * y_a + gates[:, D_MODEL:] * y_b
    x1 = x + jnp.dot(merged.astype(BF16), wo_ref[...], preferred_element_type=F32)
    h2 = _rms(x1, gffn_ref[...]).astype(BF16)
    fg = jnp.dot(h2, wfg_ref[...], preferred_element_type=F32)
    fu = jnp.dot(h2, wfu_ref[...], preferred_element_type=F32)
    a = (jax.nn.silu(fg) * fu).astype(BF16)
    x2 = x1 + jnp.dot(a, wfd_ref[...], preferred_element_type=F32)
    o_ref[0] = _rms(x2, gfin_ref[...]) if final_norm else x2


def _const_spec(shape):
    nd = len(shape)
    return pl.BlockSpec(shape, lambda *_: (0,) * nd, pipeline_mode=pl.Buffered(1))


def kernel(x, g_mix, w_in, b_gate, conv_w, w_branch_a, w_branch_b, w_out, rel_bias, g_ffn,
           w_ffn_gate, w_ffn_up, w_ffn_down, g_final):
    bsz, seq, d = x.shape
    assert d == D_MODEL and seq % TM1 == 0 and seq % TQ == 0 and seq % TM3 == 0 and TQ == TK
    k_top = min(TOPK_MAX, seq // 4)
    nck = seq // TK
    d_ff = w_ffn_gate.shape[-1]
    cparams = functools.partial(pltpu.CompilerParams, vmem_limit_bytes=VMEM_LIMIT)

    for layer in range(g_mix.shape[0]):
        w = w_in[layer]
        o = np.cumsum([0, D_CONV, D_CONV, D_CONV, D_ATTN, HEAD_DIM, HEAD_DIM,
                       N_IDX_HEADS * IDX_DIM, IDX_DIM, N_IDX_HEADS, D_MODEL, D_MODEL])
        w_cb, w_cc, w_cx, w_q, w_k, w_v, w_iq, w_ik, w_iw, w_ga, w_gb = [
            w[:, int(o[j]):int(o[j + 1])] for j in range(11)]
        w_nat = jnp.concatenate([w_cb, w_cc, w_cx, w_k, w_ik], axis=1).astype(BF16)
        w_tr = jnp.concatenate([w_q * (HEAD_DIM ** -0.5 * LOG2E), w_iq, w_v, w_iw],
                               axis=1).T.astype(BF16)
        w_g = jnp.concatenate([w_ga, w_gb], axis=1).astype(BF16)
        n_nat, n_tr = w_nat.shape[1], w_tr.shape[0]

        mix, k_n, ik_n, qT, iqT, vT, iwT = pl.pallas_call(
            _proj_kernel,
            grid=(bsz, seq // TM1),
            in_specs=[
                pl.BlockSpec((1, TM1, D_MODEL), lambda b, i: (b, i, 0)),
                _const_spec((1, D_MODEL)),
                _const_spec((D_MODEL, n_nat)),
                _const_spec((n_tr, D_MODEL)),
                _const_spec((CONV_K, D_CONV)),
            ],
            out_specs=[
                pl.BlockSpec((1, TM1, D_CONV), lambda b, i: (b, i, 0)),
                pl.BlockSpec((1, TM1, HEAD_DIM), lambda b, i: (b, i, 0)),
                pl.BlockSpec((1, TM1, IDX_DIM), lambda b, i: (b, i, 0)),
                pl.BlockSpec((1, D_ATTN, TM1), lambda b, i: (b, 0, i)),
                pl.BlockSpec((1, N_IDX_HEADS * IDX_DIM, TM1), lambda b, i: (b, 0, i)),
                pl.BlockSpec((1, TM1 // TK, HEAD_DIM + V_EXTRA, TK), lambda b, i: (b, i, 0, 0)),
                pl.BlockSpec((1, N_IDX_HEADS, TM1), lambda b, i: (b, 0, i)),
            ],
            out_shape=[
                jax.ShapeDtypeStruct((bsz, seq, D_CONV), BF16),
                jax.ShapeDtypeStruct((bsz, seq, HEAD_DIM), BF16),
                jax.ShapeDtypeStruct((bsz, seq, IDX_DIM), BF16),
                jax.ShapeDtypeStruct((bsz, D_ATTN, seq), BF16),
                jax.ShapeDtypeStruct((bsz, N_IDX_HEADS * IDX_DIM, seq), BF16),
                jax.ShapeDtypeStruct((bsz, nck, HEAD_DIM + V_EXTRA, TK), BF16),
                jax.ShapeDtypeStruct((bsz, N_IDX_HEADS, seq), F32),
            ],
            scratch_shapes=[pltpu.VMEM((TM1 + 8, D_CONV), F32)],
            compiler_params=cparams(dimension_semantics=("arbitrary", "arbitrary")),
            name="proj",
        )(x, g_mix[layer][None, :], w_nat, w_tr, conv_w[layer])

        attn = pl.pallas_call(
            functools.partial(_dsa_kernel, k_top=k_top),
            grid=(bsz, seq // TQ),
            in_specs=[
                pl.BlockSpec(memory_space=pltpu.SMEM),
                pl.BlockSpec((1, N_IDX_HEADS * IDX_DIM, TQ), lambda b, i: (b, 0, i)),
                pl.BlockSpec((1, N_IDX_HEADS, TQ), lambda b, i: (b, 0, i)),
                pl.BlockSpec((1, seq, IDX_DIM), lambda b, i: (b, 0, 0)),
                pl.BlockSpec((1, D_ATTN, TQ), lambda b, i: (b, 0, i)),
                pl.BlockSpec((1, seq, HEAD_DIM), lambda b, i: (b, 0, 0)),
                pl.BlockSpec((1, nck, HEAD_DIM + V_EXTRA, TK), lambda b, i: (b, 0, 0, 0)),
            ],
            out_specs=pl.BlockSpec((1, TQ, D_ATTN), lambda b, i: (b, i, 0)),
            out_shape=jax.ShapeDtypeStruct((bsz, seq, D_ATTN), BF16),
            scratch_shapes=[
                pltpu.VMEM((nck, TK, TQ), I32),
                pltpu.VMEM((nck, TK, TQ), I16),
                pltpu.VMEM((nck, TK, TQ), I16),
                pltpu.VMEM((nck + 2 * N_HEADS, TK, TQ), F32),
                pltpu.VMEM((2, N_HEADS, TK, TQ), F32),
                pltpu.VMEM((D_ATTN, TQ), F32),
                pltpu.VMEM((N_HEADS, TQ), F32),
                pltpu.VMEM((N_HEADS, TQ), F32),
                pltpu.VMEM((N_HEADS, TQ), F32),
                pltpu.VMEM((N_HEADS, TK, TQ), F32),
                pltpu.VMEM((N_HEADS, TK, TQ), F32),
            ],
            compiler_params=cparams(dimension_semantics=("arbitrary", "arbitrary")),
            name="dsa",
        )(rel_bias, iqT, iwT, ik_n, qT, k_n, vT)

        x = pl.pallas_call(
            functools.partial(_merge_kernel, final_norm=layer == g_mix.shape[0] - 1),
            grid=(bsz, seq // TM3),
            in_specs=[
                pl.BlockSpec((1, TM3, D_MODEL), lambda b, i: (b, i, 0)),
                pl.BlockSpec((1, TM3, D_CONV), lambda b, i: (b, i, 0)),
                pl.BlockSpec((1, TM3, D_ATTN), lambda b, i: (b, i, 0)),
                _const_spec((1, D_MODEL)),
                _const_spec((D_MODEL, 2 * D_MODEL)),
                _const_spec((1, 2 * D_MODEL)),
                _const_spec((D_CONV, D_MODEL)),
                _const_spec((D_ATTN, D_MODEL)),
                _const_spec((D_MODEL, D_MODEL)),
                _const_spec((1, D_MODEL)),
                _const_spec((D_MODEL, d_ff)),
                _const_spec((D_MODEL, d_ff)),
                _const_spec((d_ff, D_MODEL)),
                _const_spec((1, D_MODEL)),
            ],
            out_specs=pl.BlockSpec((1, TM3, D_MODEL), lambda b, i: (b, i, 0)),
            out_shape=jax.ShapeDtypeStruct((bsz, seq, D_MODEL), F32),
            compiler_params=cparams(dimension_semantics=("arbitrary", "arbitrary")),
            name="merge",
        )(x, mix, attn, g_mix[layer][None, :], w_g, b_gate[layer][None, :],
          w_branch_a[layer].astype(BF16), w_branch_b[layer].astype(BF16),
          w_out[layer].astype(BF16), g_ffn[layer][None, :],
          w_ffn_gate[layer].astype(BF16), w_ffn_up[layer].astype(BF16),
          w_ffn_down[layer].astype(BF16),
          g_final[None, :])
    return x
```

```python
import functools
import math

import numpy as np
import jax
import jax.numpy as jnp
from jax import lax
from jax.experimental import pallas as pl
from jax.experimental.pallas import tpu as pltpu

D_MODEL = 1024
D_CONV = 512
CONV_K = 3
N_HEADS = 8
HEAD_DIM = 64
D_ATTN = N_HEADS * HEAD_DIM
N_IDX_HEADS = 8
IDX_DIM = 64
TOPK_MAX = 256
N_BUCKETS = 32
MAX_EXACT = 16
MAX_DISTANCE = 128
EPS = 1e-6

F32 = jnp.float32
BF16 = jnp.bfloat16
I32 = jnp.int32
I16 = jnp.int16

INT_MIN = -(2 ** 31)
I16_MIN = -(2 ** 15)
NEG = -1e30
LOG2E = math.log2(math.e)

TM1 = 512
TQ = 256
TK = 256
TM3 = 256
V_EXTRA = 16
VMEM_LIMIT = 56 * 1024 * 1024


def _bucket_steps():
    n = np.arange(2 * MAX_DISTANCE)
    large = MAX_EXACT + (np.log(np.maximum(n, 1).astype(np.float32) / MAX_EXACT)
                         / math.log(MAX_DISTANCE / MAX_EXACT)
                         * (N_BUCKETS - MAX_EXACT)).astype(np.int32)
    large = np.minimum(large, N_BUCKETS - 1)
    b = np.where(n < MAX_EXACT, n, large)
    assert np.all(b[MAX_DISTANCE:] == N_BUCKETS - 1) and np.all(np.diff(b) >= 0)
    steps = [(int(i), int(b[i])) for i in range(1, len(b)) if b[i] != b[i - 1]]
    return int(b[0]), steps


def _fold_rows(a, rows):
    parts = [a[i:i + rows, :] for i in range(0, a.shape[0], rows)]
    while len(parts) > 1:
        parts = [parts[i] + parts[i + 1] for i in range(0, len(parts) - 1, 2)] + (
            [parts[-1]] if len(parts) % 2 else [])
    return parts[0]


def _rms(x, g):
    return x * lax.rsqrt(jnp.mean(x * x, axis=-1, keepdims=True) + EPS) * g


def _proj_kernel(x_ref, g_ref, wn_ref, wt_ref, cw_ref,
                 mix_ref, k_ref, ik_ref, qT_ref, iqT_ref, vT_ref, iwT_ref, ubuf_ref):
    i = pl.program_id(1)
    x = x_ref[0]
    h = _rms(x, g_ref[...]).astype(BF16)
    pn = jnp.dot(h, wn_ref[...], preferred_element_type=F32)
    pt = lax.dot_general(wt_ref[...], h, (((1,), (1,)), ((), ())),
                         preferred_element_type=F32)

    c_b = pn[:, 0:D_CONV]
    u = pn[:, D_CONV:2 * D_CONV] * pn[:, 2 * D_CONV:3 * D_CONV]

    @pl.when(i == 0)
    def _():
        ubuf_ref[0:8, :] = jnp.zeros((8, D_CONV), F32)

    ubuf_ref[8:8 + TM1, :] = u
    u1 = ubuf_ref[7:7 + TM1, :]
    u2 = ubuf_ref[6:6 + TM1, :]
    cw = cw_ref[...]
    y = cw[0:1, :] * u2 + cw[1:2, :] * u1 + cw[2:3, :] * u
    mix_ref[0] = (c_b * y).astype(BF16)
    ubuf_ref[0:8, :] = u[TM1 - 8:TM1, :]

    k_ref[0] = pn[:, 3 * D_CONV:3 * D_CONV + HEAD_DIM].astype(BF16)
    ik_ref[0] = pn[:, 3 * D_CONV + HEAD_DIM:3 * D_CONV + HEAD_DIM + IDX_DIM].astype(BF16)

    qT_ref[0] = pt[0:D_ATTN, :].astype(BF16)
    iqT_ref[0] = pt[D_ATTN:2 * D_ATTN, :].astype(BF16)
    vT = pt[2 * D_ATTN:2 * D_ATTN + HEAD_DIM, :].astype(BF16)
    ones_row = (lax.broadcasted_iota(I32, (V_EXTRA, TM1), 0) == 0).astype(F32).astype(BF16)
    vT = jnp.concatenate([vT, ones_row], axis=0)
    for j in range(TM1 // TK):
        vT_ref[0, j] = vT[:, j * TK:(j + 1) * TK]
    iw = pt[2 * D_ATTN + HEAD_DIM:2 * D_ATTN + HEAD_DIM + N_IDX_HEADS, :]
    iwT_ref[0] = iw * (N_IDX_HEADS ** -0.5 * IDX_DIM ** -0.5)


def _dsa_kernel(rb_ref, iqT_ref, iwT_ref, ik_ref, qT_ref, k_ref, vT_ref,
                o_ref, key_ref, khi_ref, klo_ref, add_ref, bias_ref, oT_ref, m_ref, l_ref, al_ref,
                lta_ref, ltb_ref, *, k_top):
    b = pl.program_id(0)
    qi = pl.program_id(1)
    nchunks = qi + 1
    nck = key_ref.shape[0]

    @pl.when((b == 0) & (qi == 0))
    def _():
        b0, steps = _bucket_steps()
        row = lax.broadcasted_iota(I32, (TK, TQ), 0)
        col = lax.broadcasted_iota(I32, (TK, TQ), 1)
        for off in range(2):
            dist = col - row + off * TK
            for hh in range(N_HEADS):
                val = jnp.full((TK, TQ), rb_ref[b0, hh], F32)
                for n0, bk in steps:
                    val = jnp.where(dist >= n0, rb_ref[bk, hh], val)
                bias_ref[off, hh] = (val - rb_ref[N_BUCKETS - 1, hh]) * LOG2E

    w = iwT_ref[0]
    sub = TK // 4

    def score_step(c1, c2, dst_ref, src_ref):
        if c1 is not None:
            ikc = ik_ref[0, pl.ds(pl.multiple_of(c1 * TK, TK), TK), :]
        if c2 is not None:
            col = lax.broadcasted_iota(I32, (sub, TQ), 1) + qi * TQ
        for g in range(4):
            if c1 is not None:
                for hh in range(2 * g, 2 * g + 2):
                    dst_ref[hh] = jnp.dot(ikc, iqT_ref[0, hh * IDX_DIM:(hh + 1) * IDX_DIM, :],
                                          preferred_element_type=F32)
            if c2 is not None:
                rows = slice(g * sub, (g + 1) * sub)
                acc = jnp.zeros((sub, TQ), F32)
                for hh in range(N_IDX_HEADS):
                    acc = acc + w[hh:hh + 1, :] * jnp.maximum(src_ref[hh, rows, :], 0.0)
                bits = pltpu.bitcast(acc, I32)
                key = bits ^ ((bits >> 31) & 0x7FFFFFFF)
                row = lax.broadcasted_iota(I32, (sub, TQ), 0) + (c2 * TK + g * sub)
                key = jnp.where(row <= col, key, INT_MIN)
                key_ref[c2, rows, :] = key
                khi_ref[c2, rows, :] = (key >> 16).astype(I16)

    def run_pipeline(step, n_steps):
        step(0, None, lta_ref, None)

        def two_steps(j, carry):
            c = 2 * j
            step(c + 1, c, ltb_ref, lta_ref)
            step(c + 2, c + 1, lta_ref, ltb_ref)
            return carry

        lax.fori_loop(0, n_steps // 2, two_steps, 0)

        @pl.when(n_steps % 2 == 1)
        def _():
            step(n_steps, n_steps - 1, ltb_ref, lta_ref)
            step(None, n_steps, None, ltb_ref)

        @pl.when(n_steps % 2 == 0)
        def _():
            step(None, n_steps, None, lta_ref)

    run_pipeline(score_step, nchunks - 1)

    @pl.when(nchunks % 2 == 1)
    def _():
        khi_ref[nchunks] = jnp.full((TK, TQ), I16_MIN, I16)
        klo_ref[nchunks] = jnp.full((TK, TQ), I16_MIN, I16)

    def count(pred):
        def body(c, acc):
            m = pred(key_ref[c], c).astype(I32)
            return acc + jnp.sum(m.reshape(TK // 8, 8, TQ), axis=0)
        acc = lax.fori_loop(0, nchunks, body, jnp.zeros((8, TQ), I32))
        return jnp.sum(acc, axis=0, keepdims=True)

    def count16(ref, pred):
        def ones(c):
            return _fold_rows(jnp.where(pred(ref[c]), jnp.int16(1), jnp.int16(0)), 16)

        def pair(j, acc):
            return acc + (ones(2 * j) + ones(2 * j + 1))

        acc = lax.fori_loop(0, (nchunks + 1) // 2, pair, jnp.zeros((16, TQ), I16))
        return jnp.sum(acc.astype(I32), axis=0, keepdims=True)

    def select16(ref, want):
        def step(it, v):
            cand = v + jnp.left_shift(jnp.int32(1), 15 - it)
            cand16 = cand.astype(I16)
            cnt = count16(ref, lambda kc: kc >= cand16)
            return jnp.where(cnt >= want, cand, v)
        return lax.fori_loop(0, 16, step, jnp.full((1, TQ), I16_MIN, I32))

    thr_hi = select16(khi_ref, k_top)
    thr_hi16 = thr_hi.astype(I16)
    room_hi = k_top - count16(khi_ref, lambda kc: kc > thr_hi16)

    def low_chunk(c, carry):
        lo = ((key_ref[c] & 0xFFFF) - 2 ** 15).astype(I16)
        klo_ref[c] = jnp.where(khi_ref[c] == thr_hi16, lo, jnp.int16(I16_MIN))
        return carry

    lax.fori_loop(0, nchunks, low_chunk, 0)
    thr_lo = select16(klo_ref, room_hi)
    thr_lo16 = thr_lo.astype(I16)
    thr = thr_hi * 2 ** 16 + (thr_lo + 2 ** 15)
    cnt_gt = (k_top - room_hi) + count16(klo_ref, lambda kc: kc > thr_lo16)
    cnt_ge = (k_top - room_hi) + count16(klo_ref, lambda kc: kc >= thr_lo16)
    room = k_top - cnt_gt
    excess = (cnt_ge > k_top) & (thr > INT_MIN)
    thr = jnp.maximum(thr, INT_MIN + 1)

    def key_index(c):
        return lax.broadcasted_iota(I32, (TK, TQ), 0) + c * TK

    def tie_search(_):
        def step(it, lim):
            cand = lim + jnp.left_shift(jnp.int32(1), 12 - it)
            cnt = count(lambda kc, c: (kc == thr) & (key_index(c) < cand))
            return jnp.where(cnt <= room, cand, lim)
        return lax.fori_loop(0, 13, step, jnp.zeros((1, TQ), I32))

    tie_lim = lax.cond(jnp.max(excess.astype(I32)) > 0, tie_search,
                       lambda _: jnp.full((1, TQ), 2 ** 13, I32), 0)

    def mask_chunk(c, carry):
        kc = key_ref[c]
        sel = (kc > thr) | ((kc == thr) & (key_index(c) < tie_lim))
        add_ref[c] = jnp.where(sel, 0.0, NEG).astype(F32)
        return carry

    lax.fori_loop(0, nchunks, mask_chunk, 0)
    for off in range(2):
        @pl.when(qi >= off)
        def _():
            nm = add_ref[qi - off]
            for hh in range(N_HEADS):
                add_ref[nck + off * N_HEADS + hh] = nm + bias_ref[off, hh]

    def pipe_step(c1, c2, dst_ref, src_ref):
        m_cur = m_ref[...]
        m_rows, l_rows = [], []
        if c1 is not None:
            kc = k_ref[0, pl.ds(pl.multiple_of(c1 * TK, TK), TK), :]
            near = c1 >= qi - 1
        if c2 is not None:
            vc = vT_ref[0, c2]
            alpha = al_ref[...]
        for hh in range(N_HEADS):
            rows = slice(hh * HEAD_DIM, (hh + 1) * HEAD_DIM)
            if c1 is not None:
                tile = jnp.where(near, nck + (qi - c1) * N_HEADS + hh, c1)
                lt = jnp.dot(kc, qT_ref[0, rows, :], preferred_element_type=F32) + add_ref[tile]
                dst_ref[hh] = lt
                m_rows.append(jnp.max(lt, axis=0, keepdims=True))
            if c2 is not None:
                p = jnp.exp2(src_ref[hh] - m_cur[hh:hh + 1, :]).astype(BF16)
                pv = jnp.dot(vc, p, preferred_element_type=F32)
                oT_ref[rows, :] = alpha[hh:hh + 1, :] * oT_ref[rows, :] + pv[0:HEAD_DIM, :]
                l_rows.append(pv[HEAD_DIM:HEAD_DIM + 1, :])
        if c2 is not None:
            l_ref[...] = alpha * l_ref[...] + jnp.concatenate(l_rows, axis=0)
        if c1 is not None:
            m_new = jnp.maximum(m_cur, jnp.concatenate(m_rows, axis=0))
            al_ref[...] = jnp.exp2(m_cur - m_new)
            m_ref[...] = m_new

    m_ref[...] = jnp.full((N_HEADS, TQ), NEG, F32)
    l_ref[...] = jnp.zeros((N_HEADS, TQ), F32)
    oT_ref[...] = jnp.zeros((D_ATTN, TQ), F32)
    run_pipeline(pipe_step, nchunks - 1)
    for hh in range(N_HEADS):
        rows = slice(hh * HEAD_DIM, (hh + 1) * HEAD_DIM)
        oT_ref[rows, :] = oT_ref[rows, :] / l_ref[hh:hh + 1, :]

    o_ref[0] = oT_ref[...].T.astype(BF16)


def _merge_kernel(x_ref, mix_ref, att_ref, gmix_ref, wg_ref, bg_ref, wa_ref, wb_ref, wo_ref,
                  gffn_ref, wfg_ref, wfu_ref, wfd_ref, gfin_ref, o_ref, *, final_norm):
    x = x_ref[0]
    h = _rms(x, gmix_ref[...]).astype(BF16)
    gates = jax.nn.sigmoid(jnp.dot(h, wg_ref[...], preferred_element_type=F32) + bg_ref[...])
    y_a = jnp.dot(mix_ref[0], wa_ref[...], preferred_element_type=F32)
    y_b = jnp.dot(att_ref[0], wb_ref[...], preferred_element_type=F32)
    merged = gates[:, :D_MODEL] * y_a + gates[:, D_MODEL:] * y_b
    x1 = x + jnp.dot(merged.astype(BF16), wo_ref[...], preferred_element_type=F32)
    h2 = _rms(x1, gffn_ref[...]).astype(BF16)
    fg = jnp.dot(h2, wfg_ref[...], preferred_element_type=F32)
    fu = jnp.dot(h2, wfu_ref[...], preferred_element_type=F32)
    a = (jax.nn.silu(fg) * fu).astype(BF16)
    x2 = x1 + jnp.dot(a, wfd_ref[...], preferred_element_type=F32)
    o_ref[0] = _rms(x2, gfin_ref[...]) if final_norm else x2


def _const_spec(shape):
    nd = len(shape)
    return pl.BlockSpec(shape, lambda *_: (0,) * nd, pipeline_mode=pl.Buffered(1))


def kernel(x, g_mix, w_in, b_gate, conv_w, w_branch_a, w_branch_b, w_out, rel_bias, g_ffn,
           w_ffn_gate, w_ffn_up, w_ffn_down, g_final):
    bsz, seq, d = x.shape
    assert d == D_MODEL and seq % TM1 == 0 and seq % TQ == 0 and seq % TM3 == 0 and TQ == TK
    k_top = min(TOPK_MAX, seq // 4)
    nck = seq // TK
    d_ff = w_ffn_gate.shape[-1]
    cparams = functools.partial(pltpu.CompilerParams, vmem_limit_bytes=VMEM_LIMIT)

    for layer in range(g_mix.shape[0]):
        w = w_in[layer]
        o = np.cumsum([0, D_CONV, D_CONV, D_CONV, D_ATTN, HEAD_DIM, HEAD_DIM,
                       N_IDX_HEADS * IDX_DIM, IDX_DIM, N_IDX_HEADS, D_MODEL, D_MODEL])
        w_cb, w_cc, w_cx, w_q, w_k, w_v, w_iq, w_ik, w_iw, w_ga, w_gb = [
            w[:, int(o[j]):int(o[j + 1])] for j in range(11)]
        w_nat = jnp.concatenate([w_cb, w_cc, w_cx, w_k, w_ik], axis=1).astype(BF16)
        w_tr = jnp.concatenate([w_q * (HEAD_DIM ** -0.5 * LOG2E), w_iq, w_v, w_iw],
                               axis=1).T.astype(BF16)
        w_g = jnp.concatenate([w_ga, w_gb], axis=1).astype(BF16)
        n_nat, n_tr = w_nat.shape[1], w_tr.shape[0]

        mix, k_n, ik_n, qT, iqT, vT, iwT = pl.pallas_call(
            _proj_kernel,
            grid=(bsz, seq // TM1),
            in_specs=[
                pl.BlockSpec((1, TM1, D_MODEL), lambda b, i: (b, i, 0)),
                _const_spec((1, D_MODEL)),
                _const_spec((D_MODEL, n_nat)),
                _const_spec((n_tr, D_MODEL)),
                _const_spec((CONV_K, D_CONV)),
            ],
            out_specs=[
                pl.BlockSpec((1, TM1, D_CONV), lambda b, i: (b, i, 0)),
                pl.BlockSpec((1, TM1, HEAD_DIM), lambda b, i: (b, i, 0)),
                pl.BlockSpec((1, TM1, IDX_DIM), lambda b, i: (b, i, 0)),
                pl.BlockSpec((1, D_ATTN, TM1), lambda b, i: (b, 0, i)),
                pl.BlockSpec((1, N_IDX_HEADS * IDX_DIM, TM1), lambda b, i: (b, 0, i)),
                pl.BlockSpec((1, TM1 // TK, HEAD_DIM + V_EXTRA, TK), lambda b, i: (b, i, 0, 0)),
                pl.BlockSpec((1, N_IDX_HEADS, TM1), lambda b, i: (b, 0, i)),
            ],
            out_shape=[
                jax.ShapeDtypeStruct((bsz, seq, D_CONV), BF16),
                jax.ShapeDtypeStruct((bsz, seq, HEAD_DIM), BF16),
                jax.ShapeDtypeStruct((bsz, seq, IDX_DIM), BF16),
                jax.ShapeDtypeStruct((bsz, D_ATTN, seq), BF16),
                jax.ShapeDtypeStruct((bsz, N_IDX_HEADS * IDX_DIM, seq), BF16),
                jax.ShapeDtypeStruct((bsz, nck, HEAD_DIM + V_EXTRA, TK), BF16),
                jax.ShapeDtypeStruct((bsz, N_IDX_HEADS, seq), F32),
            ],
            scratch_shapes=[pltpu.VMEM((TM1 + 8, D_CONV), F32)],
            compiler_params=cparams(dimension_semantics=("arbitrary", "arbitrary")),
            name="proj",
        )(x, g_mix[layer][None, :], w_nat, w_tr, conv_w[layer])

        attn = pl.pallas_call(
            functools.partial(_dsa_kernel, k_top=k_top),
            grid=(bsz, seq // TQ),
            in_specs=[
                pl.BlockSpec(memory_space=pltpu.SMEM),
                pl.BlockSpec((1, N_IDX_HEADS * IDX_DIM, TQ), lambda b, i: (b, 0, i)),
                pl.BlockSpec((1, N_IDX_HEADS, TQ), lambda b, i: (b, 0, i)),
                pl.BlockSpec((1, seq, IDX_DIM), lambda b, i: (b, 0, 0)),
                pl.BlockSpec((1, D_ATTN, TQ), lambda b, i: (b, 0, i)),
                pl.BlockSpec((1, seq, HEAD_DIM), lambda b, i: (b, 0, 0)),
                pl.BlockSpec((1, nck, HEAD_DIM + V_EXTRA, TK), lambda b, i: (b, 0, 0, 0)),
            ],
            out_specs=pl.BlockSpec((1, TQ, D_ATTN), lambda b, i: (b, i, 0)),
            out_shape=jax.ShapeDtypeStruct((bsz, seq, D_ATTN), BF16),
            scratch_shapes=[
                pltpu.VMEM((nck, TK, TQ), I32),
                pltpu.VMEM((nck, TK, TQ), I16),
                pltpu.VMEM((nck, TK, TQ), I16),
                pltpu.VMEM((nck + 2 * N_HEADS, TK, TQ), F32),
                pltpu.VMEM((2, N_HEADS, TK, TQ), F32),
                pltpu.VMEM((D_ATTN, TQ), F32),
                pltpu.VMEM((N_HEADS, TQ), F32),
                pltpu.VMEM((N_HEADS, TQ), F32),
                pltpu.VMEM((N_HEADS, TQ), F32),
                pltpu.VMEM((N_HEADS, TK, TQ), F32),
                pltpu.VMEM((N_HEADS, TK, TQ), F32),
            ],
            compiler_params=cparams(dimension_semantics=("arbitrary", "arbitrary")),
            name="dsa",
        )(rel_bias, iqT, iwT, ik_n, qT, k_n, vT)

        x = pl.pallas_call(
            functools.partial(_merge_kernel, final_norm=layer == g_mix.shape[0] - 1),
            grid=(bsz, seq // TM3),
            in_specs=[
                pl.BlockSpec((1, TM3, D_MODEL), lambda b, i: (b, i, 0)),
                pl.BlockSpec((1, TM3, D_CONV), lambda b, i: (b, i, 0)),
                pl.BlockSpec((1, TM3, D_ATTN), lambda b, i: (b, i, 0)),
                _const_spec((1, D_MODEL)),
                _const_spec((D_MODEL, 2 * D_MODEL)),
                _const_spec((1, 2 * D_MODEL)),
                _const_spec((D_CONV, D_MODEL)),
                _const_spec((D_ATTN, D_MODEL)),
                _const_spec((D_MODEL, D_MODEL)),
                _const_spec((1, D_MODEL)),
                _const_spec((D_MODEL, d_ff)),
                _const_spec((D_MODEL, d_ff)),
                _const_spec((d_ff, D_MODEL)),
                _const_spec((1, D_MODEL)),
            ],
            out_specs=pl.BlockSpec((1, TM3, D_MODEL), lambda b, i: (b, i, 0)),
            out_shape=jax.ShapeDtypeStruct((bsz, seq, D_MODEL), F32),
            compiler_params=cparams(dimension_semantics=("arbitrary", "arbitrary")),
            name="merge",
        )(x, mix, attn, g_mix[layer][None, :], w_g, b_gate[layer][None, :],
          w_branch_a[layer].astype(BF16), w_branch_b[layer].astype(BF16),
          w_out[layer].astype(BF16), g_ffn[layer][None, :],
          w_ffn_gate[layer].astype(BF16), w_ffn_up[layer].astype(BF16),
          w_ffn_down[layer].astype(BF16),
          g_final[None, :])
    return x
```

```python
import functools
import math

import numpy as np
import jax
import jax.numpy as jnp
from jax import lax
from jax.experimental import pallas as pl
from jax.experimental.pallas import tpu as pltpu

D_MODEL = 1024
D_CONV = 512
CONV_K = 3
N_HEADS = 8
HEAD_DIM = 64
D_ATTN = N_HEADS * HEAD_DIM
N_IDX_HEADS = 8
IDX_DIM = 64
TOPK_MAX = 256
N_BUCKETS = 32
MAX_EXACT = 16
MAX_DISTANCE = 128
EPS = 1e-6

F32 = jnp.float32
BF16 = jnp.bfloat16
I32 = jnp.int32
I16 = jnp.int16

INT_MIN = -(2 ** 31)
I16_MIN = -(2 ** 15)
NEG = -1e30
LOG2E = math.log2(math.e)

TM1 = 512
TQ = 256
TK = 256
TM3 = 256
V_EXTRA = 16
LOW_TAIL_BITS = 4
VMEM_LIMIT = 56 * 1024 * 1024


def _bucket_steps():
    n = np.arange(2 * MAX_DISTANCE)
    large = MAX_EXACT + (np.log(np.maximum(n, 1).astype(np.float32) / MAX_EXACT)
                         / math.log(MAX_DISTANCE / MAX_EXACT)
                         * (N_BUCKETS - MAX_EXACT)).astype(np.int32)
    large = np.minimum(large, N_BUCKETS - 1)
    b = np.where(n < MAX_EXACT, n, large)
    assert np.all(b[MAX_DISTANCE:] == N_BUCKETS - 1) and np.all(np.diff(b) >= 0)
    steps = [(int(i), int(b[i])) for i in range(1, len(b)) if b[i] != b[i - 1]]
    return int(b[0]), steps


def _fold_rows(a, rows):
    parts = [a[i:i + rows, :] for i in range(0, a.shape[0], rows)]
    while len(parts) > 1:
        parts = [parts[i] + parts[i + 1] for i in range(0, len(parts) - 1, 2)] + (
            [parts[-1]] if len(parts) % 2 else [])
    return parts[0]


def _rms(x, g):
    return x * lax.rsqrt(jnp.mean(x * x, axis=-1, keepdims=True) + EPS) * g


def _proj_kernel(x_ref, g_ref, wn_ref, wt_ref, cw_ref,
                 mix_ref, k_ref, ik_ref, qT_ref, iqT_ref, vT_ref, iwT_ref, ubuf_ref):
    i = pl.program_id(1)
    x = x_ref[0]
    h = _rms(x, g_ref[...]).astype(BF16)

    def nat(c0, c1):
        return jnp.dot(h, wn_ref[:, c0:c1], preferred_element_type=F32)

    def tr(r0, r1):
        return lax.dot_general(wt_ref[r0:r1, :], h, (((1,), (1,)), ((), ())),
                               preferred_element_type=F32)

    u = nat(D_CONV, 2 * D_CONV) * nat(2 * D_CONV, 3 * D_CONV)

    @pl.when(i == 0)
    def _():
        ubuf_ref[0:8, :] = jnp.zeros((8, D_CONV), F32)

    ubuf_ref[8:8 + TM1, :] = u
    u1 = ubuf_ref[7:7 + TM1, :]
    u2 = ubuf_ref[6:6 + TM1, :]
    cw = cw_ref[...]
    y = cw[0:1, :] * u2 + cw[1:2, :] * u1 + cw[2:3, :] * u
    mix_ref[0] = (nat(0, D_CONV) * y).astype(BF16)
    ubuf_ref[0:8, :] = u[TM1 - 8:TM1, :]

    kk = nat(3 * D_CONV, 3 * D_CONV + HEAD_DIM + IDX_DIM)
    k_ref[0] = kk[:, 0:HEAD_DIM].astype(BF16)
    ik_ref[0] = kk[:, HEAD_DIM:HEAD_DIM + IDX_DIM].astype(BF16)

    qT_ref[0] = tr(0, D_ATTN).astype(BF16)
    iqT_ref[0] = tr(D_ATTN, 2 * D_ATTN).astype(BF16)
    vw = tr(2 * D_ATTN, 2 * D_ATTN + HEAD_DIM + V_EXTRA)
    vT = vw[0:HEAD_DIM, :].astype(BF16)
    ones_row = (lax.broadcasted_iota(I32, (V_EXTRA, TM1), 0) == 0).astype(F32).astype(BF16)
    vT = jnp.concatenate([vT, ones_row], axis=0)
    for j in range(TM1 // TK):
        vT_ref[0, j] = vT[:, j * TK:(j + 1) * TK]
    iwT_ref[0] = vw[HEAD_DIM:HEAD_DIM + N_IDX_HEADS, :] * (N_IDX_HEADS ** -0.5 * IDX_DIM ** -0.5)


def _dsa_kernel(rb_ref, iqT_ref, iwT_ref, ik_ref, qT_ref, k_ref, vT_ref,
                o_ref, key_ref, khi_ref, klo_ref, add_ref, bias_ref, oT_ref, m_ref, l_ref, al_ref,
                lta_ref, ltb_ref, *, k_top):
    b = pl.program_id(0)
    qi = pl.program_id(1)
    nchunks = qi + 1
    nck = key_ref.shape[0]

    @pl.when((b == 0) & (qi == 0))
    def _():
        b0, steps = _bucket_steps()
        row = lax.broadcasted_iota(I32, (TK, TQ), 0)
        col = lax.broadcasted_iota(I32, (TK, TQ), 1)
        for off in range(2):
            dist = col - row + off * TK
            for hh in range(N_HEADS):
                val = jnp.full((TK, TQ), rb_ref[b0, hh], F32)
                for n0, bk in steps:
                    val = jnp.where(dist >= n0, rb_ref[bk, hh], val)
                bias_ref[off, hh] = (val - rb_ref[N_BUCKETS - 1, hh]) * LOG2E

    w = iwT_ref[0]
    sub = TK // 4

    def score_step(c1, c2, dst_ref, src_ref):
        if c1 is not None:
            ikc = ik_ref[0, pl.ds(pl.multiple_of(c1 * TK, TK), TK), :]
        if c2 is not None:
            col = lax.broadcasted_iota(I32, (sub, TQ), 1) + qi * TQ
        for g in range(4):
            if c1 is not None:
                for hh in range(2 * g, 2 * g + 2):
                    dst_ref[hh] = jnp.dot(ikc, iqT_ref[0, hh * IDX_DIM:(hh + 1) * IDX_DIM, :],
                                          preferred_element_type=F32)
            if c2 is not None:
                rows = slice(g * sub, (g + 1) * sub)
                acc = jnp.zeros((sub, TQ), F32)
                for hh in range(N_IDX_HEADS):
                    acc = acc + w[hh:hh + 1, :] * jnp.maximum(src_ref[hh, rows, :], 0.0)
                bits = pltpu.bitcast(acc, I32)
                key = bits ^ ((bits >> 31) & 0x7FFFFFFF)
                row = lax.broadcasted_iota(I32, (sub, TQ), 0) + (c2 * TK + g * sub)
                key = jnp.where(row <= col, key, INT_MIN)
                key_ref[c2, rows, :] = key
                khi_ref[c2, rows, :] = (key >> 16).astype(I16)

    def run_pipeline(step, n_steps):
        step(0, None, lta_ref, None)

        def two_steps(j, carry):
            c = 2 * j
            step(c + 1, c, ltb_ref, lta_ref)
            step(c + 2, c + 1, lta_ref, ltb_ref)
            return carry

        lax.fori_loop(0, n_steps // 2, two_steps, 0)

        @pl.when(n_steps % 2 == 1)
        def _():
            step(n_steps, n_steps - 1, ltb_ref, lta_ref)
            step(None, n_steps, None, ltb_ref)

        @pl.when(n_steps % 2 == 0)
        def _():
            step(None, n_steps, None, lta_ref)

    run_pipeline(score_step, nchunks - 1)

    @pl.when(nchunks % 2 == 1)
    def _():
        khi_ref[nchunks] = jnp.full((TK, TQ), I16_MIN, I16)
        klo_ref[nchunks] = jnp.full((TK, TQ), I16_MIN, I16)

    def count(pred):
        def body(c, acc):
            m = pred(key_ref[c], c).astype(I32)
            return acc + jnp.sum(m.reshape(TK // 8, 8, TQ), axis=0)
        acc = lax.fori_loop(0, nchunks, body, jnp.zeros((8, TQ), I32))
        return jnp.sum(acc, axis=0, keepdims=True)

    def count16(ref, pred):
        def ones(c):
            return _fold_rows(jnp.where(pred(ref[c]), jnp.int16(1), jnp.int16(0)), 16)

        def pair(j, acc):
            return acc + (ones(2 * j) + ones(2 * j + 1))

        acc = lax.fori_loop(0, (nchunks + 1) // 2, pair, jnp.zeros((16, TQ), I16))
        return jnp.sum(acc.astype(I32), axis=0, keepdims=True)

    def max16(x, y):
        return jnp.where(x > y, x, y)

    def min16(x, y):
        return jnp.where(x < y, x, y)

    def fold16(a, op):
        parts = [a[i:i + 16, :] for i in range(0, a.shape[0], 16)]
        while len(parts) > 1:
            parts = [op(parts[i], parts[i + 1]) for i in range(0, len(parts), 2)]
        wide = parts[0].astype(I32)
        return (jnp.max if op is max16 else jnp.min)(wide, axis=0, keepdims=True)

    gmax = lax.fori_loop(0, nchunks, lambda c, g: max16(g, khi_ref[c]),
                         jnp.full((TK, TQ), I16_MIN, I16))
    lo0 = fold16(gmax, min16)
    hi0 = fold16(gmax, max16)
    width = jnp.max(hi0 - lo0)
    n_bisect = sum(((width >> j) > 0).astype(I32) for j in range(16))

    def bisect(it, lohi):
        lo, hi = lohi
        mid = lo + ((hi - lo + 1) >> 1)
        mid16 = mid.astype(I16)
        ok = count16(khi_ref, lambda kc: kc >= mid16) >= k_top
        return jnp.where(ok, mid, lo), jnp.where(ok, hi, mid - 1)

    thr_hi, _ = lax.fori_loop(0, n_bisect, bisect, (lo0, hi0))
    thr_hi16 = thr_hi.astype(I16)
    room_hi = k_top - count16(khi_ref, lambda kc: kc > thr_hi16)

    def low_chunk(c, carry):
        lo = ((key_ref[c] & 0xFFFF) - 2 ** 15).astype(I16)
        klo_ref[c] = jnp.where(khi_ref[c] == thr_hi16, lo, jnp.int16(I16_MIN))
        return carry

    lax.fori_loop(0, nchunks, low_chunk, 0)

    def low_bits(first, n, vc):
        def step(j, vc):
            v, cv = vc
            cand = v + jnp.left_shift(jnp.int32(1), 15 - first - j)
            cand16 = cand.astype(I16)
            cnt = count16(klo_ref, lambda kc: kc >= cand16)
            ok = cnt >= room_hi
            return jnp.where(ok, cand, v), jnp.where(ok, cnt, cv)
        return lax.fori_loop(0, n, step, vc)

    vc = low_bits(0, 16 - LOW_TAIL_BITS,
                  (jnp.full((1, TQ), I16_MIN, I32), jnp.full((1, TQ), 2 ** 30, I32)))
    unresolved = jnp.max((vc[1] != room_hi).astype(I32)) > 0
    thr_lo, cnt_lo = lax.cond(unresolved,
                              lambda vc: low_bits(16 - LOW_TAIL_BITS, LOW_TAIL_BITS, vc),
                              lambda vc: vc, vc)
    thr = thr_hi * 2 ** 16 + (thr_lo + 2 ** 15)
    excess = (cnt_lo > room_hi) & (thr > INT_MIN)
    any_excess = jnp.max(excess.astype(I32)) > 0
    thr = jnp.maximum(thr, INT_MIN + 1)

    @pl.when(jnp.logical_not(any_excess))
    def _():
        def mask_chunk(c, carry):
            add_ref[c] = jnp.where(key_ref[c] >= thr, 0.0, NEG).astype(F32)
            return carry
        lax.fori_loop(0, nchunks, mask_chunk, 0)

    @pl.when(any_excess)
    def _():
        thr_lo16 = thr_lo.astype(I16)
        room = room_hi - count16(klo_ref, lambda kc: kc > thr_lo16)

        def key_index(c):
            return lax.broadcasted_iota(I32, (TK, TQ), 0) + c * TK

        def step(it, lim):
            cand = lim + jnp.left_shift(jnp.int32(1), 12 - it)
            cnt = count(lambda kc, c: (kc == thr) & (key_index(c) < cand))
            return jnp.where(cnt <= room, cand, lim)

        tie_lim = lax.fori_loop(0, 13, step, jnp.zeros((1, TQ), I32))

        def mask_chunk(c, carry):
            kc = key_ref[c]
            sel = (kc > thr) | ((kc == thr) & (key_index(c) < tie_lim))
            add_ref[c] = jnp.where(sel, 0.0, NEG).astype(F32)
            return carry
        lax.fori_loop(0, nchunks, mask_chunk, 0)

    for off in range(2):
        @pl.when(qi >= off)
        def _():
            nm = add_ref[qi - off]
            for hh in range(N_HEADS):
                add_ref[nck + off * N_HEADS + hh] = nm + bias_ref[off, hh]

    def pipe_step(c1, c2, dst_ref, src_ref):
        m_cur = m_ref[...]
        m_rows, l_rows = [], []
        if c1 is not None:
            kc = k_ref[0, pl.ds(pl.multiple_of(c1 * TK, TK), TK), :]
            near = c1 >= qi - 1
        if c2 is not None:
            vc = vT_ref[0, c2]
            alpha = al_ref[...]
        for hh in range(N_HEADS):
            rows = slice(hh * HEAD_DIM, (hh + 1) * HEAD_DIM)
            if c1 is not None:
                tile = jnp.where(near, nck + (qi - c1) * N_HEADS + hh, c1)
                lt = jnp.dot(kc, qT_ref[0, rows, :], preferred_element_type=F32) + add_ref[tile]
                dst_ref[hh] = lt
                m_rows.append(jnp.max(lt, axis=0, keepdims=True))
            if c2 is not None:
                p = jnp.exp2(src_ref[hh] - m_cur[hh:hh + 1, :]).astype(BF16)
                pv = jnp.dot(vc, p, preferred_element_type=F32)
                oT_ref[rows, :] = alpha[hh:hh + 1, :] * oT_ref[rows, :] + pv[0:HEAD_DIM, :]
                l_rows.append(pv[HEAD_DIM:HEAD_DIM + 1, :])
        if c2 is not None:
            l_ref[...] = alpha * l_ref[...] + jnp.concatenate(l_rows, axis=0)
        if c1 is not None:
            m_new = jnp.maximum(m_cur, jnp.concatenate(m_rows, axis=0))
            al_ref[...] = jnp.exp2(m_cur - m_new)
            m_ref[...] = m_new

    m_ref[...] = jnp.full((N_HEADS, TQ), NEG, F32)
    l_ref[...] = jnp.zeros((N_HEADS, TQ), F32)
    oT_ref[...] = jnp.zeros((D_ATTN, TQ), F32)
    run_pipeline(pipe_step, nchunks - 1)
    for hh in range(N_HEADS):
        rows = slice(hh * HEAD_DIM, (hh + 1) * HEAD_DIM)
        oT_ref[rows, :] = oT_ref[rows, :] / l_ref[hh:hh + 1, :]

    o_ref[0] = oT_ref[...].T.astype(BF16)


def _merge_kernel(x_ref, mix_ref, att_ref, gmix_ref, wg_ref, bg_ref, wa_ref, wb_ref, wo_ref,
                  gffn_ref, wfg_ref, wfu_ref, wfd_ref, gfin_ref, o_ref, *, final_norm):
    x = x_ref[0]
    h = _rms(x, gmix_ref[...]).astype(BF16)
    gates = jax.nn.sigmoid(jnp.dot(h, wg_ref[...], preferred_element_type=F32) + bg_ref[...])
    y_a = jnp.dot(mix_ref[0], wa_ref[...], preferred_element_type=F32)
    y_b = jnp.dot(att_ref[0], wb_ref[...], preferred_element_type=F32)
    merged = gates[:, :D_MODEL] * y_a + gates[:, D_MODEL:] * y_b
    x1 = x + jnp.dot(merged.astype(BF16), wo_ref[...], preferred_element_type=F32)
    h2 = _rms(x1, gffn_ref[...]).astype(BF16)
    fg = jnp.dot(h2, wfg_ref[...], preferred_element_type=F32)
    fu = jnp.dot(h2, wfu_ref[...], preferred_element_type=F32)
    a = (jax.nn.silu(fg) * fu).astype(BF16)
    x2 = x1 + jnp.dot(a, wfd_ref[...], preferred_element_type=F32)
    o_ref[0] = _rms(x2, gfin_ref[...]) if final_norm else x2


def _const_spec(shape):
    nd = len(shape)
    return pl.BlockSpec(shape, lambda *_: (0,) * nd, pipeline_mode=pl.Buffered(1))


def kernel(x, g_mix, w_in, b_gate, conv_w, w_branch_a, w_branch_b, w_out, rel_bias, g_ffn,
           w_ffn_gate, w_ffn_up, w_ffn_down, g_final):
    bsz, seq, d = x.shape
    assert d == D_MODEL and seq % TM1 == 0 and seq % TQ == 0 and seq % TM3 == 0 and TQ == TK
    k_top = min(TOPK_MAX, seq // 4)
    assert k_top <= TK
    nck = seq // TK
    d_ff = w_ffn_gate.shape[-1]
    cparams = functools.partial(pltpu.CompilerParams, vmem_limit_bytes=VMEM_LIMIT)

    for layer in range(g_mix.shape[0]):
        w = w_in[layer]
        o = np.cumsum([0, D_CONV, D_CONV, D_CONV, D_ATTN, HEAD_DIM, HEAD_DIM,
                       N_IDX_HEADS * IDX_DIM, IDX_DIM, N_IDX_HEADS, D_MODEL, D_MODEL])
        w_cb, w_cc, w_cx, w_q, w_k, w_v, w_iq, w_ik, w_iw, w_ga, w_gb = [
            w[:, int(o[j]):int(o[j + 1])] for j in range(11)]
        w_nat = jnp.concatenate([w_cb, w_cc, w_cx, w_k, w_ik], axis=1).astype(BF16)
        w_pad = jnp.zeros((D_MODEL, V_EXTRA - N_IDX_HEADS), w.dtype)
        w_tr = jnp.concatenate([w_q * (HEAD_DIM ** -0.5 * LOG2E), w_iq, w_v, w_iw, w_pad],
                               axis=1).T.astype(BF16)
        w_g = jnp.concatenate([w_ga, w_gb], axis=1).astype(BF16)
        n_nat, n_tr = w_nat.shape[1], w_tr.shape[0]

        mix, k_n, ik_n, qT, iqT, vT, iwT = pl.pallas_call(
            _proj_kernel,
            grid=(bsz, seq // TM1),
            in_specs=[
                pl.BlockSpec((1, TM1, D_MODEL), lambda b, i: (b, i, 0)),
                _const_spec((1, D_MODEL)),
                _const_spec((D_MODEL, n_nat)),
                _const_spec((n_tr, D_MODEL)),
                _const_spec((CONV_K, D_CONV)),
            ],
            out_specs=[
                pl.BlockSpec((1, TM1, D_CONV), lambda b, i: (b, i, 0)),
                pl.BlockSpec((1, TM1, HEAD_DIM), lambda b, i: (b, i, 0)),
                pl.BlockSpec((1, TM1, IDX_DIM), lambda b, i: (b, i, 0)),
                pl.BlockSpec((1, D_ATTN, TM1), lambda b, i: (b, 0, i)),
                pl.BlockSpec((1, N_IDX_HEADS * IDX_DIM, TM1), lambda b, i: (b, 0, i)),
                pl.BlockSpec((1, TM1 // TK, HEAD_DIM + V_EXTRA, TK), lambda b, i: (b, i, 0, 0)),
                pl.BlockSpec((1, N_IDX_HEADS, TM1), lambda b, i: (b, 0, i)),
            ],
            out_shape=[
                jax.ShapeDtypeStruct((bsz, seq, D_CONV), BF16),
                jax.ShapeDtypeStruct((bsz, seq, HEAD_DIM), BF16),
                jax.ShapeDtypeStruct((bsz, seq, IDX_DIM), BF16),
                jax.ShapeDtypeStruct((bsz, D_ATTN, seq), BF16),
                jax.ShapeDtypeStruct((bsz, N_IDX_HEADS * IDX_DIM, seq), BF16),
                jax.ShapeDtypeStruct((bsz, nck, HEAD_DIM + V_EXTRA, TK), BF16),
                jax.ShapeDtypeStruct((bsz, N_IDX_HEADS, seq), F32),
            ],
            scratch_shapes=[pltpu.VMEM((TM1 + 8, D_CONV), F32)],
            compiler_params=cparams(dimension_semantics=("arbitrary", "arbitrary")),
            name="proj",
        )(x, g_mix[layer][None, :], w_nat, w_tr, conv_w[layer])

        attn = pl.pallas_call(
            functools.partial(_dsa_kernel, k_top=k_top),
            grid=(bsz, seq // TQ),
            in_specs=[
                pl.BlockSpec(memory_space=pltpu.SMEM),
                pl.BlockSpec((1, N_IDX_HEADS * IDX_DIM, TQ), lambda b, i: (b, 0, i)),
                pl.BlockSpec((1, N_IDX_HEADS, TQ), lambda b, i: (b, 0, i)),
                pl.BlockSpec((1, seq, IDX_DIM), lambda b, i: (b, 0, 0)),
                pl.BlockSpec((1, D_ATTN, TQ), lambda b, i: (b, 0, i)),
                pl.BlockSpec((1, seq, HEAD_DIM), lambda b, i: (b, 0, 0)),
                pl.BlockSpec((1, nck, HEAD_DIM + V_EXTRA, TK), lambda b, i: (b, 0, 0, 0)),
            ],
            out_specs=pl.BlockSpec((1, TQ, D_ATTN), lambda b, i: (b, i, 0)),
            out_shape=jax.ShapeDtypeStruct((bsz, seq, D_ATTN), BF16),
            scratch_shapes=[
                pltpu.VMEM((nck, TK, TQ), I32),
                pltpu.VMEM((nck, TK, TQ), I16),
                pltpu.VMEM((nck, TK, TQ), I16),
                pltpu.VMEM((nck + 2 * N_HEADS, TK, TQ), F32),
                pltpu.VMEM((2, N_HEADS, TK, TQ), F32),
                pltpu.VMEM((D_ATTN, TQ), F32),
                pltpu.VMEM((N_HEADS, TQ), F32),
                pltpu.VMEM((N_HEADS, TQ), F32),
                pltpu.VMEM((N_HEADS, TQ), F32),
                pltpu.VMEM((N_HEADS, TK, TQ), F32),
                pltpu.VMEM((N_HEADS, TK, TQ), F32),
            ],
            compiler_params=cparams(dimension_semantics=("arbitrary", "arbitrary")),
            name="dsa",
        )(rel_bias, iqT, iwT, ik_n, qT, k_n, vT)

        x = pl.pallas_call(
            functools.partial(_merge_kernel, final_norm=layer == g_mix.shape[0] - 1),
            grid=(bsz, seq // TM3),
            in_specs=[
                pl.BlockSpec((1, TM3, D_MODEL), lambda b, i: (b, i, 0)),
                pl.BlockSpec((1, TM3, D_CONV), lambda b, i: (b, i, 0)),
                pl.BlockSpec((1, TM3, D_ATTN), lambda b, i: (b, i, 0)),
                _const_spec((1, D_MODEL)),
                _const_spec((D_MODEL, 2 * D_MODEL)),
                _const_spec((1, 2 * D_MODEL)),
                _const_spec((D_CONV, D_MODEL)),
                _const_spec((D_ATTN, D_MODEL)),
                _const_spec((D_MODEL, D_MODEL)),
                _const_spec((1, D_MODEL)),
                _const_spec((D_MODEL, d_ff)),
                _const_spec((D_MODEL, d_ff)),
                _const_spec((d_ff, D_MODEL)),
                _const_spec((1, D_MODEL)),
            ],
            out_specs=pl.BlockSpec((1, TM3, D_MODEL), lambda b, i: (b, i, 0)),
            out_shape=jax.ShapeDtypeStruct((bsz, seq, D_MODEL), F32),
            compiler_params=cparams(dimension_semantics=("arbitrary", "arbitrary")),
            name="merge",
        )(x, mix, attn, g_mix[layer][None, :], w_g, b_gate[layer][None, :],
          w_branch_a[layer].astype(BF16), w_branch_b[layer].astype(BF16),
          w_out[layer].astype(BF16), g_ffn[layer][None, :],
          w_ffn_gate[layer].astype(BF16), w_ffn_up[layer].astype(BF16),
          w_ffn_down[layer].astype(BF16),
          g_final[None, :])
    return x
```

```python
import functools
import math

import numpy as np
import jax
import jax.numpy as jnp
from jax import lax
from jax.experimental import pallas as pl
from jax.experimental.pallas import tpu as pltpu

D_MODEL = 1024
D_CONV = 512
CONV_K = 3
N_HEADS = 8
HEAD_DIM = 64
D_ATTN = N_HEADS * HEAD_DIM
N_IDX_HEADS = 8
IDX_DIM = 64
TOPK_MAX = 256
N_BUCKETS = 32
MAX_EXACT = 16
MAX_DISTANCE = 128
EPS = 1e-6

F32 = jnp.float32
BF16 = jnp.bfloat16
I32 = jnp.int32
I16 = jnp.int16

INT_MIN = -(2 ** 31)
I16_MIN = -(2 ** 15)
NEG = -1e30
LOG2E = math.log2(math.e)

TM1 = 512
TQ = 256
TK = 256
TM3 = 256
V_EXTRA = 16
MID_BITS = 12
LOW_TAIL_BITS = 4
ONE_BF16_BITS = 0x3F80
NEG_INF_BF16_BITS = -0x80
VMEM_LIMIT = 56 * 1024 * 1024


def _bucket_steps():
    n = np.arange(2 * MAX_DISTANCE)
    large = MAX_EXACT + (np.log(np.maximum(n, 1).astype(np.float32) / MAX_EXACT)
                         / math.log(MAX_DISTANCE / MAX_EXACT)
                         * (N_BUCKETS - MAX_EXACT)).astype(np.int32)
    large = np.minimum(large, N_BUCKETS - 1)
    b = np.where(n < MAX_EXACT, n, large)
    assert np.all(b[MAX_DISTANCE:] == N_BUCKETS - 1) and np.all(np.diff(b) >= 0)
    steps = [(int(i), int(b[i])) for i in range(1, len(b)) if b[i] != b[i - 1]]
    return int(b[0]), steps


def _fold_rows(a, rows):
    parts = [a[i:i + rows, :] for i in range(0, a.shape[0], rows)]
    while len(parts) > 1:
        parts = [parts[i] + parts[i + 1] for i in range(0, len(parts) - 1, 2)] + (
            [parts[-1]] if len(parts) % 2 else [])
    return parts[0]


def _rms(x, g):
    return x * lax.rsqrt(jnp.mean(x * x, axis=-1, keepdims=True) + EPS) * g


def _proj_kernel(x_ref, g_ref, wn_ref, wt_ref, cw_ref,
                 mix_ref, k_ref, ik_ref, qT_ref, iqT_ref, vT_ref, iwT_ref, ubuf_ref):
    i = pl.program_id(1)
    x = x_ref[0]
    h = _rms(x, g_ref[...]).astype(BF16)

    def nat(c0, c1):
        return jnp.dot(h, wn_ref[:, c0:c1], preferred_element_type=F32)

    def tr(r0, r1):
        return lax.dot_general(wt_ref[r0:r1, :], h, (((1,), (1,)), ((), ())),
                               preferred_element_type=F32)

    u = nat(D_CONV, 2 * D_CONV) * nat(2 * D_CONV, 3 * D_CONV)

    @pl.when(i == 0)
    def _():
        ubuf_ref[0:8, :] = jnp.zeros((8, D_CONV), F32)

    ubuf_ref[8:8 + TM1, :] = u
    u1 = ubuf_ref[7:7 + TM1, :]
    u2 = ubuf_ref[6:6 + TM1, :]
    cw = cw_ref[...]
    y = cw[0:1, :] * u2 + cw[1:2, :] * u1 + cw[2:3, :] * u
    mix_ref[0] = (nat(0, D_CONV) * y).astype(BF16)
    ubuf_ref[0:8, :] = u[TM1 - 8:TM1, :]

    kk = nat(3 * D_CONV, 3 * D_CONV + HEAD_DIM + IDX_DIM)
    k_ref[0] = kk[:, 0:HEAD_DIM].astype(BF16)
    ik_ref[0] = kk[:, HEAD_DIM:HEAD_DIM + IDX_DIM].astype(BF16)

    qT_ref[0] = tr(0, D_ATTN).astype(BF16)
    iqT_ref[0] = tr(D_ATTN, 2 * D_ATTN).astype(BF16)
    vw = tr(2 * D_ATTN, 2 * D_ATTN + HEAD_DIM + V_EXTRA)
    vT = vw[0:HEAD_DIM, :].astype(BF16)
    ones_row = (lax.broadcasted_iota(I32, (V_EXTRA, TM1), 0) == 0).astype(F32).astype(BF16)
    vT = jnp.concatenate([vT, ones_row], axis=0)
    for j in range(TM1 // TK):
        vT_ref[0, j] = vT[:, j * TK:(j + 1) * TK]
    iwT_ref[0] = vw[HEAD_DIM:HEAD_DIM + N_IDX_HEADS, :] * (N_IDX_HEADS ** -0.5 * IDX_DIM ** -0.5)


def _dsa_kernel(rb_ref, iqT_ref, iwT_ref, ik_ref, qT_ref, k_ref, vT_ref,
                o_ref, key_ref, khi_ref, klo_ref, add_ref, bias_ref, oT_ref, m_ref, l_ref, al_ref,
                lta_ref, ltb_ref, *, k_top):
    b = pl.program_id(0)
    qi = pl.program_id(1)
    nchunks = qi + 1
    nck = key_ref.shape[0]

    @pl.when((b == 0) & (qi == 0))
    def _():
        b0, steps = _bucket_steps()
        row = lax.broadcasted_iota(I32, (TK, TQ), 0)
        col = lax.broadcasted_iota(I32, (TK, TQ), 1)
        for off in range(2):
            dist = col - row + off * TK
            for hh in range(N_HEADS):
                val = jnp.full((TK, TQ), rb_ref[b0, hh], F32)
                for n0, bk in steps:
                    val = jnp.where(dist >= n0, rb_ref[bk, hh], val)
                bias_ref[off, hh] = (val - rb_ref[N_BUCKETS - 1, hh]) * LOG2E

    w = iwT_ref[0]
    sub = TK // 4

    def score_step(c1, c2, dst_ref, src_ref):
        if c1 is not None:
            ikc = ik_ref[0, pl.ds(pl.multiple_of(c1 * TK, TK), TK), :]
        if c2 is not None:
            col = lax.broadcasted_iota(I32, (sub, TQ), 1) + qi * TQ
        for g in range(4):
            if c1 is not None:
                for hh in range(2 * g, 2 * g + 2):
                    dst_ref[hh] = jnp.dot(ikc, iqT_ref[0, hh * IDX_DIM:(hh + 1) * IDX_DIM, :],
                                          preferred_element_type=F32)
            if c2 is not None:
                rows = slice(g * sub, (g + 1) * sub)
                acc = jnp.zeros((sub, TQ), F32)
                for hh in range(N_IDX_HEADS):
                    acc = acc + w[hh:hh + 1, :] * jnp.maximum(src_ref[hh, rows, :], 0.0)
                acc = jnp.where(acc == 0.0, 0.0, acc)
                bits = pltpu.bitcast(acc, I32)
                key = bits ^ ((bits >> 31) & 0x7FFFFFFF)
                row = lax.broadcasted_iota(I32, (sub, TQ), 0) + (c2 * TK + g * sub)
                key = jnp.where(row <= col, key, INT_MIN)
                key_ref[c2, rows, :] = key
                hi = jnp.where(row <= col, bits >> 16, NEG_INF_BF16_BITS)
                khi_ref[c2, rows, :] = pltpu.bitcast(hi.astype(I16), BF16)

    def run_pipeline(step, n_steps):
        step(0, None, lta_ref, None)

        def two_steps(j, carry):
            c = 2 * j
            step(c + 1, c, ltb_ref, lta_ref)
            step(c + 2, c + 1, lta_ref, ltb_ref)
            return carry

        lax.fori_loop(0, n_steps // 2, two_steps, 0)

        @pl.when(n_steps % 2 == 1)
        def _():
            step(n_steps, n_steps - 1, ltb_ref, lta_ref)
            step(None, n_steps, None, ltb_ref)

        @pl.when(n_steps % 2 == 0)
        def _():
            step(None, n_steps, None, lta_ref)

    run_pipeline(score_step, nchunks - 1)

    @pl.when(nchunks % 2 == 1)
    def _():
        khi_ref[nchunks] = jnp.full((TK, TQ), -jnp.inf, BF16)
        klo_ref[nchunks] = jnp.zeros((TK, TQ), BF16)

    def count(pred):
        def body(c, acc):
            m = pred(key_ref[c], c).astype(I32)
            return acc + jnp.sum(m.reshape(TK // 8, 8, TQ), axis=0)
        acc = lax.fori_loop(0, nchunks, body, jnp.zeros((8, TQ), I32))
        return jnp.sum(acc, axis=0, keepdims=True)

    def count16(ref, pred):
        def ones(c):
            return _fold_rows(jnp.where(pred(ref[c]), BF16(1), BF16(0)), 16)

        def pair(j, acc):
            return acc + (ones(2 * j) + ones(2 * j + 1))

        acc = lax.fori_loop(0, (nchunks + 1) // 2, pair, jnp.zeros((16, TQ), BF16))
        return jnp.sum(acc.astype(F32), axis=0, keepdims=True).astype(I32)

    def as_bf16(bits):
        signed = ((bits + 2 ** 15) & 0xFFFF) - 2 ** 15
        return pltpu.bitcast(jnp.broadcast_to(signed, (16, TQ)).astype(I16), BF16)[0:1, :]

    def hi_pattern(key16):
        return key16 ^ ((key16 >> 15) & 0x7FFF)

    def hi_candidate(key16):
        bits = hi_pattern(key16)
        mag = bits & 0x7FFF
        bits = jnp.where(mag < 0x80, jnp.where((bits < 0) | (mag == 0), 0, 0x80), bits)
        return as_bf16(bits)

    def descend(ref, nbits, want, candidate, v0):
        def step(j, vc):
            v, cv = vc
            cand = v + jnp.left_shift(jnp.int32(1), nbits - 1 - j)
            cand16 = candidate(cand)
            cnt = count16(ref, lambda kc: kc >= cand16)
            ok = cnt >= want
            return jnp.where(ok, cand, v), jnp.where(ok, cnt, cv)
        return lax.fori_loop(0, nbits, step, (v0, jnp.full((1, TQ), 2 ** 30, I32)))

    def digit(d):
        return as_bf16(ONE_BF16_BITS + d)

    thr_hi, _ = descend(khi_ref, 16, k_top, hi_candidate, jnp.full((1, TQ), I16_MIN, I32))
    thr_hi16 = as_bf16(hi_pattern(thr_hi))
    room_hi = k_top - count16(khi_ref, lambda kc: kc > thr_hi16)

    def mid_chunk(c, carry):
        d12 = (key_ref[c] >> LOW_TAIL_BITS) & (2 ** MID_BITS - 1)
        pat = pltpu.bitcast((ONE_BF16_BITS + d12).astype(I16), BF16)
        klo_ref[c] = jnp.where(khi_ref[c] == thr_hi16, pat, BF16(0))
        return carry

    lax.fori_loop(0, nchunks, mid_chunk, 0)
    v_mid, cnt_mid = descend(klo_ref, MID_BITS, room_hi, digit, jnp.zeros((1, TQ), I32))
    unresolved = jnp.max(((cnt_mid != room_hi) & (thr_hi > I16_MIN)).astype(I32)) > 0

    def low_tail(_):
        mid16 = digit(v_mid)
        want = room_hi - count16(klo_ref, lambda kc: kc > mid16)

        def tail_chunk(c, carry):
            pat = pltpu.bitcast((ONE_BF16_BITS + (key_ref[c] & (2 ** LOW_TAIL_BITS - 1))).astype(I16),
                                BF16)
            klo_ref[c] = jnp.where(klo_ref[c] == mid16, pat, BF16(0))
            return carry

        lax.fori_loop(0, nchunks, tail_chunk, 0)
        v_low, cnt_low = descend(klo_ref, LOW_TAIL_BITS, want, digit, jnp.zeros((1, TQ), I32))
        return v_mid * 2 ** LOW_TAIL_BITS + v_low, (cnt_low > want).astype(I32)

    thr_lo, excess = lax.cond(unresolved, low_tail,
                              lambda _: (v_mid * 2 ** LOW_TAIL_BITS, jnp.zeros((1, TQ), I32)), 0)
    thr = thr_hi * 2 ** 16 + thr_lo
    any_excess = jnp.max(excess * (thr > INT_MIN).astype(I32)) > 0
    thr = jnp.maximum(thr, INT_MIN + 1)

    @pl.when(jnp.logical_not(any_excess))
    def _():
        def mask_chunk(c, carry):
            add_ref[c] = jnp.where(key_ref[c] >= thr, 0.0, NEG).astype(F32)
            return carry
        lax.fori_loop(0, nchunks, mask_chunk, 0)

    @pl.when(any_excess)
    def _():
        room = k_top - count(lambda kc, c: kc > thr)

        def key_index(c):
            return lax.broadcasted_iota(I32, (TK, TQ), 0) + c * TK

        def step(it, lim):
            cand = lim + jnp.left_shift(jnp.int32(1), 12 - it)
            cnt = count(lambda kc, c: (kc == thr) & (key_index(c) < cand))
            return jnp.where(cnt <= room, cand, lim)

        tie_lim = lax.fori_loop(0, 13, step, jnp.zeros((1, TQ), I32))

        def mask_chunk(c, carry):
            kc = key_ref[c]
            sel = (kc > thr) | ((kc == thr) & (key_index(c) < tie_lim))
            add_ref[c] = jnp.where(sel, 0.0, NEG).astype(F32)
            return carry
        lax.fori_loop(0, nchunks, mask_chunk, 0)

    for off in range(2):
        @pl.when(qi >= off)
        def _():
            nm = add_ref[qi - off]
            for hh in range(N_HEADS):
                add_ref[nck + off * N_HEADS + hh] = nm + bias_ref[off, hh]

    def pipe_step(c1, c2, dst_ref, src_ref):
        m_cur = m_ref[...]
        m_rows, l_rows = [], []
        if c1 is not None:
            kc = k_ref[0, pl.ds(pl.multiple_of(c1 * TK, TK), TK), :]
            near = c1 >= qi - 1
        if c2 is not None:
            vc = vT_ref[0, c2]
            alpha = al_ref[...]
        for hh in range(N_HEADS):
            rows = slice(hh * HEAD_DIM, (hh + 1) * HEAD_DIM)
            if c1 is not None:
                tile = jnp.where(near, nck + (qi - c1) * N_HEADS + hh, c1)
                lt = jnp.dot(kc, qT_ref[0, rows, :], preferred_element_type=F32) + add_ref[tile]
                dst_ref[hh] = lt
                m_rows.append(jnp.max(lt, axis=0, keepdims=True))
            if c2 is not None:
                p = jnp.exp2(src_ref[hh] - m_cur[hh:hh + 1, :]).astype(BF16)
                pv = jnp.dot(vc, p, preferred_element_type=F32)
                oT_ref[rows, :] = alpha[hh:hh + 1, :] * oT_ref[rows, :] + pv[0:HEAD_DIM, :]
                l_rows.append(pv[HEAD_DIM:HEAD_DIM + 1, :])
        if c2 is not None:
            l_ref[...] = alpha * l_ref[...] + jnp.concatenate(l_rows, axis=0)
        if c1 is not None:
            m_new = jnp.maximum(m_cur, jnp.concatenate(m_rows, axis=0))
            al_ref[...] = jnp.exp2(m_cur - m_new)
            m_ref[...] = m_new

    m_ref[...] = jnp.full((N_HEADS, TQ), NEG, F32)
    l_ref[...] = jnp.zeros((N_HEADS, TQ), F32)
    oT_ref[...] = jnp.zeros((D_ATTN, TQ), F32)
    run_pipeline(pipe_step, nchunks - 1)
    for hh in range(N_HEADS):
        rows = slice(hh * HEAD_DIM, (hh + 1) * HEAD_DIM)
        oT_ref[rows, :] = oT_ref[rows, :] / l_ref[hh:hh + 1, :]

    o_ref[0] = oT_ref[...].T.astype(BF16)


def _merge_kernel(x_ref, mix_ref, att_ref, gmix_ref, wg_ref, bg_ref, wa_ref, wb_ref, wo_ref,
                  gffn_ref, wfg_ref, wfu_ref, wfd_ref, gfin_ref, o_ref, *, final_norm):
    x = x_ref[0]
    h = _rms(x, gmix_ref[...]).astype(BF16)
    gates = jax.nn.sigmoid(jnp.dot(h, wg_ref[...], preferred_element_type=F32) + bg_ref[...])
    y_a = jnp.dot(mix_ref[0], wa_ref[...], preferred_element_type=F32)
    y_b = jnp.dot(att_ref[0], wb_ref[...], preferred_element_type=F32)
    merged = gates[:, :D_MODEL] * y_a + gates[:, D_MODEL:] * y_b
    x1 = x + jnp.dot(merged.astype(BF16), wo_ref[...], preferred_element_type=F32)
    h2 = _rms(x1, gffn_ref[...]).astype(BF16)
    fg = jnp.dot(h2, wfg_ref[...], preferred_element_type=F32)
    fu = jnp.dot(h2, wfu_ref[...], preferred_element_type=F32)
    a = (jax.nn.silu(fg) * fu).astype(BF16)
    x2 = x1 + jnp.dot(a, wfd_ref[...], preferred_element_type=F32)
    o_ref[0] = _rms(x2, gfin_ref[...]) if final_norm else x2


def _const_spec(shape):
    nd = len(shape)
    return pl.BlockSpec(shape, lambda *_: (0,) * nd, pipeline_mode=pl.Buffered(1))


def kernel(x, g_mix, w_in, b_gate, conv_w, w_branch_a, w_branch_b, w_out, rel_bias, g_ffn,
           w_ffn_gate, w_ffn_up, w_ffn_down, g_final):
    bsz, seq, d = x.shape
    assert d == D_MODEL and seq % TM1 == 0 and seq % TQ == 0 and seq % TM3 == 0 and TQ == TK
    k_top = min(TOPK_MAX, seq // 4)
    assert 16 * (seq // TK) <= 256
    nck = seq // TK
    d_ff = w_ffn_gate.shape[-1]
    cparams = functools.partial(pltpu.CompilerParams, vmem_limit_bytes=VMEM_LIMIT)

    for layer in range(g_mix.shape[0]):
        w = w_in[layer]
        o = np.cumsum([0, D_CONV, D_CONV, D_CONV, D_ATTN, HEAD_DIM, HEAD_DIM,
                       N_IDX_HEADS * IDX_DIM, IDX_DIM, N_IDX_HEADS, D_MODEL, D_MODEL])
        w_cb, w_cc, w_cx, w_q, w_k, w_v, w_iq, w_ik, w_iw, w_ga, w_gb = [
            w[:, int(o[j]):int(o[j + 1])] for j in range(11)]
        w_nat = jnp.concatenate([w_cb, w_cc, w_cx, w_k, w_ik], axis=1).astype(BF16)
        w_pad = jnp.zeros((D_MODEL, V_EXTRA - N_IDX_HEADS), w.dtype)
        w_tr = jnp.concatenate([w_q * (HEAD_DIM ** -0.5 * LOG2E), w_iq, w_v, w_iw, w_pad],
                               axis=1).T.astype(BF16)
        w_g = jnp.concatenate([w_ga, w_gb], axis=1).astype(BF16)
        n_nat, n_tr = w_nat.shape[1], w_tr.shape[0]

        mix, k_n, ik_n, qT, iqT, vT, iwT = pl.pallas_call(
            _proj_kernel,
            grid=(bsz, seq // TM1),
            in_specs=[
                pl.BlockSpec((1, TM1, D_MODEL), lambda b, i: (b, i, 0)),
                _const_spec((1, D_MODEL)),
                _const_spec((D_MODEL, n_nat)),
                _const_spec((n_tr, D_MODEL)),
                _const_spec((CONV_K, D_CONV)),
            ],
            out_specs=[
                pl.BlockSpec((1, TM1, D_CONV), lambda b, i: (b, i, 0)),
                pl.BlockSpec((1, TM1, HEAD_DIM), lambda b, i: (b, i, 0)),
                pl.BlockSpec((1, TM1, IDX_DIM), lambda b, i: (b, i, 0)),
                pl.BlockSpec((1, D_ATTN, TM1), lambda b, i: (b, 0, i)),
                pl.BlockSpec((1, N_IDX_HEADS * IDX_DIM, TM1), lambda b, i: (b, 0, i)),
                pl.BlockSpec((1, TM1 // TK, HEAD_DIM + V_EXTRA, TK), lambda b, i: (b, i, 0, 0)),
                pl.BlockSpec((1, N_IDX_HEADS, TM1), lambda b, i: (b, 0, i)),
            ],
            out_shape=[
                jax.ShapeDtypeStruct((bsz, seq, D_CONV), BF16),
                jax.ShapeDtypeStruct((bsz, seq, HEAD_DIM), BF16),
                jax.ShapeDtypeStruct((bsz, seq, IDX_DIM), BF16),
                jax.ShapeDtypeStruct((bsz, D_ATTN, seq), BF16),
                jax.ShapeDtypeStruct((bsz, N_IDX_HEADS * IDX_DIM, seq), BF16),
                jax.ShapeDtypeStruct((bsz, nck, HEAD_DIM + V_EXTRA, TK), BF16),
                jax.ShapeDtypeStruct((bsz, N_IDX_HEADS, seq), F32),
            ],
            scratch_shapes=[pltpu.VMEM((TM1 + 8, D_CONV), F32)],
            compiler_params=cparams(dimension_semantics=("arbitrary", "arbitrary")),
            name="proj",
        )(x, g_mix[layer][None, :], w_nat, w_tr, conv_w[layer])

        attn = pl.pallas_call(
            functools.partial(_dsa_kernel, k_top=k_top),
            grid=(bsz, seq // TQ),
            in_specs=[
                pl.BlockSpec(memory_space=pltpu.SMEM),
                pl.BlockSpec((1, N_IDX_HEADS * IDX_DIM, TQ), lambda b, i: (b, 0, i)),
                pl.BlockSpec((1, N_IDX_HEADS, TQ), lambda b, i: (b, 0, i)),
                pl.BlockSpec((1, seq, IDX_DIM), lambda b, i: (b, 0, 0)),
                pl.BlockSpec((1, D_ATTN, TQ), lambda b, i: (b, 0, i)),
                pl.BlockSpec((1, seq, HEAD_DIM), lambda b, i: (b, 0, 0)),
                pl.BlockSpec((1, nck, HEAD_DIM + V_EXTRA, TK), lambda b, i: (b, 0, 0, 0)),
            ],
            out_specs=pl.BlockSpec((1, TQ, D_ATTN), lambda b, i: (b, i, 0)),
            out_shape=jax.ShapeDtypeStruct((bsz, seq, D_ATTN), BF16),
            scratch_shapes=[
                pltpu.VMEM((nck, TK, TQ), I32),
                pltpu.VMEM((nck, TK, TQ), BF16),
                pltpu.VMEM((nck, TK, TQ), BF16),
                pltpu.VMEM((nck + 2 * N_HEADS, TK, TQ), F32),
                pltpu.VMEM((2, N_HEADS, TK, TQ), F32),
                pltpu.VMEM((D_ATTN, TQ), F32),
                pltpu.VMEM((N_HEADS, TQ), F32),
                pltpu.VMEM((N_HEADS, TQ), F32),
                pltpu.VMEM((N_HEADS, TQ), F32),
                pltpu.VMEM((N_HEADS, TK, TQ), F32),
                pltpu.VMEM((N_HEADS, TK, TQ), F32),
            ],
            compiler_params=cparams(dimension_semantics=("arbitrary", "arbitrary")),
            name="dsa",
        )(rel_bias, iqT, iwT, ik_n, qT, k_n, vT)

        x = pl.pallas_call(
            functools.partial(_merge_kernel, final_norm=layer == g_mix.shape[0] - 1),
            grid=(bsz, seq // TM3),
            in_specs=[
                pl.BlockSpec((1, TM3, D_MODEL), lambda b, i: (b, i, 0)),
                pl.BlockSpec((1, TM3, D_CONV), lambda b, i: (b, i, 0)),
                pl.BlockSpec((1, TM3, D_ATTN), lambda b, i: (b, i, 0)),
                _const_spec((1, D_MODEL)),
                _const_spec((D_MODEL, 2 * D_MODEL)),
                _const_spec((1, 2 * D_MODEL)),
                _const_spec((D_CONV, D_MODEL)),
                _const_spec((D_ATTN, D_MODEL)),
                _const_spec((D_MODEL, D_MODEL)),
                _const_spec((1, D_MODEL)),
                _const_spec((D_MODEL, d_ff)),
                _const_spec((D_MODEL, d_ff)),
                _const_spec((d_ff, D_MODEL)),
                _const_spec((1, D_MODEL)),
            ],
            out_specs=pl.BlockSpec((1, TM3, D_MODEL), lambda b, i: (b, i, 0)),
            out_shape=jax.ShapeDtypeStruct((bsz, seq, D_MODEL), F32),
            compiler_params=cparams(dimension_semantics=("arbitrary", "arbitrary")),
            name="merge",
        )(x, mix, attn, g_mix[layer][None, :], w_g, b_gate[layer][None, :],
          w_branch_a[layer].astype(BF16), w_branch_b[layer].astype(BF16),
          w_out[layer].astype(BF16), g_ffn[layer][None, :],
          w_ffn_gate[layer].astype(BF16), w_ffn_up[layer].astype(BF16),
          w_ffn_down[layer].astype(BF16),
          g_final[None, :])
    return x
```

```python
import functools
import math

import numpy as np
import jax
import jax.numpy as jnp
from jax import lax
from jax.experimental import pallas as pl
from jax.experimental.pallas import tpu as pltpu

D_MODEL = 1024
D_CONV = 512
CONV_K = 3
N_HEADS = 8
HEAD_DIM = 64
D_ATTN = N_HEADS * HEAD_DIM
N_IDX_HEADS = 8
IDX_DIM = 64
TOPK_MAX = 256
N_BUCKETS = 32
MAX_EXACT = 16
MAX_DISTANCE = 128
EPS = 1e-6

F32 = jnp.float32
BF16 = jnp.bfloat16
I32 = jnp.int32
I16 = jnp.int16

INT_MIN = -(2 ** 31)
I16_MIN = -(2 ** 15)
NEG = -1e30
LOG2E = math.log2(math.e)

TM1 = 512
TQ = 256
TK = 256
TM3 = 256
V_EXTRA = 16
MID_BITS = 12
LOW_TAIL_BITS = 4
ONE_BF16_BITS = 0x3F80
NEG_INF_BF16_BITS = -0x80
VMEM_LIMIT = 56 * 1024 * 1024


def _bucket_steps():
    n = np.arange(2 * MAX_DISTANCE)
    large = MAX_EXACT + (np.log(np.maximum(n, 1).astype(np.float32) / MAX_EXACT)
                         / math.log(MAX_DISTANCE / MAX_EXACT)
                         * (N_BUCKETS - MAX_EXACT)).astype(np.int32)
    large = np.minimum(large, N_BUCKETS - 1)
    b = np.where(n < MAX_EXACT, n, large)
    assert np.all(b[MAX_DISTANCE:] == N_BUCKETS - 1) and np.all(np.diff(b) >= 0)
    steps = [(int(i), int(b[i])) for i in range(1, len(b)) if b[i] != b[i - 1]]
    return int(b[0]), steps


def _fold_rows(a, rows):
    parts = [a[i:i + rows, :] for i in range(0, a.shape[0], rows)]
    while len(parts) > 1:
        parts = [parts[i] + parts[i + 1] for i in range(0, len(parts) - 1, 2)] + (
            [parts[-1]] if len(parts) % 2 else [])
    return parts[0]


def _rms(x, g):
    return x * lax.rsqrt(jnp.mean(x * x, axis=-1, keepdims=True) + EPS) * g


def _proj_kernel(x_ref, g_ref, wn_ref, wt_ref, cw_ref,
                 mix_ref, k_ref, ik_ref, qT_ref, iqT_ref, vT_ref, iwT_ref, ubuf_ref):
    i = pl.program_id(1)
    x = x_ref[0]
    h = _rms(x, g_ref[...]).astype(BF16)

    def nat(c0, c1):
        return jnp.dot(h, wn_ref[:, c0:c1], preferred_element_type=F32)

    def tr(r0, r1):
        return lax.dot_general(wt_ref[r0:r1, :], h, (((1,), (1,)), ((), ())),
                               preferred_element_type=F32)

    u = nat(D_CONV, 2 * D_CONV) * nat(2 * D_CONV, 3 * D_CONV)

    @pl.when(i == 0)
    def _():
        ubuf_ref[0:8, :] = jnp.zeros((8, D_CONV), F32)

    ubuf_ref[8:8 + TM1, :] = u
    u1 = ubuf_ref[7:7 + TM1, :]
    u2 = ubuf_ref[6:6 + TM1, :]
    cw = cw_ref[...]
    y = cw[0:1, :] * u2 + cw[1:2, :] * u1 + cw[2:3, :] * u
    mix_ref[0] = (nat(0, D_CONV) * y).astype(BF16)
    ubuf_ref[0:8, :] = u[TM1 - 8:TM1, :]

    kk = nat(3 * D_CONV, 3 * D_CONV + HEAD_DIM + IDX_DIM)
    k_ref[0] = kk[:, 0:HEAD_DIM].astype(BF16)
    ik_ref[0] = kk[:, HEAD_DIM:HEAD_DIM + IDX_DIM].astype(BF16)

    qT_ref[0] = tr(0, D_ATTN).astype(BF16)
    iqT_ref[0] = tr(D_ATTN, 2 * D_ATTN).astype(BF16)
    vw = tr(2 * D_ATTN, 2 * D_ATTN + HEAD_DIM + V_EXTRA)
    vT = vw[0:HEAD_DIM, :].astype(BF16)
    ones_row = (lax.broadcasted_iota(I32, (V_EXTRA, TM1), 0) == 0).astype(F32).astype(BF16)
    vT = jnp.concatenate([vT, ones_row], axis=0)
    for j in range(TM1 // TK):
        vT_ref[0, j] = vT[:, j * TK:(j + 1) * TK]
    iwT_ref[0] = vw[HEAD_DIM:HEAD_DIM + N_IDX_HEADS, :] * (N_IDX_HEADS ** -0.5 * IDX_DIM ** -0.5)


def _dsa_kernel(rb_ref, iqT_ref, iwT_ref, ik_ref, qT_ref, k_ref, vT_ref,
                o_ref, key_ref, khi_ref, klo_ref, add_ref, bias_ref, oT_ref, m_ref, l_ref, al_ref,
                lta_ref, ltb_ref, *, k_top):
    b = pl.program_id(0)
    qi = pl.program_id(1)
    nchunks = qi + 1
    nck = key_ref.shape[0]
    n_keys = nck * TK

    @pl.when((b == 0) & (qi == 0))
    def _():
        b0, steps = _bucket_steps()
        row = lax.broadcasted_iota(I32, (TK, TQ), 0)
        col = lax.broadcasted_iota(I32, (TK, TQ), 1)
        for off in range(2):
            dist = col - row + off * TK
            for hh in range(N_HEADS):
                val = jnp.full((TK, TQ), rb_ref[b0, hh], F32)
                for n0, bk in steps:
                    val = jnp.where(dist >= n0, rb_ref[bk, hh], val)
                bias_ref[off, hh] = (val - rb_ref[N_BUCKETS - 1, hh]) * LOG2E

    w = iwT_ref[0]
    sub = TK // 4

    def score_step(c1, c2, dst_ref, src_ref):
        if c1 is not None:
            ikc = ik_ref[0, pl.ds(pl.multiple_of(c1 * TK, TK), TK), :]
        if c2 is not None:
            col = lax.broadcasted_iota(I32, (sub, TQ), 1) + qi * TQ
        for g in range(4):
            if c1 is not None:
                for hh in range(2 * g, 2 * g + 2):
                    dst_ref[hh] = jnp.dot(ikc, iqT_ref[0, hh * IDX_DIM:(hh + 1) * IDX_DIM, :],
                                          preferred_element_type=F32)
            if c2 is not None:
                rows = slice(g * sub, (g + 1) * sub)
                acc = jnp.zeros((sub, TQ), F32)
                for hh in range(N_IDX_HEADS):
                    acc = acc + w[hh:hh + 1, :] * jnp.maximum(src_ref[hh, rows, :], 0.0)
                zero = acc == 0.0
                bits = pltpu.bitcast(jnp.where(zero, 0.0, acc), I32)
                key = bits ^ ((bits >> 31) & 0x7FFFFFFF)
                row = lax.broadcasted_iota(I32, (sub, TQ), 0) + (c2 * TK + g * sub)
                key = jnp.where(zero, (n_keys - 1 - row) << LOW_TAIL_BITS, key)
                key = jnp.where(row <= col, key, INT_MIN)
                key_ref[c2, rows, :] = key
                hi = jnp.where(row <= col, bits >> 16, NEG_INF_BF16_BITS)
                khi_ref[c2, rows, :] = pltpu.bitcast(hi.astype(I16), BF16)

    def run_pipeline(step, n_steps):
        step(0, None, lta_ref, None)

        def two_steps(j, carry):
            c = 2 * j
            step(c + 1, c, ltb_ref, lta_ref)
            step(c + 2, c + 1, lta_ref, ltb_ref)
            return carry

        lax.fori_loop(0, n_steps // 2, two_steps, 0)

        @pl.when(n_steps % 2 == 1)
        def _():
            step(n_steps, n_steps - 1, ltb_ref, lta_ref)
            step(None, n_steps, None, ltb_ref)

        @pl.when(n_steps % 2 == 0)
        def _():
            step(None, n_steps, None, lta_ref)

    run_pipeline(score_step, nchunks - 1)

    @pl.when(nchunks % 2 == 1)
    def _():
        khi_ref[nchunks] = jnp.full((TK, TQ), -jnp.inf, BF16)
        klo_ref[nchunks] = jnp.zeros((TK, TQ), BF16)

    def count(pred):
        def body(c, acc):
            m = pred(key_ref[c], c).astype(I32)
            return acc + jnp.sum(m.reshape(TK // 8, 8, TQ), axis=0)
        acc = lax.fori_loop(0, nchunks, body, jnp.zeros((8, TQ), I32))
        return jnp.sum(acc, axis=0, keepdims=True)

    def count16(ref, pred):
        def ones(c):
            return _fold_rows(jnp.where(pred(ref[c]), BF16(1), BF16(0)), 16)

        def pair(j, acc):
            return acc + (ones(2 * j) + ones(2 * j + 1))

        acc = lax.fori_loop(0, (nchunks + 1) // 2, pair, jnp.zeros((16, TQ), BF16))
        return jnp.sum(acc.astype(F32), axis=0, keepdims=True).astype(I32)

    def as_bf16(bits):
        signed = ((bits + 2 ** 15) & 0xFFFF) - 2 ** 15
        return pltpu.bitcast(jnp.broadcast_to(signed, (16, TQ)).astype(I16), BF16)[0:1, :]

    def hi_pattern(key16):
        return key16 ^ ((key16 >> 15) & 0x7FFF)

    def hi_candidate(key16):
        bits = hi_pattern(key16)
        mag = bits & 0x7FFF
        bits = jnp.where(mag < 0x80, jnp.where((bits < 0) | (mag == 0), 0, 0x80), bits)
        return as_bf16(bits)

    def descend(ref, nbits, want, candidate, v0, cnt0):
        def step(j, vc):
            v, cv = vc
            cand = v + jnp.left_shift(jnp.int32(1), nbits - 1 - j)
            cand16 = candidate(cand)
            cnt = count16(ref, lambda kc: kc >= cand16)
            ok = cnt >= want
            return jnp.where(ok, cand, v), jnp.where(ok, cnt, cv)
        return lax.fori_loop(0, nbits, step, (v0, cnt0))

    def digit(d):
        return as_bf16(ONE_BF16_BITS + d)

    unknown = jnp.full((1, TQ), 2 ** 30, I32)
    thr_hi, cnt_hi = descend(khi_ref, 16, k_top, hi_candidate, jnp.full((1, TQ), I16_MIN, I32),
                             unknown)
    thr_hi16 = as_bf16(hi_pattern(thr_hi))
    room_hi = k_top - count16(khi_ref, lambda kc: kc > thr_hi16)

    def mid_chunk(c, carry):
        d12 = (key_ref[c] >> LOW_TAIL_BITS) & (2 ** MID_BITS - 1)
        pat = pltpu.bitcast((ONE_BF16_BITS + d12).astype(I16), BF16)
        klo_ref[c] = jnp.where(khi_ref[c] == thr_hi16, pat, BF16(0))
        return carry

    lax.fori_loop(0, nchunks, mid_chunk, 0)
    n_cand = cnt_hi - (k_top - room_hi)
    v_mid, cnt_mid = descend(klo_ref, MID_BITS, room_hi, digit, jnp.zeros((1, TQ), I32), n_cand)
    unresolved = jnp.max(((cnt_mid != room_hi) & (thr_hi > I16_MIN)).astype(I32)) > 0

    def low_tail(_):
        mid16 = digit(v_mid)
        want = room_hi - count16(klo_ref, lambda kc: kc > mid16)

        def tail_chunk(c, carry):
            pat = pltpu.bitcast((ONE_BF16_BITS + (key_ref[c] & (2 ** LOW_TAIL_BITS - 1))).astype(I16),
                                BF16)
            klo_ref[c] = jnp.where(klo_ref[c] == mid16, pat, BF16(0))
            return carry

        lax.fori_loop(0, nchunks, tail_chunk, 0)
        v_low, cnt_low = descend(klo_ref, LOW_TAIL_BITS, want, digit, jnp.zeros((1, TQ), I32),
                                 cnt_mid - (room_hi - want))
        return v_mid * 2 ** LOW_TAIL_BITS + v_low, (cnt_low > want).astype(I32)

    thr_lo, excess = lax.cond(unresolved, low_tail,
                              lambda _: (v_mid * 2 ** LOW_TAIL_BITS, jnp.zeros((1, TQ), I32)), 0)
    thr = thr_hi * 2 ** 16 + thr_lo
    any_excess = jnp.max(excess * (thr > INT_MIN).astype(I32)) > 0
    thr = jnp.maximum(thr, INT_MIN + 1)

    @pl.when(jnp.logical_not(any_excess))
    def _():
        def mask_chunk(c, carry):
            add_ref[c] = jnp.where(key_ref[c] >= thr, 0.0, NEG).astype(F32)
            return carry
        lax.fori_loop(0, nchunks, mask_chunk, 0)

    @pl.when(any_excess)
    def _():
        room = k_top - count(lambda kc, c: kc > thr)

        def key_index(c):
            return lax.broadcasted_iota(I32, (TK, TQ), 0) + c * TK

        def step(it, lim):
            cand = lim + jnp.left_shift(jnp.int32(1), 12 - it)
            cnt = count(lambda kc, c: (kc == thr) & (key_index(c) < cand))
            return jnp.where(cnt <= room, cand, lim)

        tie_lim = lax.fori_loop(0, 13, step, jnp.zeros((1, TQ), I32))

        def mask_chunk(c, carry):
            kc = key_ref[c]
            sel = (kc > thr) | ((kc == thr) & (key_index(c) < tie_lim))
            add_ref[c] = jnp.where(sel, 0.0, NEG).astype(F32)
            return carry
        lax.fori_loop(0, nchunks, mask_chunk, 0)

    for off in range(2):
        @pl.when(qi >= off)
        def _():
            nm = add_ref[qi - off]
            for hh in range(N_HEADS):
                add_ref[nck + off * N_HEADS + hh] = nm + bias_ref[off, hh]

    def pipe_step(c1, c2, dst_ref, src_ref):
        m_cur = m_ref[...]
        m_rows, l_rows = [], []
        if c1 is not None:
            kc = k_ref[0, pl.ds(pl.multiple_of(c1 * TK, TK), TK), :]
            near = c1 >= qi - 1
        if c2 is not None:
            vc = vT_ref[0, c2]
            alpha = al_ref[...]
        for hh in range(N_HEADS):
            rows = slice(hh * HEAD_DIM, (hh + 1) * HEAD_DIM)
            if c1 is not None:
                tile = jnp.where(near, nck + (qi - c1) * N_HEADS + hh, c1)
                lt = jnp.dot(kc, qT_ref[0, rows, :], preferred_element_type=F32) + add_ref[tile]
                dst_ref[hh] = lt
                m_rows.append(jnp.max(lt, axis=0, keepdims=True))
            if c2 is not None:
                p = jnp.exp2(src_ref[hh] - m_cur[hh:hh + 1, :]).astype(BF16)
                pv = jnp.dot(vc, p, preferred_element_type=F32)
                oT_ref[rows, :] = alpha[hh:hh + 1, :] * oT_ref[rows, :] + pv[0:HEAD_DIM, :]
                l_rows.append(pv[HEAD_DIM:HEAD_DIM + 1, :])
        if c2 is not None:
            l_ref[...] = alpha * l_ref[...] + jnp.concatenate(l_rows, axis=0)
        if c1 is not None:
            m_new = jnp.maximum(m_cur, jnp.concatenate(m_rows, axis=0))
            al_ref[...] = jnp.exp2(m_cur - m_new)
            m_ref[...] = m_new

    m_ref[...] = jnp.full((N_HEADS, TQ), NEG, F32)
    l_ref[...] = jnp.zeros((N_HEADS, TQ), F32)
    oT_ref[...] = jnp.zeros((D_ATTN, TQ), F32)
    run_pipeline(pipe_step, nchunks - 1)
    for hh in range(N_HEADS):
        rows = slice(hh * HEAD_DIM, (hh + 1) * HEAD_DIM)
        oT_ref[rows, :] = oT_ref[rows, :] / l_ref[hh:hh + 1, :]

    o_ref[0] = oT_ref[...].T.astype(BF16)


def _merge_kernel(x_ref, mix_ref, att_ref, gmix_ref, wg_ref, bg_ref, wa_ref, wb_ref, wo_ref,
                  gffn_ref, wfg_ref, wfu_ref, wfd_ref, gfin_ref, o_ref, *, final_norm):
    x = x_ref[0]
    h = _rms(x, gmix_ref[...]).astype(BF16)
    gates = jax.nn.sigmoid(jnp.dot(h, wg_ref[...], preferred_element_type=F32) + bg_ref[...])
    y_a = jnp.dot(mix_ref[0], wa_ref[...], preferred_element_type=F32)
    y_b = jnp.dot(att_ref[0], wb_ref[...], preferred_element_type=F32)
    merged = gates[:, :D_MODEL] * y_a + gates[:, D_MODEL:] * y_b
    x1 = x + jnp.dot(merged.astype(BF16), wo_ref[...], preferred_element_type=F32)
    h2 = _rms(x1, gffn_ref[...]).astype(BF16)
    fg = jnp.dot(h2, wfg_ref[...], preferred_element_type=F32)
    fu = jnp.dot(h2, wfu_ref[...], preferred_element_type=F32)
    a = (jax.nn.silu(fg) * fu).astype(BF16)
    x2 = x1 + jnp.dot(a, wfd_ref[...], preferred_element_type=F32)
    o_ref[0] = _rms(x2, gfin_ref[...]) if final_norm else x2


def _const_spec(shape):
    nd = len(shape)
    return pl.BlockSpec(shape, lambda *_: (0,) * nd, pipeline_mode=pl.Buffered(1))


def kernel(x, g_mix, w_in, b_gate, conv_w, w_branch_a, w_branch_b, w_out, rel_bias, g_ffn,
           w_ffn_gate, w_ffn_up, w_ffn_down, g_final):
    bsz, seq, d = x.shape
    assert d == D_MODEL and seq % TM1 == 0 and seq % TQ == 0 and seq % TM3 == 0 and TQ == TK
    k_top = min(TOPK_MAX, seq // 4)
    assert 16 * (seq // TK) <= 256
    assert seq <= 2 ** MID_BITS
    nck = seq // TK
    d_ff = w_ffn_gate.shape[-1]
    cparams = functools.partial(pltpu.CompilerParams, vmem_limit_bytes=VMEM_LIMIT)

    for layer in range(g_mix.shape[0]):
        w = w_in[layer]
        o = np.cumsum([0, D_CONV, D_CONV, D_CONV, D_ATTN, HEAD_DIM, HEAD_DIM,
                       N_IDX_HEADS * IDX_DIM, IDX_DIM, N_IDX_HEADS, D_MODEL, D_MODEL])
        w_cb, w_cc, w_cx, w_q, w_k, w_v, w_iq, w_ik, w_iw, w_ga, w_gb = [
            w[:, int(o[j]):int(o[j + 1])] for j in range(11)]
        w_nat = jnp.concatenate([w_cb, w_cc, w_cx, w_k, w_ik], axis=1).astype(BF16)
        w_pad = jnp.zeros((D_MODEL, V_EXTRA - N_IDX_HEADS), w.dtype)
        w_tr = jnp.concatenate([w_q * (HEAD_DIM ** -0.5 * LOG2E), w_iq, w_v, w_iw, w_pad],
                               axis=1).T.astype(BF16)
        w_g = jnp.concatenate([w_ga, w_gb], axis=1).astype(BF16)
        n_nat, n_tr = w_nat.shape[1], w_tr.shape[0]

        mix, k_n, ik_n, qT, iqT, vT, iwT = pl.pallas_call(
            _proj_kernel,
            grid=(bsz, seq // TM1),
            in_specs=[
                pl.BlockSpec((1, TM1, D_MODEL), lambda b, i: (b, i, 0)),
                _const_spec((1, D_MODEL)),
                _const_spec((D_MODEL, n_nat)),
                _const_spec((n_tr, D_MODEL)),
                _const_spec((CONV_K, D_CONV)),
            ],
            out_specs=[
                pl.BlockSpec((1, TM1, D_CONV), lambda b, i: (b, i, 0)),
                pl.BlockSpec((1, TM1, HEAD_DIM), lambda b, i: (b, i, 0)),
                pl.BlockSpec((1, TM1, IDX_DIM), lambda b, i: (b, i, 0)),
                pl.BlockSpec((1, D_ATTN, TM1), lambda b, i: (b, 0, i)),
                pl.BlockSpec((1, N_IDX_HEADS * IDX_DIM, TM1), lambda b, i: (b, 0, i)),
                pl.BlockSpec((1, TM1 // TK, HEAD_DIM + V_EXTRA, TK), lambda b, i: (b, i, 0, 0)),
                pl.BlockSpec((1, N_IDX_HEADS, TM1), lambda b, i: (b, 0, i)),
            ],
            out_shape=[
                jax.ShapeDtypeStruct((bsz, seq, D_CONV), BF16),
                jax.ShapeDtypeStruct((bsz, seq, HEAD_DIM), BF16),
                jax.ShapeDtypeStruct((bsz, seq, IDX_DIM), BF16),
                jax.ShapeDtypeStruct((bsz, D_ATTN, seq), BF16),
                jax.ShapeDtypeStruct((bsz, N_IDX_HEADS * IDX_DIM, seq), BF16),
                jax.ShapeDtypeStruct((bsz, nck, HEAD_DIM + V_EXTRA, TK), BF16),
                jax.ShapeDtypeStruct((bsz, N_IDX_HEADS, seq), F32),
            ],
            scratch_shapes=[pltpu.VMEM((TM1 + 8, D_CONV), F32)],
            compiler_params=cparams(dimension_semantics=("arbitrary", "arbitrary")),
            name="proj",
        )(x, g_mix[layer][None, :], w_nat, w_tr, conv_w[layer])

        attn = pl.pallas_call(
            functools.partial(_dsa_kernel, k_top=k_top),
            grid=(bsz, seq // TQ),
            in_specs=[
                pl.BlockSpec(memory_space=pltpu.SMEM),
                pl.BlockSpec((1, N_IDX_HEADS * IDX_DIM, TQ), lambda b, i: (b, 0, i)),
                pl.BlockSpec((1, N_IDX_HEADS, TQ), lambda b, i: (b, 0, i)),
                pl.BlockSpec((1, seq, IDX_DIM), lambda b, i: (b, 0, 0)),
                pl.BlockSpec((1, D_ATTN, TQ), lambda b, i: (b, 0, i)),
                pl.BlockSpec((1, seq, HEAD_DIM), lambda b, i: (b, 0, 0)),
                pl.BlockSpec((1, nck, HEAD_DIM + V_EXTRA, TK), lambda b, i: (b, 0, 0, 0)),
            ],
            out_specs=pl.BlockSpec((1, TQ, D_ATTN), lambda b, i: (b, i, 0)),
            out_shape=jax.ShapeDtypeStruct((bsz, seq, D_ATTN), BF16),
            scratch_shapes=[
                pltpu.VMEM((nck, TK, TQ), I32),
                pltpu.VMEM((nck, TK, TQ), BF16),
                pltpu.VMEM((nck, TK, TQ), BF16),
                pltpu.VMEM((nck + 2 * N_HEADS, TK, TQ), F32),
                pltpu.VMEM((2, N_HEADS, TK, TQ), F32),
                pltpu.VMEM((D_ATTN, TQ), F32),
                pltpu.VMEM((N_HEADS, TQ), F32),
                pltpu.VMEM((N_HEADS, TQ), F32),
                pltpu.VMEM((N_HEADS, TQ), F32),
                pltpu.VMEM((N_HEADS, TK, TQ), F32),
                pltpu.VMEM((N_HEADS, TK, TQ), F32),
            ],
            compiler_params=cparams(dimension_semantics=("arbitrary", "arbitrary")),
            name="dsa",
        )(rel_bias, iqT, iwT, ik_n, qT, k_n, vT)

        x = pl.pallas_call(
            functools.partial(_merge_kernel, final_norm=layer == g_mix.shape[0] - 1),
            grid=(bsz, seq // TM3),
            in_specs=[
                pl.BlockSpec((1, TM3, D_MODEL), lambda b, i: (b, i, 0)),
                pl.BlockSpec((1, TM3, D_CONV), lambda b, i: (b, i, 0)),
                pl.BlockSpec((1, TM3, D_ATTN), lambda b, i: (b, i, 0)),
                _const_spec((1, D_MODEL)),
                _const_spec((D_MODEL, 2 * D_MODEL)),
                _const_spec((1, 2 * D_MODEL)),
                _const_spec((D_CONV, D_MODEL)),
                _const_spec((D_ATTN, D_MODEL)),
                _const_spec((D_MODEL, D_MODEL)),
                _const_spec((1, D_MODEL)),
                _const_spec((D_MODEL, d_ff)),
                _const_spec((D_MODEL, d_ff)),
                _const_spec((d_ff, D_MODEL)),
                _const_spec((1, D_MODEL)),
            ],
            out_specs=pl.BlockSpec((1, TM3, D_MODEL), lambda b, i: (b, i, 0)),
            out_shape=jax.ShapeDtypeStruct((bsz, seq, D_MODEL), F32),
            compiler_params=cparams(dimension_semantics=("arbitrary", "arbitrary")),
            name="merge",
        )(x, mix, attn, g_mix[layer][None, :], w_g, b_gate[layer][None, :],
          w_branch_a[layer].astype(BF16), w_branch_b[layer].astype(BF16),
          w_out[layer].astype(BF16), g_ffn[layer][None, :],
          w_ffn_gate[layer].astype(BF16), w_ffn_up[layer].astype(BF16),
          w_ffn_down[layer].astype(BF16),
          g_final[None, :])
    return x
```

```python
import functools
import math

import numpy as np
import jax
import jax.numpy as jnp
from jax import lax
from jax.experimental import pallas as pl
from jax.experimental.pallas import tpu as pltpu

D_MODEL = 1024
D_CONV = 512
CONV_K = 3
N_HEADS = 8
HEAD_DIM = 64
D_ATTN = N_HEADS * HEAD_DIM
N_IDX_HEADS = 8
IDX_DIM = 64
TOPK_MAX = 256
N_BUCKETS = 32
MAX_EXACT = 16
MAX_DISTANCE = 128
EPS = 1e-6

F32 = jnp.float32
BF16 = jnp.bfloat16
I32 = jnp.int32
I16 = jnp.int16

INT_MIN = -(2 ** 31)
I16_MIN = -(2 ** 15)
NEG = -1e30
LOG2E = math.log2(math.e)

TM1 = 512
TQ = 256
TK = 256
TM3 = 256
V_EXTRA = 16
MID_BITS = 12
LOW_TAIL_BITS = 4
ONE_BF16_BITS = 0x3F80
NEG_INF_BF16_BITS = -0x80
STAGE1_LEAD = 2
VMEM_LIMIT = 56 * 1024 * 1024


def _bucket_steps():
    n = np.arange(2 * MAX_DISTANCE)
    large = MAX_EXACT + (np.log(np.maximum(n, 1).astype(np.float32) / MAX_EXACT)
                         / math.log(MAX_DISTANCE / MAX_EXACT)
                         * (N_BUCKETS - MAX_EXACT)).astype(np.int32)
    large = np.minimum(large, N_BUCKETS - 1)
    b = np.where(n < MAX_EXACT, n, large)
    assert np.all(b[MAX_DISTANCE:] == N_BUCKETS - 1) and np.all(np.diff(b) >= 0)
    steps = [(int(i), int(b[i])) for i in range(1, len(b)) if b[i] != b[i - 1]]
    return int(b[0]), steps


def _fold_rows(a, rows):
    parts = [a[i:i + rows, :] for i in range(0, a.shape[0], rows)]
    while len(parts) > 1:
        parts = [parts[i] + parts[i + 1] for i in range(0, len(parts) - 1, 2)] + (
            [parts[-1]] if len(parts) % 2 else [])
    return parts[0]


def _rms(x, g):
    return x * lax.rsqrt(jnp.mean(x * x, axis=-1, keepdims=True) + EPS) * g


def _proj_kernel(x_ref, g_ref, wn_ref, wt_ref, cw_ref,
                 mix_ref, k_ref, ik_ref, qT_ref, iqT_ref, vT_ref, iwT_ref, ubuf_ref):
    i = pl.program_id(1)
    x = x_ref[0]
    h = _rms(x, g_ref[...]).astype(BF16)

    def nat(c0, c1):
        return jnp.dot(h, wn_ref[:, c0:c1], preferred_element_type=F32)

    def tr(r0, r1):
        return lax.dot_general(wt_ref[r0:r1, :], h, (((1,), (1,)), ((), ())),
                               preferred_element_type=F32)

    u = nat(D_CONV, 2 * D_CONV) * nat(2 * D_CONV, 3 * D_CONV)

    @pl.when(i == 0)
    def _():
        ubuf_ref[0:8, :] = jnp.zeros((8, D_CONV), F32)

    ubuf_ref[8:8 + TM1, :] = u
    u1 = ubuf_ref[7:7 + TM1, :]
    u2 = ubuf_ref[6:6 + TM1, :]
    cw = cw_ref[...]
    y = cw[0:1, :] * u2 + cw[1:2, :] * u1 + cw[2:3, :] * u
    mix_ref[0] = (nat(0, D_CONV) * y).astype(BF16)
    ubuf_ref[0:8, :] = u[TM1 - 8:TM1, :]

    kk = nat(3 * D_CONV, 3 * D_CONV + HEAD_DIM + IDX_DIM)
    k_ref[0] = kk[:, 0:HEAD_DIM].astype(BF16)
    ik_ref[0] = kk[:, HEAD_DIM:HEAD_DIM + IDX_DIM].astype(BF16)

    qT_ref[0] = tr(0, D_ATTN).astype(BF16)
    iqT_ref[0] = tr(D_ATTN, 2 * D_ATTN).astype(BF16)
    vw = tr(2 * D_ATTN, 2 * D_ATTN + HEAD_DIM + V_EXTRA)
    vT = vw[0:HEAD_DIM, :].astype(BF16)
    ones_row = (lax.broadcasted_iota(I32, (V_EXTRA, TM1), 0) == 0).astype(F32).astype(BF16)
    vT = jnp.concatenate([vT, ones_row], axis=0)
    for j in range(TM1 // TK):
        vT_ref[0, j] = vT[:, j * TK:(j + 1) * TK]
    iwT_ref[0] = vw[HEAD_DIM:HEAD_DIM + N_IDX_HEADS, :] * (N_IDX_HEADS ** -0.5 * IDX_DIM ** -0.5)


def _dsa_kernel(rb_ref, iqT_ref, iwT_ref, ik_ref, qT_ref, k_ref, vT_ref,
                o_ref, key_ref, khi_ref, klo_ref, add_ref, bias_ref, oT_ref, m_ref, l_ref, al_ref,
                lta_ref, ltb_ref, *, k_top):
    b = pl.program_id(0)
    qi = pl.program_id(1)
    nchunks = qi + 1
    nck = key_ref.shape[0]
    n_keys = nck * TK

    @pl.when((b == 0) & (qi == 0))
    def _():
        b0, steps = _bucket_steps()
        row = lax.broadcasted_iota(I32, (TK, TQ), 0)
        col = lax.broadcasted_iota(I32, (TK, TQ), 1)
        for off in range(2):
            dist = col - row + off * TK
            for hh in range(N_HEADS):
                val = jnp.full((TK, TQ), rb_ref[b0, hh], F32)
                for n0, bk in steps:
                    val = jnp.where(dist >= n0, rb_ref[bk, hh], val)
                bias_ref[off, hh] = (val - rb_ref[N_BUCKETS - 1, hh]) * LOG2E

    w = iwT_ref[0]
    sub = TK // 4

    def score_step(c1, c2, dst_ref, src_ref, diagonal=False):
        if c1 is not None:
            ikc = ik_ref[0, pl.ds(pl.multiple_of(c1 * TK, TK), TK), :]
        if c2 is not None:
            row0 = lax.broadcasted_iota(I32, (sub, TQ), 0)
            lane = lax.broadcasted_iota(I32, (sub, TQ), 1)
        for g in range(4):
            if c1 is not None:
                for hh in range(2 * g, 2 * g + 2):
                    dst_ref[hh] = jnp.dot(ikc, iqT_ref[0, hh * IDX_DIM:(hh + 1) * IDX_DIM, :],
                                          preferred_element_type=F32)
            if c2 is not None:
                rows = slice(g * sub, (g + 1) * sub)
                acc = jnp.zeros((sub, TQ), F32)
                for hh in range(N_IDX_HEADS):
                    acc = acc + w[hh:hh + 1, :] * jnp.maximum(src_ref[hh, rows, :], 0.0)
                zero = acc == 0.0
                bits = pltpu.bitcast(jnp.where(zero, 0.0, acc), I32)
                key = bits ^ ((bits >> 31) & 0x7FFFFFFF)
                hi = bits >> 16
                first = c2 * TK + g * sub
                key = jnp.where(zero, ((n_keys - 1 - first) - row0) << LOW_TAIL_BITS, key)
                if diagonal:
                    causal = row0 + (g * sub) <= lane
                    key = jnp.where(causal, key, INT_MIN)
                    hi = jnp.where(causal, hi, NEG_INF_BF16_BITS)
                key_ref[c2, rows, :] = key
                khi_ref[c2, rows, :] = pltpu.bitcast(hi.astype(I16), BF16)

    def run_pipeline(step, n_steps):
        step(0, None, lta_ref, None)

        def two_steps(j, carry):
            c = 2 * j
            step(c + 1, c, ltb_ref, lta_ref)
            step(c + 2, c + 1, lta_ref, ltb_ref)
            return carry

        lax.fori_loop(0, n_steps // 2, two_steps, 0)

        @pl.when(n_steps % 2 == 1)
        def _():
            step(n_steps, n_steps - 1, ltb_ref, lta_ref)
            step(None, n_steps, None, ltb_ref, diagonal=True)

        @pl.when(n_steps % 2 == 0)
        def _():
            step(None, n_steps, None, lta_ref, diagonal=True)

    run_pipeline(score_step, nchunks - 1)

    @pl.when(nchunks % 2 == 1)
    def _():
        khi_ref[nchunks] = jnp.full((TK, TQ), -jnp.inf, BF16)
        klo_ref[nchunks] = jnp.zeros((TK, TQ), BF16)

    def count(pred):
        def body(c, acc):
            m = pred(key_ref[c], c).astype(I32)
            return acc + jnp.sum(m.reshape(TK // 8, 8, TQ), axis=0)
        acc = lax.fori_loop(0, nchunks, body, jnp.zeros((8, TQ), I32))
        return jnp.sum(acc, axis=0, keepdims=True)

    def count16(ref, pred):
        def ones(c):
            return _fold_rows(jnp.where(pred(ref[c]), BF16(1), BF16(0)), 16)

        def pair(j, acc):
            return acc + (ones(2 * j) + ones(2 * j + 1))

        acc = lax.fori_loop(0, (nchunks + 1) // 2, pair, jnp.zeros((16, TQ), BF16))
        return jnp.sum(acc.astype(F32), axis=0, keepdims=True).astype(I32)

    def as_bf16(bits):
        signed = ((bits + 2 ** 15) & 0xFFFF) - 2 ** 15
        return pltpu.bitcast(jnp.broadcast_to(signed, (16, TQ)).astype(I16), BF16)[0:1, :]

    def hi_pattern(key16):
        return key16 ^ ((key16 >> 15) & 0x7FFF)

    def hi_candidate(key16):
        bits = hi_pattern(key16)
        mag = bits & 0x7FFF
        bits = jnp.where(mag < 0x80, jnp.where((bits < 0) | (mag == 0), 0, 0x80), bits)
        return as_bf16(bits)

    def descend(ref, nbits, want, candidate, v0, cnt0):
        def step(j, vc):
            v, cv = vc
            cand = v + jnp.left_shift(jnp.int32(1), nbits - 1 - j)
            cand16 = candidate(cand)
            cnt = count16(ref, lambda kc: kc >= cand16)
            ok = cnt >= want
            return jnp.where(ok, cand, v), jnp.where(ok, cnt, cv)
        return lax.fori_loop(0, nbits, step, (v0, cnt0))

    def digit(d):
        return as_bf16(ONE_BF16_BITS + d)

    unknown = jnp.full((1, TQ), 2 ** 30, I32)
    thr_hi, cnt_hi = descend(khi_ref, 16, k_top, hi_candidate, jnp.full((1, TQ), I16_MIN, I32),
                             unknown)
    thr_hi16 = as_bf16(hi_pattern(thr_hi))
    room_hi = k_top - count16(khi_ref, lambda kc: kc > thr_hi16)

    def mid_chunk(c, carry):
        d12 = (key_ref[c] >> LOW_TAIL_BITS) & (2 ** MID_BITS - 1)
        pat = pltpu.bitcast((ONE_BF16_BITS + d12).astype(I16), BF16)
        klo_ref[c] = jnp.where(khi_ref[c] == thr_hi16, pat, BF16(0))
        return carry

    lax.fori_loop(0, nchunks, mid_chunk, 0)
    n_cand = cnt_hi - (k_top - room_hi)
    v_mid, cnt_mid = descend(klo_ref, MID_BITS, room_hi, digit, jnp.zeros((1, TQ), I32), n_cand)
    unresolved = jnp.max(((cnt_mid != room_hi) & (thr_hi > I16_MIN)).astype(I32)) > 0

    def low_tail(_):
        mid16 = digit(v_mid)
        want = room_hi - count16(klo_ref, lambda kc: kc > mid16)

        def tail_chunk(c, carry):
            pat = pltpu.bitcast((ONE_BF16_BITS + (key_ref[c] & (2 ** LOW_TAIL_BITS - 1))).astype(I16),
                                BF16)
            klo_ref[c] = jnp.where(klo_ref[c] == mid16, pat, BF16(0))
            return carry

        lax.fori_loop(0, nchunks, tail_chunk, 0)
        v_low, cnt_low = descend(klo_ref, LOW_TAIL_BITS, want, digit, jnp.zeros((1, TQ), I32),
                                 cnt_mid - (room_hi - want))
        return v_mid * 2 ** LOW_TAIL_BITS + v_low, (cnt_low > want).astype(I32)

    thr_lo, excess = lax.cond(unresolved, low_tail,
                              lambda _: (v_mid * 2 ** LOW_TAIL_BITS, jnp.zeros((1, TQ), I32)), 0)
    thr = thr_hi * 2 ** 16 + thr_lo
    any_excess = jnp.max(excess * (thr > INT_MIN).astype(I32)) > 0
    thr = jnp.maximum(thr, INT_MIN + 1)

    @pl.when(jnp.logical_not(any_excess))
    def _():
        def mask_chunk(c, carry):
            add_ref[c] = jnp.where(key_ref[c] >= thr, 0.0, NEG).astype(F32)
            return carry
        lax.fori_loop(0, nchunks, mask_chunk, 0)

    @pl.when(any_excess)
    def _():
        room = k_top - count(lambda kc, c: kc > thr)

        def key_index(c):
            return lax.broadcasted_iota(I32, (TK, TQ), 0) + c * TK

        def step(it, lim):
            cand = lim + jnp.left_shift(jnp.int32(1), 12 - it)
            cnt = count(lambda kc, c: (kc == thr) & (key_index(c) < cand))
            return jnp.where(cnt <= room, cand, lim)

        tie_lim = lax.fori_loop(0, 13, step, jnp.zeros((1, TQ), I32))

        def mask_chunk(c, carry):
            kc = key_ref[c]
            sel = (kc > thr) | ((kc == thr) & (key_index(c) < tie_lim))
            add_ref[c] = jnp.where(sel, 0.0, NEG).astype(F32)
            return carry
        lax.fori_loop(0, nchunks, mask_chunk, 0)

    for off in range(2):
        @pl.when(qi >= off)
        def _():
            nm = add_ref[qi - off]
            for hh in range(N_HEADS):
                add_ref[nck + off * N_HEADS + hh] = nm + bias_ref[off, hh]

    def pipe_step(c1, c2, dst_ref, src_ref, diagonal=False):
        m_cur = m_ref[...]
        m_rows, l_rows = [], []
        if c1 is not None:
            kc = k_ref[0, pl.ds(pl.multiple_of(c1 * TK, TK), TK), :]
            near = c1 >= qi - 1
        if c2 is not None:
            vc = vT_ref[0, c2]
            alpha = al_ref[...]
        for i in range(N_HEADS + STAGE1_LEAD):
            if c1 is not None and i < N_HEADS:
                hh = i
                rows = slice(hh * HEAD_DIM, (hh + 1) * HEAD_DIM)
                tile = jnp.where(near, nck + (qi - c1) * N_HEADS + hh, c1)
                lt = jnp.dot(kc, qT_ref[0, rows, :], preferred_element_type=F32) + add_ref[tile]
                dst_ref[hh] = lt
                m_rows.append(jnp.max(lt, axis=0, keepdims=True))
                if hh == N_HEADS - 1:
                    m_new = jnp.maximum(m_cur, jnp.concatenate(m_rows, axis=0))
                    al_ref[...] = jnp.exp2(m_cur - m_new)
                    m_ref[...] = m_new
            if c2 is not None and i >= STAGE1_LEAD:
                hh = i - STAGE1_LEAD
                rows = slice(hh * HEAD_DIM, (hh + 1) * HEAD_DIM)
                p = jnp.exp2(src_ref[hh] - m_cur[hh:hh + 1, :]).astype(BF16)
                pv = jnp.dot(vc, p, preferred_element_type=F32)
                oT_ref[rows, :] = alpha[hh:hh + 1, :] * oT_ref[rows, :] + pv[0:HEAD_DIM, :]
                l_rows.append(pv[HEAD_DIM:HEAD_DIM + 1, :])
        if c2 is not None:
            l_ref[...] = alpha * l_ref[...] + jnp.concatenate(l_rows, axis=0)

    m_ref[...] = jnp.full((N_HEADS, TQ), NEG, F32)
    l_ref[...] = jnp.zeros((N_HEADS, TQ), F32)
    oT_ref[...] = jnp.zeros((D_ATTN, TQ), F32)
    run_pipeline(pipe_step, nchunks - 1)
    for hh in range(N_HEADS):
        rows = slice(hh * HEAD_DIM, (hh + 1) * HEAD_DIM)
        oT_ref[rows, :] = oT_ref[rows, :] / l_ref[hh:hh + 1, :]

    o_ref[0] = oT_ref[...].T.astype(BF16)


def _merge_kernel(x_ref, mix_ref, att_ref, gmix_ref, wg_ref, bg_ref, wa_ref, wb_ref, wo_ref,
                  gffn_ref, wfg_ref, wfu_ref, wfd_ref, gfin_ref, o_ref, *, final_norm):
    x = x_ref[0]
    h = _rms(x, gmix_ref[...]).astype(BF16)
    gates = jax.nn.sigmoid(jnp.dot(h, wg_ref[...], preferred_element_type=F32) + bg_ref[...])
    y_a = jnp.dot(mix_ref[0], wa_ref[...], preferred_element_type=F32)
    y_b = jnp.dot(att_ref[0], wb_ref[...], preferred_element_type=F32)
    merged = gates[:, :D_MODEL] * y_a + gates[:, D_MODEL:] * y_b
    x1 = x + jnp.dot(merged.astype(BF16), wo_ref[...], preferred_element_type=F32)
    h2 = _rms(x1, gffn_ref[...]).astype(BF16)
    fg = jnp.dot(h2, wfg_ref[...], preferred_element_type=F32)
    fu = jnp.dot(h2, wfu_ref[...], preferred_element_type=F32)
    a = (jax.nn.silu(fg) * fu).astype(BF16)
    x2 = x1 + jnp.dot(a, wfd_ref[...], preferred_element_type=F32)
    o_ref[0] = _rms(x2, gfin_ref[...]) if final_norm else x2


def _const_spec(shape):
    nd = len(shape)
    return pl.BlockSpec(shape, lambda *_: (0,) * nd, pipeline_mode=pl.Buffered(1))


def kernel(x, g_mix, w_in, b_gate, conv_w, w_branch_a, w_branch_b, w_out, rel_bias, g_ffn,
           w_ffn_gate, w_ffn_up, w_ffn_down, g_final):
    bsz, seq, d = x.shape
    assert d == D_MODEL and seq % TM1 == 0 and seq % TQ == 0 and seq % TM3 == 0 and TQ == TK
    k_top = min(TOPK_MAX, seq // 4)
    assert 16 * (seq // TK) <= 256
    assert seq <= 2 ** MID_BITS
    nck = seq // TK
    d_ff = w_ffn_gate.shape[-1]
    cparams = functools.partial(pltpu.CompilerParams, vmem_limit_bytes=VMEM_LIMIT)

    for layer in range(g_mix.shape[0]):
        w = w_in[layer]
        o = np.cumsum([0, D_CONV, D_CONV, D_CONV, D_ATTN, HEAD_DIM, HEAD_DIM,
                       N_IDX_HEADS * IDX_DIM, IDX_DIM, N_IDX_HEADS, D_MODEL, D_MODEL])
        w_cb, w_cc, w_cx, w_q, w_k, w_v, w_iq, w_ik, w_iw, w_ga, w_gb = [
            w[:, int(o[j]):int(o[j + 1])] for j in range(11)]
        w_nat = jnp.concatenate([w_cb, w_cc, w_cx, w_k, w_ik], axis=1).astype(BF16)
        w_pad = jnp.zeros((D_MODEL, V_EXTRA - N_IDX_HEADS), w.dtype)
        w_tr = jnp.concatenate([w_q * (HEAD_DIM ** -0.5 * LOG2E), w_iq, w_v, w_iw, w_pad],
                               axis=1).T.astype(BF16)
        w_g = jnp.concatenate([w_ga, w_gb], axis=1).astype(BF16)
        n_nat, n_tr = w_nat.shape[1], w_tr.shape[0]

        mix, k_n, ik_n, qT, iqT, vT, iwT = pl.pallas_call(
            _proj_kernel,
            grid=(bsz, seq // TM1),
            in_specs=[
                pl.BlockSpec((1, TM1, D_MODEL), lambda b, i: (b, i, 0)),
                _const_spec((1, D_MODEL)),
                _const_spec((D_MODEL, n_nat)),
                _const_spec((n_tr, D_MODEL)),
                _const_spec((CONV_K, D_CONV)),
            ],
            out_specs=[
                pl.BlockSpec((1, TM1, D_CONV), lambda b, i: (b, i, 0)),
                pl.BlockSpec((1, TM1, HEAD_DIM), lambda b, i: (b, i, 0)),
                pl.BlockSpec((1, TM1, IDX_DIM), lambda b, i: (b, i, 0)),
                pl.BlockSpec((1, D_ATTN, TM1), lambda b, i: (b, 0, i)),
                pl.BlockSpec((1, N_IDX_HEADS * IDX_DIM, TM1), lambda b, i: (b, 0, i)),
                pl.BlockSpec((1, TM1 // TK, HEAD_DIM + V_EXTRA, TK), lambda b, i: (b, i, 0, 0)),
                pl.BlockSpec((1, N_IDX_HEADS, TM1), lambda b, i: (b, 0, i)),
            ],
            out_shape=[
                jax.ShapeDtypeStruct((bsz, seq, D_CONV), BF16),
                jax.ShapeDtypeStruct((bsz, seq, HEAD_DIM), BF16),
                jax.ShapeDtypeStruct((bsz, seq, IDX_DIM), BF16),
                jax.ShapeDtypeStruct((bsz, D_ATTN, seq), BF16),
                jax.ShapeDtypeStruct((bsz, N_IDX_HEADS * IDX_DIM, seq), BF16),
                jax.ShapeDtypeStruct((bsz, nck, HEAD_DIM + V_EXTRA, TK), BF16),
                jax.ShapeDtypeStruct((bsz, N_IDX_HEADS, seq), F32),
            ],
            scratch_shapes=[pltpu.VMEM((TM1 + 8, D_CONV), F32)],
            compiler_params=cparams(dimension_semantics=("arbitrary", "arbitrary")),
            name="proj",
        )(x, g_mix[layer][None, :], w_nat, w_tr, conv_w[layer])

        attn = pl.pallas_call(
            functools.partial(_dsa_kernel, k_top=k_top),
            grid=(bsz, seq // TQ),
            in_specs=[
                pl.BlockSpec(memory_space=pltpu.SMEM),
                pl.BlockSpec((1, N_IDX_HEADS * IDX_DIM, TQ), lambda b, i: (b, 0, i)),
                pl.BlockSpec((1, N_IDX_HEADS, TQ), lambda b, i: (b, 0, i)),
                pl.BlockSpec((1, seq, IDX_DIM), lambda b, i: (b, 0, 0)),
                pl.BlockSpec((1, D_ATTN, TQ), lambda b, i: (b, 0, i)),
                pl.BlockSpec((1, seq, HEAD_DIM), lambda b, i: (b, 0, 0)),
                pl.BlockSpec((1, nck, HEAD_DIM + V_EXTRA, TK), lambda b, i: (b, 0, 0, 0)),
            ],
            out_specs=pl.BlockSpec((1, TQ, D_ATTN), lambda b, i: (b, i, 0)),
            out_shape=jax.ShapeDtypeStruct((bsz, seq, D_ATTN), BF16),
            scratch_shapes=[
                pltpu.VMEM((nck, TK, TQ), I32),
                pltpu.VMEM((nck, TK, TQ), BF16),
                pltpu.VMEM((nck, TK, TQ), BF16),
                pltpu.VMEM((nck + 2 * N_HEADS, TK, TQ), F32),
                pltpu.VMEM((2, N_HEADS, TK, TQ), F32),
                pltpu.VMEM((D_ATTN, TQ), F32),
                pltpu.VMEM((N_HEADS, TQ), F32),
                pltpu.VMEM((N_HEADS, TQ), F32),
                pltpu.VMEM((N_HEADS, TQ), F32),
                pltpu.VMEM((N_HEADS, TK, TQ), F32),
                pltpu.VMEM((N_HEADS, TK, TQ), F32),
            ],
            compiler_params=cparams(dimension_semantics=("arbitrary", "arbitrary")),
            name="dsa",
        )(rel_bias, iqT, iwT, ik_n, qT, k_n, vT)

        x = pl.pallas_call(
            functools.partial(_merge_kernel, final_norm=layer == g_mix.shape[0] - 1),
            grid=(bsz, seq // TM3),
            in_specs=[
                pl.BlockSpec((1, TM3, D_MODEL), lambda b, i: (b, i, 0)),
                pl.BlockSpec((1, TM3, D_CONV), lambda b, i: (b, i, 0)),
                pl.BlockSpec((1, TM3, D_ATTN), lambda b, i: (b, i, 0)),
                _const_spec((1, D_MODEL)),
                _const_spec((D_MODEL, 2 * D_MODEL)),
                _const_spec((1, 2 * D_MODEL)),
                _const_spec((D_CONV, D_MODEL)),
                _const_spec((D_ATTN, D_MODEL)),
                _const_spec((D_MODEL, D_MODEL)),
                _const_spec((1, D_MODEL)),
                _const_spec((D_MODEL, d_ff)),
                _const_spec((D_MODEL, d_ff)),
                _const_spec((d_ff, D_MODEL)),
                _const_spec((1, D_MODEL)),
            ],
            out_specs=pl.BlockSpec((1, TM3, D_MODEL), lambda b, i: (b, i, 0)),
            out_shape=jax.ShapeDtypeStruct((bsz, seq, D_MODEL), F32),
            compiler_params=cparams(dimension_semantics=("arbitrary", "arbitrary")),
            name="merge",
        )(x, mix, attn, g_mix[layer][None, :], w_g, b_gate[layer][None, :],
          w_branch_a[layer].astype(BF16), w_branch_b[layer].astype(BF16),
          w_out[layer].astype(BF16), g_ffn[layer][None, :],
          w_ffn_gate[layer].astype(BF16), w_ffn_up[layer].astype(BF16),
          w_ffn_down[layer].astype(BF16),
          g_final[None, :])
    return x
```

```python
import functools
import math

import numpy as np
import jax
import jax.numpy as jnp
from jax import lax
from jax.experimental import pallas as pl
from jax.experimental.pallas import tpu as pltpu

D_MODEL = 1024
D_CONV = 512
CONV_K = 3
N_HEADS = 8
HEAD_DIM = 64
D_ATTN = N_HEADS * HEAD_DIM
N_IDX_HEADS = 8
IDX_DIM = 64
TOPK_MAX = 256
N_BUCKETS = 32
MAX_EXACT = 16
MAX_DISTANCE = 128
EPS = 1e-6

F32 = jnp.float32
BF16 = jnp.bfloat16
I32 = jnp.int32

INT_MIN = -(2 ** 31)
NEG = -1e30
LOG2E = math.log2(math.e)

TM1 = 512
TQ = 256
TK = 256
TM3 = 256
V_EXTRA = 16
QUAD = 4
LANES = 128
STAGE1_LEAD = 2
VMEM_LIMIT = 56 * 1024 * 1024


def _bucket_steps():
    n = np.arange(2 * MAX_DISTANCE)
    large = MAX_EXACT + (np.log(np.maximum(n, 1).astype(np.float32) / MAX_EXACT)
                         / math.log(MAX_DISTANCE / MAX_EXACT)
                         * (N_BUCKETS - MAX_EXACT)).astype(np.int32)
    large = np.minimum(large, N_BUCKETS - 1)
    b = np.where(n < MAX_EXACT, n, large)
    assert np.all(b[MAX_DISTANCE:] == N_BUCKETS - 1) and np.all(np.diff(b) >= 0)
    steps = [(int(i), int(b[i])) for i in range(1, len(b)) if b[i] != b[i - 1]]
    return int(b[0]), steps


def _fold_rows(a, rows):
    parts = [a[i:i + rows, :] for i in range(0, a.shape[0], rows)]
    while len(parts) > 1:
        parts = [parts[i] + parts[i + 1] for i in range(0, len(parts) - 1, 2)] + (
            [parts[-1]] if len(parts) % 2 else [])
    return parts[0]


def _bit_transpose32(a):
    a = list(a)
    j, m = 16, 0x0000FFFF
    while j:
        k = 0
        while k < 32:
            t = (a[k] ^ (a[k + j] >> j)) & (m if m < 2 ** 31 else m - 2 ** 32)
            a[k] = a[k] ^ t
            a[k + j] = a[k + j] ^ (t << j)
            k = (k + j + 1) & ~j
        j >>= 1
        m ^= (m << j) & 0xFFFFFFFF
    return a


def _rms(x, g):
    return x * lax.rsqrt(jnp.mean(x * x, axis=-1, keepdims=True) + EPS) * g


def _proj_kernel(x_ref, g_ref, wn_ref, wt_ref, cw_ref,
                 mix_ref, k_ref, ik_ref, qT_ref, iqT_ref, vT_ref, iwT_ref, ubuf_ref):
    i = pl.program_id(1)
    x = x_ref[0]
    h = _rms(x, g_ref[...]).astype(BF16)

    def nat(c0, c1):
        return jnp.dot(h, wn_ref[:, c0:c1], preferred_element_type=F32)

    def tr(r0, r1):
        return lax.dot_general(wt_ref[r0:r1, :], h, (((1,), (1,)), ((), ())),
                               preferred_element_type=F32)

    u = nat(D_CONV, 2 * D_CONV) * nat(2 * D_CONV, 3 * D_CONV)

    @pl.when(i == 0)
    def _():
        ubuf_ref[0:8, :] = jnp.zeros((8, D_CONV), F32)

    ubuf_ref[8:8 + TM1, :] = u
    u1 = ubuf_ref[7:7 + TM1, :]
    u2 = ubuf_ref[6:6 + TM1, :]
    cw = cw_ref[...]
    y = cw[0:1, :] * u2 + cw[1:2, :] * u1 + cw[2:3, :] * u
    mix_ref[0] = (nat(0, D_CONV) * y).astype(BF16)
    ubuf_ref[0:8, :] = u[TM1 - 8:TM1, :]

    kk = nat(3 * D_CONV, 3 * D_CONV + HEAD_DIM + IDX_DIM)
    k_ref[0] = kk[:, 0:HEAD_DIM].astype(BF16)
    ik_ref[0] = kk[:, HEAD_DIM:HEAD_DIM + IDX_DIM].astype(BF16)

    qT_ref[0] = tr(0, D_ATTN).astype(BF16)
    iqT_ref[0] = tr(D_ATTN, 2 * D_ATTN).astype(BF16)
    vw = tr(2 * D_ATTN, 2 * D_ATTN + HEAD_DIM + V_EXTRA)
    vT = vw[0:HEAD_DIM, :].astype(BF16)
    ones_row = (lax.broadcasted_iota(I32, (V_EXTRA, TM1), 0) == 0).astype(F32).astype(BF16)
    vT = jnp.concatenate([vT, ones_row], axis=0)
    for j in range(TM1 // TK):
        vT_ref[0, j] = vT[:, j * TK:(j + 1) * TK]
    iwT_ref[0] = vw[HEAD_DIM:HEAD_DIM + N_IDX_HEADS, :] * (N_IDX_HEADS ** -0.5 * IDX_DIM ** -0.5)


def _dsa_kernel(rb_ref, iqT_ref, iwT_ref, ik_ref, qT_ref, k_ref, vT_ref,
                o_ref, key_ref, planes_ref, alive_ref, add_ref, bias_ref, oT_ref, m_ref, l_ref, al_ref,
                lta_ref, ltb_ref, *, k_top):
    b = pl.program_id(0)
    qi = pl.program_id(1)
    nchunks = qi + 1
    nck = key_ref.shape[0]
    n_keys = nck * TK

    @pl.when((b == 0) & (qi == 0))
    def _():
        b0, steps = _bucket_steps()
        row = lax.broadcasted_iota(I32, (TK, TQ), 0)
        col = lax.broadcasted_iota(I32, (TK, TQ), 1)
        for off in range(2):
            dist = col - row + off * TK
            for hh in range(N_HEADS):
                val = jnp.full((TK, TQ), rb_ref[b0, hh], F32)
                for n0, bk in steps:
                    val = jnp.where(dist >= n0, rb_ref[bk, hh], val)
                bias_ref[off, hh] = (val - rb_ref[N_BUCKETS - 1, hh]) * LOG2E

    w = iwT_ref[0]
    sub = TK // 4

    def score_step(c1, c2, dst_ref, src_ref, diagonal=False):
        if c1 is not None:
            ikc = ik_ref[0, pl.ds(pl.multiple_of(c1 * TK, TK), TK), :]
        if c2 is not None:
            row0 = lax.broadcasted_iota(I32, (sub, TQ), 0)
            lane = lax.broadcasted_iota(I32, (sub, TQ), 1)
        for g in range(4):
            if c1 is not None:
                for hh in range(2 * g, 2 * g + 2):
                    dst_ref[hh] = jnp.dot(ikc, iqT_ref[0, hh * IDX_DIM:(hh + 1) * IDX_DIM, :],
                                          preferred_element_type=F32)
            if c2 is not None:
                rows = slice(g * sub, (g + 1) * sub)
                acc = jnp.zeros((sub, TQ), F32)
                for hh in range(N_IDX_HEADS):
                    acc = acc + w[hh:hh + 1, :] * jnp.maximum(src_ref[hh, rows, :], 0.0)
                zero = acc == 0.0
                bits = pltpu.bitcast(jnp.where(zero, 0.0, acc), I32)
                key = bits ^ ((bits >> 31) & 0x7FFFFFFF)
                first = c2 * TK + g * sub
                key = jnp.where(zero, (n_keys - 1 - first) - row0, key)
                if diagonal:
                    key = jnp.where(row0 + (g * sub) <= lane, key, INT_MIN)
                key_ref[c2, rows, :] = key
        if c2 is not None:
            for l0 in range(0, TQ, LANES):
                planes = _bit_transpose32([key_ref[c2, 8 * v:8 * v + 8, l0:l0 + LANES]
                                           for v in range(TK // 8)])
                for bit in range(32):
                    plane = planes[31 - bit]
                    planes_ref[c2, bit, :, l0:l0 + LANES] = ~plane if bit == 31 else plane

    def run_pipeline(step, n_steps):
        step(0, None, lta_ref, None)

        def two_steps(j, carry):
            c = 2 * j
            step(c + 1, c, ltb_ref, lta_ref)
            step(c + 2, c + 1, lta_ref, ltb_ref)
            return carry

        lax.fori_loop(0, n_steps // 2, two_steps, 0)

        @pl.when(n_steps % 2 == 1)
        def _():
            step(n_steps, n_steps - 1, ltb_ref, lta_ref)
            step(None, n_steps, None, ltb_ref, diagonal=True)

        @pl.when(n_steps % 2 == 0)
        def _():
            step(None, n_steps, None, lta_ref, diagonal=True)

    run_pipeline(score_step, nchunks - 1)

    def count(pred):
        def body(c, acc):
            m = pred(key_ref[c], c).astype(I32)
            return acc + jnp.sum(m.reshape(TK // 8, 8, TQ), axis=0)
        acc = lax.fori_loop(0, nchunks, body, jnp.zeros((8, TQ), I32))
        return jnp.sum(acc, axis=0, keepdims=True)

    n_quads = (nchunks + QUAD - 1) // QUAD

    def init_alive(c, carry):
        alive_ref[c] = jnp.broadcast_to(jnp.where(c < nchunks, -1, 0).astype(I32), (8, TQ))
        return carry

    lax.fori_loop(0, n_quads * QUAD, init_alive, 0)

    def select_pass(bit, state, first):
        k_rem, alive_n, thr_u, flip = state

        def quad(j, cnt):
            for i in range(QUAD):
                c = QUAD * j + i
                a = alive_ref[c]
                if not first:
                    a = a & (planes_ref[c, bit + 1] ^ flip)
                    alive_ref[c] = a
                cnt = cnt + lax.population_count(a & planes_ref[c, bit])
            return cnt

        ones = jnp.sum(lax.fori_loop(0, n_quads, quad, jnp.zeros((8, TQ), I32)),
                       axis=0, keepdims=True)
        take = ones >= k_rem
        return (jnp.where(take, k_rem, k_rem - ones), jnp.where(take, ones, alive_n - ones),
                thr_u | jnp.where(take, jnp.left_shift(jnp.int32(1), bit), 0),
                jnp.broadcast_to(jnp.where(take, 0, -1), (8, TQ)))

    state = (jnp.full((1, TQ), k_top, I32), jnp.full((1, TQ), nchunks * TK, I32),
             jnp.zeros((1, TQ), I32), jnp.zeros((8, TQ), I32))
    state = select_pass(31, state, first=True)
    k_rem, alive_n, thr_u, _ = lax.fori_loop(
        1, 32, lambda p, st: select_pass(31 - p, st, first=False), state)
    thr = thr_u ^ INT_MIN
    any_excess = jnp.max(((alive_n > k_rem) & (thr > INT_MIN)).astype(I32)) > 0
    thr = jnp.maximum(thr, INT_MIN + 1)

    @pl.when(jnp.logical_not(any_excess))
    def _():
        def mask_chunk(c, carry):
            add_ref[c] = jnp.where(key_ref[c] >= thr, 0.0, NEG).astype(F32)
            return carry
        lax.fori_loop(0, nchunks, mask_chunk, 0)

    @pl.when(any_excess)
    def _():
        room = k_top - count(lambda kc, c: kc > thr)

        def key_index(c):
            return lax.broadcasted_iota(I32, (TK, TQ), 0) + c * TK

        def step(it, lim):
            cand = lim + jnp.left_shift(jnp.int32(1), 12 - it)
            cnt = count(lambda kc, c: (kc == thr) & (key_index(c) < cand))
            return jnp.where(cnt <= room, cand, lim)

        tie_lim = lax.fori_loop(0, 13, step, jnp.zeros((1, TQ), I32))

        def mask_chunk(c, carry):
            kc = key_ref[c]
            sel = (kc > thr) | ((kc == thr) & (key_index(c) < tie_lim))
            add_ref[c] = jnp.where(sel, 0.0, NEG).astype(F32)
            return carry
        lax.fori_loop(0, nchunks, mask_chunk, 0)

    for off in range(2):
        @pl.when(qi >= off)
        def _():
            nm = add_ref[qi - off]
            for hh in range(N_HEADS):
                add_ref[nck + off * N_HEADS + hh] = nm + bias_ref[off, hh]

    def pipe_step(c1, c2, dst_ref, src_ref, diagonal=False):
        m_cur = m_ref[...]
        m_rows, l_rows = [], []
        if c1 is not None:
            kc = k_ref[0, pl.ds(pl.multiple_of(c1 * TK, TK), TK), :]
            near = c1 >= qi - 1
        if c2 is not None:
            vc = vT_ref[0, c2]
            alpha = al_ref[...]
        for i in range(N_HEADS + STAGE1_LEAD):
            if c1 is not None and i < N_HEADS:
                hh = i
                rows = slice(hh * HEAD_DIM, (hh + 1) * HEAD_DIM)
                tile = jnp.where(near, nck + (qi - c1) * N_HEADS + hh, c1)
                lt = jnp.dot(kc, qT_ref[0, rows, :], preferred_element_type=F32) + add_ref[tile]
                dst_ref[hh] = lt
                m_rows.append(jnp.max(lt, axis=0, keepdims=True))
                if hh == N_HEADS - 1:
                    m_new = jnp.maximum(m_cur, jnp.concatenate(m_rows, axis=0))
                    al_ref[...] = jnp.exp2(m_cur - m_new)
                    m_ref[...] = m_new
            if c2 is not None and i >= STAGE1_LEAD:
                hh = i - STAGE1_LEAD
                rows = slice(hh * HEAD_DIM, (hh + 1) * HEAD_DIM)
                p = jnp.exp2(src_ref[hh] - m_cur[hh:hh + 1, :]).astype(BF16)
                pv = jnp.dot(vc, p, preferred_element_type=F32)
                oT_ref[rows, :] = alpha[hh:hh + 1, :] * oT_ref[rows, :] + pv[0:HEAD_DIM, :]
                l_rows.append(pv[HEAD_DIM:HEAD_DIM + 1, :])
        if c2 is not None:
            l_ref[...] = alpha * l_ref[...] + jnp.concatenate(l_rows, axis=0)

    m_ref[...] = jnp.full((N_HEADS, TQ), NEG, F32)
    l_ref[...] = jnp.zeros((N_HEADS, TQ), F32)
    oT_ref[...] = jnp.zeros((D_ATTN, TQ), F32)
    run_pipeline(pipe_step, nchunks - 1)
    for hh in range(N_HEADS):
        rows = slice(hh * HEAD_DIM, (hh + 1) * HEAD_DIM)
        oT_ref[rows, :] = oT_ref[rows, :] / l_ref[hh:hh + 1, :]

    o_ref[0] = oT_ref[...].T.astype(BF16)


def _merge_kernel(x_ref, mix_ref, att_ref, gmix_ref, wg_ref, bg_ref, wa_ref, wb_ref, wo_ref,
                  gffn_ref, wfg_ref, wfu_ref, wfd_ref, gfin_ref, o_ref, *, final_norm):
    x = x_ref[0]
    h = _rms(x, gmix_ref[...]).astype(BF16)
    gates = jax.nn.sigmoid(jnp.dot(h, wg_ref[...], preferred_element_type=F32) + bg_ref[...])
    y_a = jnp.dot(mix_ref[0], wa_ref[...], preferred_element_type=F32)
    y_b = jnp.dot(att_ref[0], wb_ref[...], preferred_element_type=F32)
    merged = gates[:, :D_MODEL] * y_a + gates[:, D_MODEL:] * y_b
    x1 = x + jnp.dot(merged.astype(BF16), wo_ref[...], preferred_element_type=F32)
    h2 = _rms(x1, gffn_ref[...]).astype(BF16)
    fg = jnp.dot(h2, wfg_ref[...], preferred_element_type=F32)
    fu = jnp.dot(h2, wfu_ref[...], preferred_element_type=F32)
    a = (jax.nn.silu(fg) * fu).astype(BF16)
    x2 = x1 + jnp.dot(a, wfd_ref[...], preferred_element_type=F32)
    o_ref[0] = _rms(x2, gfin_ref[...]) if final_norm else x2


def _const_spec(shape):
    nd = len(shape)
    return pl.BlockSpec(shape, lambda *_: (0,) * nd, pipeline_mode=pl.Buffered(1))


def kernel(x, g_mix, w_in, b_gate, conv_w, w_branch_a, w_branch_b, w_out, rel_bias, g_ffn,
           w_ffn_gate, w_ffn_up, w_ffn_down, g_final):
    bsz, seq, d = x.shape
    assert d == D_MODEL and seq % TM1 == 0 and seq % TQ == 0 and seq % TM3 == 0 and TQ == TK
    k_top = min(TOPK_MAX, seq // 4)
    assert TK == 32 * 8 and (seq // TK) % QUAD == 0
    assert seq < 2 ** 23
    nck = seq // TK
    d_ff = w_ffn_gate.shape[-1]
    cparams = functools.partial(pltpu.CompilerParams, vmem_limit_bytes=VMEM_LIMIT)

    for layer in range(g_mix.shape[0]):
        w = w_in[layer]
        o = np.cumsum([0, D_CONV, D_CONV, D_CONV, D_ATTN, HEAD_DIM, HEAD_DIM,
                       N_IDX_HEADS * IDX_DIM, IDX_DIM, N_IDX_HEADS, D_MODEL, D_MODEL])
        w_cb, w_cc, w_cx, w_q, w_k, w_v, w_iq, w_ik, w_iw, w_ga, w_gb = [
            w[:, int(o[j]):int(o[j + 1])] for j in range(11)]
        w_nat = jnp.concatenate([w_cb, w_cc, w_cx, w_k, w_ik], axis=1).astype(BF16)
        w_pad = jnp.zeros((D_MODEL, V_EXTRA - N_IDX_HEADS), w.dtype)
        w_tr = jnp.concatenate([w_q * (HEAD_DIM ** -0.5 * LOG2E), w_iq, w_v, w_iw, w_pad],
                               axis=1).T.astype(BF16)
        w_g = jnp.concatenate([w_ga, w_gb], axis=1).astype(BF16)
        n_nat, n_tr = w_nat.shape[1], w_tr.shape[0]

        mix, k_n, ik_n, qT, iqT, vT, iwT = pl.pallas_call(
            _proj_kernel,
            grid=(bsz, seq // TM1),
            in_specs=[
                pl.BlockSpec((1, TM1, D_MODEL), lambda b, i: (b, i, 0)),
                _const_spec((1, D_MODEL)),
                _const_spec((D_MODEL, n_nat)),
                _const_spec((n_tr, D_MODEL)),
                _const_spec((CONV_K, D_CONV)),
            ],
            out_specs=[
                pl.BlockSpec((1, TM1, D_CONV), lambda b, i: (b, i, 0)),
                pl.BlockSpec((1, TM1, HEAD_DIM), lambda b, i: (b, i, 0)),
                pl.BlockSpec((1, TM1, IDX_DIM), lambda b, i: (b, i, 0)),
                pl.BlockSpec((1, D_ATTN, TM1), lambda b, i: (b, 0, i)),
                pl.BlockSpec((1, N_IDX_HEADS * IDX_DIM, TM1), lambda b, i: (b, 0, i)),
                pl.BlockSpec((1, TM1 // TK, HEAD_DIM + V_EXTRA, TK), lambda b, i: (b, i, 0, 0)),
                pl.BlockSpec((1, N_IDX_HEADS, TM1), lambda b, i: (b, 0, i)),
            ],
            out_shape=[
                jax.ShapeDtypeStruct((bsz, seq, D_CONV), BF16),
                jax.ShapeDtypeStruct((bsz, seq, HEAD_DIM), BF16),
                jax.ShapeDtypeStruct((bsz, seq, IDX_DIM), BF16),
                jax.ShapeDtypeStruct((bsz, D_ATTN, seq), BF16),
                jax.ShapeDtypeStruct((bsz, N_IDX_HEADS * IDX_DIM, seq), BF16),
                jax.ShapeDtypeStruct((bsz, nck, HEAD_DIM + V_EXTRA, TK), BF16),
                jax.ShapeDtypeStruct((bsz, N_IDX_HEADS, seq), F32),
            ],
            scratch_shapes=[pltpu.VMEM((TM1 + 8, D_CONV), F32)],
            compiler_params=cparams(dimension_semantics=("arbitrary", "arbitrary")),
            name="proj",
        )(x, g_mix[layer][None, :], w_nat, w_tr, conv_w[layer])

        attn = pl.pallas_call(
            functools.partial(_dsa_kernel, k_top=k_top),
            grid=(bsz, seq // TQ),
            in_specs=[
                pl.BlockSpec(memory_space=pltpu.SMEM),
                pl.BlockSpec((1, N_IDX_HEADS * IDX_DIM, TQ), lambda b, i: (b, 0, i)),
                pl.BlockSpec((1, N_IDX_HEADS, TQ), lambda b, i: (b, 0, i)),
                pl.BlockSpec((1, seq, IDX_DIM), lambda b, i: (b, 0, 0)),
                pl.BlockSpec((1, D_ATTN, TQ), lambda b, i: (b, 0, i)),
                pl.BlockSpec((1, seq, HEAD_DIM), lambda b, i: (b, 0, 0)),
                pl.BlockSpec((1, nck, HEAD_DIM + V_EXTRA, TK), lambda b, i: (b, 0, 0, 0)),
            ],
            out_specs=pl.BlockSpec((1, TQ, D_ATTN), lambda b, i: (b, i, 0)),
            out_shape=jax.ShapeDtypeStruct((bsz, seq, D_ATTN), BF16),
            scratch_shapes=[
                pltpu.VMEM((nck, TK, TQ), I32),
                pltpu.VMEM((nck, 32, 8, TQ), I32),
                pltpu.VMEM((nck, 8, TQ), I32),
                pltpu.VMEM((nck + 2 * N_HEADS, TK, TQ), F32),
                pltpu.VMEM((2, N_HEADS, TK, TQ), F32),
                pltpu.VMEM((D_ATTN, TQ), F32),
                pltpu.VMEM((N_HEADS, TQ), F32),
                pltpu.VMEM((N_HEADS, TQ), F32),
                pltpu.VMEM((N_HEADS, TQ), F32),
                pltpu.VMEM((N_HEADS, TK, TQ), F32),
                pltpu.VMEM((N_HEADS, TK, TQ), F32),
            ],
            compiler_params=cparams(dimension_semantics=("arbitrary", "arbitrary")),
            name="dsa",
        )(rel_bias, iqT, iwT, ik_n, qT, k_n, vT)

        x = pl.pallas_call(
            functools.partial(_merge_kernel, final_norm=layer == g_mix.shape[0] - 1),
            grid=(bsz, seq // TM3),
            in_specs=[
                pl.BlockSpec((1, TM3, D_MODEL), lambda b, i: (b, i, 0)),
                pl.BlockSpec((1, TM3, D_CONV), lambda b, i: (b, i, 0)),
                pl.BlockSpec((1, TM3, D_ATTN), lambda b, i: (b, i, 0)),
                _const_spec((1, D_MODEL)),
                _const_spec((D_MODEL, 2 * D_MODEL)),
                _const_spec((1, 2 * D_MODEL)),
                _const_spec((D_CONV, D_MODEL)),
                _const_spec((D_ATTN, D_MODEL)),
                _const_spec((D_MODEL, D_MODEL)),
                _const_spec((1, D_MODEL)),
                _const_spec((D_MODEL, d_ff)),
                _const_spec((D_MODEL, d_ff)),
                _const_spec((d_ff, D_MODEL)),
                _const_spec((1, D_MODEL)),
            ],
            out_specs=pl.BlockSpec((1, TM3, D_MODEL), lambda b, i: (b, i, 0)),
            out_shape=jax.ShapeDtypeStruct((bsz, seq, D_MODEL), F32),
            compiler_params=cparams(dimension_semantics=("arbitrary", "arbitrary")),
            name="merge",
        )(x, mix, attn, g_mix[layer][None, :], w_g, b_gate[layer][None, :],
          w_branch_a[layer].astype(BF16), w_branch_b[layer].astype(BF16),
          w_out[layer].astype(BF16), g_ffn[layer][None, :],
          w_ffn_gate[layer].astype(BF16), w_ffn_up[layer].astype(BF16),
          w_ffn_down[layer].astype(BF16),
          g_final[None, :])
    return x
```

```python
import functools
import math

import numpy as np
import jax
import jax.numpy as jnp
from jax import lax
from jax.experimental import pallas as pl
from jax.experimental.pallas import tpu as pltpu

D_MODEL = 1024
D_CONV = 512
CONV_K = 3
N_HEADS = 8
HEAD_DIM = 64
D_ATTN = N_HEADS * HEAD_DIM
N_IDX_HEADS = 8
IDX_DIM = 64
TOPK_MAX = 256
N_BUCKETS = 32
MAX_EXACT = 16
MAX_DISTANCE = 128
EPS = 1e-6

F32 = jnp.float32
BF16 = jnp.bfloat16
I32 = jnp.int32

INT_MIN = -(2 ** 31)
NEG = -1e30
LOG2E = math.log2(math.e)

TM1 = 512
TQ = 256
TK = 256
TM3 = 256
V_EXTRA = 16
QUAD = 4
LANES = 128
STAGE1_LEAD = 2
VMEM_LIMIT = 56 * 1024 * 1024


def _bucket_steps():
    n = np.arange(2 * MAX_DISTANCE)
    large = MAX_EXACT + (np.log(np.maximum(n, 1).astype(np.float32) / MAX_EXACT)
                         / math.log(MAX_DISTANCE / MAX_EXACT)
                         * (N_BUCKETS - MAX_EXACT)).astype(np.int32)
    large = np.minimum(large, N_BUCKETS - 1)
    b = np.where(n < MAX_EXACT, n, large)
    assert np.all(b[MAX_DISTANCE:] == N_BUCKETS - 1) and np.all(np.diff(b) >= 0)
    steps = [(int(i), int(b[i])) for i in range(1, len(b)) if b[i] != b[i - 1]]
    return int(b[0]), steps


def _fold_rows(a, rows):
    parts = [a[i:i + rows, :] for i in range(0, a.shape[0], rows)]
    while len(parts) > 1:
        parts = [parts[i] + parts[i + 1] for i in range(0, len(parts) - 1, 2)] + (
            [parts[-1]] if len(parts) % 2 else [])
    return parts[0]


def _bit_transpose32(a):
    a = list(a)
    j, m = 16, 0x0000FFFF
    while j:
        k = 0
        while k < 32:
            t = (a[k] ^ (a[k + j] >> j)) & (m if m < 2 ** 31 else m - 2 ** 32)
            a[k] = a[k] ^ t
            a[k + j] = a[k + j] ^ (t << j)
            k = (k + j + 1) & ~j
        j >>= 1
        m ^= (m << j) & 0xFFFFFFFF
    return a


def _rms(x, g):
    return x * lax.rsqrt(jnp.mean(x * x, axis=-1, keepdims=True) + EPS) * g


def _proj_kernel(x_ref, g_ref, wn_ref, wt_ref, cw_ref,
                 mix_ref, k_ref, ik_ref, qT_ref, iqT_ref, vT_ref, iwT_ref, ubuf_ref):
    i = pl.program_id(1)
    x = x_ref[0]
    h = _rms(x, g_ref[...]).astype(BF16)

    def nat(c0, c1):
        return jnp.dot(h, wn_ref[:, c0:c1], preferred_element_type=F32)

    def tr(r0, r1):
        return lax.dot_general(wt_ref[r0:r1, :], h, (((1,), (1,)), ((), ())),
                               preferred_element_type=F32)

    u = nat(D_CONV, 2 * D_CONV) * nat(2 * D_CONV, 3 * D_CONV)

    @pl.when(i == 0)
    def _():
        ubuf_ref[0:8, :] = jnp.zeros((8, D_CONV), F32)

    ubuf_ref[8:8 + TM1, :] = u
    u1 = ubuf_ref[7:7 + TM1, :]
    u2 = ubuf_ref[6:6 + TM1, :]
    cw = cw_ref[...]
    y = cw[0:1, :] * u2 + cw[1:2, :] * u1 + cw[2:3, :] * u
    mix_ref[0] = (nat(0, D_CONV) * y).astype(BF16)
    ubuf_ref[0:8, :] = u[TM1 - 8:TM1, :]

    kk = nat(3 * D_CONV, 3 * D_CONV + HEAD_DIM + IDX_DIM)
    k_ref[0] = kk[:, 0:HEAD_DIM].astype(BF16)
    ik_ref[0] = kk[:, HEAD_DIM:HEAD_DIM + IDX_DIM].astype(BF16)

    qT_ref[0] = tr(0, D_ATTN).astype(BF16)
    iqT_ref[0] = tr(D_ATTN, 2 * D_ATTN).astype(BF16)
    vw = tr(2 * D_ATTN, 2 * D_ATTN + HEAD_DIM + V_EXTRA)
    vT = vw[0:HEAD_DIM, :].astype(BF16)
    ones_row = (lax.broadcasted_iota(I32, (V_EXTRA, TM1), 0) == 0).astype(F32).astype(BF16)
    vT = jnp.concatenate([vT, ones_row], axis=0)
    for j in range(TM1 // TK):
        vT_ref[0, j] = vT[:, j * TK:(j + 1) * TK]
    iwT_ref[0] = vw[HEAD_DIM:HEAD_DIM + N_IDX_HEADS, :] * (N_IDX_HEADS ** -0.5 * IDX_DIM ** -0.5)


def _dsa_kernel(rb_ref, iqT_ref, iwT_ref, ik_ref, qT_ref, k_ref, vT_ref,
                o_ref, key_ref, planes_ref, alive_ref, add_ref, bias_ref, oT_ref, m_ref, l_ref, al_ref,
                lta_ref, ltb_ref, *, k_top):
    b = pl.program_id(0)
    qi = pl.program_id(1)
    nchunks = qi + 1
    nck = key_ref.shape[0]
    n_keys = nck * TK

    @pl.when((b == 0) & (qi == 0))
    def _():
        b0, steps = _bucket_steps()
        row = lax.broadcasted_iota(I32, (TK, TQ), 0)
        col = lax.broadcasted_iota(I32, (TK, TQ), 1)
        for off in range(2):
            dist = col - row + off * TK
            for hh in range(N_HEADS):
                val = jnp.full((TK, TQ), rb_ref[b0, hh], F32)
                for n0, bk in steps:
                    val = jnp.where(dist >= n0, rb_ref[bk, hh], val)
                bias_ref[off, hh] = (val - rb_ref[N_BUCKETS - 1, hh]) * LOG2E

    w = iwT_ref[0]
    sub = TK // 4

    def score_step(c1, c2, dst_ref, src_ref, diagonal=False):
        if c1 is not None:
            ikc = ik_ref[0, pl.ds(pl.multiple_of(c1 * TK, TK), TK), :]
        if c2 is not None:
            row0 = lax.broadcasted_iota(I32, (sub, TQ), 0)
            lane = lax.broadcasted_iota(I32, (sub, TQ), 1)
        for g in range(4):
            if c1 is not None:
                for hh in range(2 * g, 2 * g + 2):
                    dst_ref[hh] = jnp.dot(ikc, iqT_ref[0, hh * IDX_DIM:(hh + 1) * IDX_DIM, :],
                                          preferred_element_type=F32)
            if c2 is not None:
                rows = slice(g * sub, (g + 1) * sub)
                acc = jnp.zeros((sub, TQ), F32)
                for hh in range(N_IDX_HEADS):
                    acc = acc + w[hh:hh + 1, :] * jnp.maximum(src_ref[hh, rows, :], 0.0)
                zero = acc == 0.0
                bits = pltpu.bitcast(jnp.where(zero, 0.0, acc), I32)
                key = bits ^ ((bits >> 31) & 0x7FFFFFFF)
                first = c2 * TK + g * sub
                key = jnp.where(zero, (n_keys - 1 - first) - row0, key)
                if diagonal:
                    key = jnp.where(row0 + (g * sub) <= lane, key, INT_MIN)
                key_ref[c2, rows, :] = key
        if c2 is not None:
            for l0 in range(0, TQ, LANES):
                planes = _bit_transpose32([key_ref[c2, 8 * v:8 * v + 8, l0:l0 + LANES]
                                           for v in range(TK // 8)])
                for bit in range(32):
                    plane = planes[31 - bit]
                    planes_ref[c2, bit, :, l0:l0 + LANES] = ~plane if bit == 31 else plane

    def run_pipeline(step, n_steps):
        step(0, None, lta_ref, None)

        def two_steps(j, carry):
            c = 2 * j
            step(c + 1, c, ltb_ref, lta_ref)
            step(c + 2, c + 1, lta_ref, ltb_ref)
            return carry

        lax.fori_loop(0, n_steps // 2, two_steps, 0)

        @pl.when(n_steps % 2 == 1)
        def _():
            step(n_steps, n_steps - 1, ltb_ref, lta_ref)
            step(None, n_steps, None, ltb_ref, diagonal=True)

        @pl.when(n_steps % 2 == 0)
        def _():
            step(None, n_steps, None, lta_ref, diagonal=True)

    run_pipeline(score_step, nchunks - 1)

    def count(pred):
        def body(c, acc):
            m = pred(key_ref[c], c).astype(I32)
            return acc + jnp.sum(m.reshape(TK // 8, 8, TQ), axis=0)
        acc = lax.fori_loop(0, nchunks, body, jnp.zeros((8, TQ), I32))
        return jnp.sum(acc, axis=0, keepdims=True)

    n_quads = (nchunks + QUAD - 1) // QUAD

    def init_alive(c, carry):
        alive_ref[c] = jnp.broadcast_to(jnp.where(c < nchunks, -1, 0).astype(I32), (8, TQ))
        return carry

    lax.fori_loop(0, n_quads * QUAD, init_alive, 0)

    def select_pass(bit, state, first):
        k_rem, alive_n, thr_u, flip = state

        def quad(j, cnt):
            for i in range(QUAD):
                c = QUAD * j + i
                cp = jnp.minimum(c, nchunks - 1)
                a = alive_ref[c]
                if not first:
                    a = a & (planes_ref[cp, bit + 1] ^ flip)
                    alive_ref[c] = a
                cnt = cnt + lax.population_count(a & planes_ref[cp, bit])
            return cnt

        ones = jnp.sum(lax.fori_loop(0, n_quads, quad, jnp.zeros((8, TQ), I32)),
                       axis=0, keepdims=True)
        take = ones >= k_rem
        return (jnp.where(take, k_rem, k_rem - ones), jnp.where(take, ones, alive_n - ones),
                thr_u | jnp.where(take, jnp.left_shift(jnp.int32(1), bit), 0),
                jnp.broadcast_to(jnp.where(take, 0, -1), (8, TQ)))

    state = (jnp.full((1, TQ), k_top, I32), jnp.full((1, TQ), nchunks * TK, I32),
             jnp.zeros((1, TQ), I32), jnp.zeros((8, TQ), I32))
    state = select_pass(31, state, first=True)
    k_rem, alive_n, thr_u, _ = lax.fori_loop(
        1, 32, lambda p, st: select_pass(31 - p, st, first=False), state)
    thr = thr_u ^ INT_MIN
    any_excess = jnp.max(((alive_n > k_rem) & (thr > INT_MIN)).astype(I32)) > 0
    thr = jnp.maximum(thr, INT_MIN + 1)

    @pl.when(jnp.logical_not(any_excess))
    def _():
        def mask_chunk(c, carry):
            add_ref[c] = jnp.where(key_ref[c] >= thr, 0.0, NEG).astype(F32)
            return carry
        lax.fori_loop(0, nchunks, mask_chunk, 0)

    @pl.when(any_excess)
    def _():
        room = k_top - count(lambda kc, c: kc > thr)

        def key_index(c):
            return lax.broadcasted_iota(I32, (TK, TQ), 0) + c * TK

        def step(it, lim):
            cand = lim + jnp.left_shift(jnp.int32(1), 12 - it)
            cnt = count(lambda kc, c: (kc == thr) & (key_index(c) < cand))
            return jnp.where(cnt <= room, cand, lim)

        tie_lim = lax.fori_loop(0, 13, step, jnp.zeros((1, TQ), I32))

        def mask_chunk(c, carry):
            kc = key_ref[c]
            sel = (kc > thr) | ((kc == thr) & (key_index(c) < tie_lim))
            add_ref[c] = jnp.where(sel, 0.0, NEG).astype(F32)
            return carry
        lax.fori_loop(0, nchunks, mask_chunk, 0)

    for off in range(2):
        @pl.when(qi >= off)
        def _():
            nm = add_ref[qi - off]
            for hh in range(N_HEADS):
                add_ref[nck + off * N_HEADS + hh] = nm + bias_ref[off, hh]

    def pipe_step(c1, c2, dst_ref, src_ref, diagonal=False):
        m_cur = m_ref[...]
        m_rows, l_rows = [], []
        if c1 is not None:
            kc = k_ref[0, pl.ds(pl.multiple_of(c1 * TK, TK), TK), :]
            near = c1 >= qi - 1
        if c2 is not None:
            vc = vT_ref[0, c2]
            alpha = al_ref[...]
        for i in range(N_HEADS + STAGE1_LEAD):
            if c1 is not None and i < N_HEADS:
                hh = i
                rows = slice(hh * HEAD_DIM, (hh + 1) * HEAD_DIM)
                tile = jnp.where(near, nck + (qi - c1) * N_HEADS + hh, c1)
                lt = jnp.dot(kc, qT_ref[0, rows, :], preferred_element_type=F32) + add_ref[tile]
                dst_ref[hh] = lt
                m_rows.append(jnp.max(lt, axis=0, keepdims=True))
                if hh == N_HEADS - 1:
                    m_new = jnp.maximum(m_cur, jnp.concatenate(m_rows, axis=0))
                    al_ref[...] = jnp.exp2(m_cur - m_new)
                    m_ref[...] = m_new
            if c2 is not None and i >= STAGE1_LEAD:
                hh = i - STAGE1_LEAD
                rows = slice(hh * HEAD_DIM, (hh + 1) * HEAD_DIM)
                p = jnp.exp2(src_ref[hh] - m_cur[hh:hh + 1, :]).astype(BF16)
                pv = jnp.dot(vc, p, preferred_element_type=F32)
                oT_ref[rows, :] = alpha[hh:hh + 1, :] * oT_ref[rows, :] + pv[0:HEAD_DIM, :]
                l_rows.append(pv[HEAD_DIM:HEAD_DIM + 1, :])
        if c2 is not None:
            l_ref[...] = alpha * l_ref[...] + jnp.concatenate(l_rows, axis=0)

    m_ref[...] = jnp.full((N_HEADS, TQ), NEG, F32)
    l_ref[...] = jnp.zeros((N_HEADS, TQ), F32)
    oT_ref[...] = jnp.zeros((D_ATTN, TQ), F32)
    run_pipeline(pipe_step, nchunks - 1)
    for hh in range(N_HEADS):
        rows = slice(hh * HEAD_DIM, (hh + 1) * HEAD_DIM)
        oT_ref[rows, :] = oT_ref[rows, :] / l_ref[hh:hh + 1, :]

    o_ref[0] = oT_ref[...].T.astype(BF16)


def _merge_kernel(x_ref, mix_ref, att_ref, gmix_ref, wg_ref, bg_ref, wa_ref, wb_ref, wo_ref,
                  gffn_ref, wfg_ref, wfu_ref, wfd_ref, gfin_ref, o_ref, *, final_norm):
    x = x_ref[0]
    h = _rms(x, gmix_ref[...]).astype(BF16)
    gates = jax.nn.sigmoid(jnp.dot(h, wg_ref[...], preferred_element_type=F32) + bg_ref[...])
    y_a = jnp.dot(mix_ref[0], wa_ref[...], preferred_element_type=F32)
    y_b = jnp.dot(att_ref[0], wb_ref[...], preferred_element_type=F32)
    merged = gates[:, :D_MODEL] * y_a + gates[:, D_MODEL:] * y_b
    x1 = x + jnp.dot(merged.astype(BF16), wo_ref[...], preferred_element_type=F32)
    h2 = _rms(x1, gffn_ref[...]).astype(BF16)
    fg = jnp.dot(h2, wfg_ref[...], preferred_element_type=F32)
    fu = jnp.dot(h2, wfu_ref[...], preferred_element_type=F32)
    a = (jax.nn.silu(fg) * fu).astype(BF16)
    x2 = x1 + jnp.dot(a, wfd_ref[...], preferred_element_type=F32)
    o_ref[0] = _rms(x2, gfin_ref[...]) if final_norm else x2


def _const_spec(shape):
    nd = len(shape)
    return pl.BlockSpec(shape, lambda *_: (0,) * nd, pipeline_mode=pl.Buffered(1))


def kernel(x, g_mix, w_in, b_gate, conv_w, w_branch_a, w_branch_b, w_out, rel_bias, g_ffn,
           w_ffn_gate, w_ffn_up, w_ffn_down, g_final):
    bsz, seq, d = x.shape
    assert d == D_MODEL and seq % TM1 == 0 and seq % TQ == 0 and seq % TM3 == 0 and TQ == TK
    k_top = min(TOPK_MAX, seq // 4)
    assert TK == 32 * 8 and (seq // TK) % QUAD == 0
    assert seq < 2 ** 23
    nck = seq // TK
    d_ff = w_ffn_gate.shape[-1]
    cparams = functools.partial(pltpu.CompilerParams, vmem_limit_bytes=VMEM_LIMIT)

    for layer in range(g_mix.shape[0]):
        w = w_in[layer]
        o = np.cumsum([0, D_CONV, D_CONV, D_CONV, D_ATTN, HEAD_DIM, HEAD_DIM,
                       N_IDX_HEADS * IDX_DIM, IDX_DIM, N_IDX_HEADS, D_MODEL, D_MODEL])
        w_cb, w_cc, w_cx, w_q, w_k, w_v, w_iq, w_ik, w_iw, w_ga, w_gb = [
            w[:, int(o[j]):int(o[j + 1])] for j in range(11)]
        w_nat = jnp.concatenate([w_cb, w_cc, w_cx, w_k, w_ik], axis=1).astype(BF16)
        w_pad = jnp.zeros((D_MODEL, V_EXTRA - N_IDX_HEADS), w.dtype)
        w_tr = jnp.concatenate([w_q * (HEAD_DIM ** -0.5 * LOG2E), w_iq, w_v, w_iw, w_pad],
                               axis=1).T.astype(BF16)
        w_g = jnp.concatenate([w_ga, w_gb], axis=1).astype(BF16)
        n_nat, n_tr = w_nat.shape[1], w_tr.shape[0]

        mix, k_n, ik_n, qT, iqT, vT, iwT = pl.pallas_call(
            _proj_kernel,
            grid=(bsz, seq // TM1),
            in_specs=[
                pl.BlockSpec((1, TM1, D_MODEL), lambda b, i: (b, i, 0)),
                _const_spec((1, D_MODEL)),
                _const_spec((D_MODEL, n_nat)),
                _const_spec((n_tr, D_MODEL)),
                _const_spec((CONV_K, D_CONV)),
            ],
            out_specs=[
                pl.BlockSpec((1, TM1, D_CONV), lambda b, i: (b, i, 0)),
                pl.BlockSpec((1, TM1, HEAD_DIM), lambda b, i: (b, i, 0)),
                pl.BlockSpec((1, TM1, IDX_DIM), lambda b, i: (b, i, 0)),
                pl.BlockSpec((1, D_ATTN, TM1), lambda b, i: (b, 0, i)),
                pl.BlockSpec((1, N_IDX_HEADS * IDX_DIM, TM1), lambda b, i: (b, 0, i)),
                pl.BlockSpec((1, TM1 // TK, HEAD_DIM + V_EXTRA, TK), lambda b, i: (b, i, 0, 0)),
                pl.BlockSpec((1, N_IDX_HEADS, TM1), lambda b, i: (b, 0, i)),
            ],
            out_shape=[
                jax.ShapeDtypeStruct((bsz, seq, D_CONV), BF16),
                jax.ShapeDtypeStruct((bsz, seq, HEAD_DIM), BF16),
                jax.ShapeDtypeStruct((bsz, seq, IDX_DIM), BF16),
                jax.ShapeDtypeStruct((bsz, D_ATTN, seq), BF16),
                jax.ShapeDtypeStruct((bsz, N_IDX_HEADS * IDX_DIM, seq), BF16),
                jax.ShapeDtypeStruct((bsz, nck, HEAD_DIM + V_EXTRA, TK), BF16),
                jax.ShapeDtypeStruct((bsz, N_IDX_HEADS, seq), F32),
            ],
            scratch_shapes=[pltpu.VMEM((TM1 + 8, D_CONV), F32)],
            compiler_params=cparams(dimension_semantics=("arbitrary", "arbitrary")),
            name="proj",
        )(x, g_mix[layer][None, :], w_nat, w_tr, conv_w[layer])

        attn = pl.pallas_call(
            functools.partial(_dsa_kernel, k_top=k_top),
            grid=(bsz, seq // TQ),
            in_specs=[
                pl.BlockSpec(memory_space=pltpu.SMEM),
                pl.BlockSpec((1, N_IDX_HEADS * IDX_DIM, TQ), lambda b, i: (b, 0, i)),
                pl.BlockSpec((1, N_IDX_HEADS, TQ), lambda b, i: (b, 0, i)),
                pl.BlockSpec((1, seq, IDX_DIM), lambda b, i: (b, 0, 0)),
                pl.BlockSpec((1, D_ATTN, TQ), lambda b, i: (b, 0, i)),
                pl.BlockSpec((1, seq, HEAD_DIM), lambda b, i: (b, 0, 0)),
                pl.BlockSpec((1, nck, HEAD_DIM + V_EXTRA, TK), lambda b, i: (b, 0, 0, 0)),
            ],
            out_specs=pl.BlockSpec((1, TQ, D_ATTN), lambda b, i: (b, i, 0)),
            out_shape=jax.ShapeDtypeStruct((bsz, seq, D_ATTN), BF16),
            scratch_shapes=[
                pltpu.VMEM((nck, TK, TQ), I32),
                pltpu.VMEM((nck, 32, 8, TQ), I32),
                pltpu.VMEM((nck, 8, TQ), I32),
                pltpu.VMEM((nck + 2 * N_HEADS, TK, TQ), F32),
                pltpu.VMEM((2, N_HEADS, TK, TQ), F32),
                pltpu.VMEM((D_ATTN, TQ), F32),
                pltpu.VMEM((N_HEADS, TQ), F32),
                pltpu.VMEM((N_HEADS, TQ), F32),
                pltpu.VMEM((N_HEADS, TQ), F32),
                pltpu.VMEM((N_HEADS, TK, TQ), F32),
                pltpu.VMEM((N_HEADS, TK, TQ), F32),
            ],
            compiler_params=cparams(dimension_semantics=("arbitrary", "arbitrary")),
            name="dsa",
        )(rel_bias, iqT, iwT, ik_n, qT, k_n, vT)

        x = pl.pallas_call(
            functools.partial(_merge_kernel, final_norm=layer == g_mix.shape[0] - 1),
            grid=(bsz, seq // TM3),
            in_specs=[
                pl.BlockSpec((1, TM3, D_MODEL), lambda b, i: (b, i, 0)),
                pl.BlockSpec((1, TM3, D_CONV), lambda b, i: (b, i, 0)),
                pl.BlockSpec((1, TM3, D_ATTN), lambda b, i: (b, i, 0)),
                _const_spec((1, D_MODEL)),
                _const_spec((D_MODEL, 2 * D_MODEL)),
                _const_spec((1, 2 * D_MODEL)),
                _const_spec((D_CONV, D_MODEL)),
                _const_spec((D_ATTN, D_MODEL)),
                _const_spec((D_MODEL, D_MODEL)),
                _const_spec((1, D_MODEL)),
                _const_spec((D_MODEL, d_ff)),
                _const_spec((D_MODEL, d_ff)),
                _const_spec((d_ff, D_MODEL)),
                _const_spec((1, D_MODEL)),
            ],
            out_specs=pl.BlockSpec((1, TM3, D_MODEL), lambda b, i: (b, i, 0)),
            out_shape=jax.ShapeDtypeStruct((bsz, seq, D_MODEL), F32),
            compiler_params=cparams(dimension_semantics=("arbitrary", "arbitrary")),
            name="merge",
        )(x, mix, attn, g_mix[layer][None, :], w_g, b_gate[layer][None, :],
          w_branch_a[layer].astype(BF16), w_branch_b[layer].astype(BF16),
          w_out[layer].astype(BF16), g_ffn[layer][None, :],
          w_ffn_gate[layer].astype(BF16), w_ffn_up[layer].astype(BF16),
          w_ffn_down[layer].astype(BF16),
          g_final[None, :])
    return x
```

```python
import functools
import math

import numpy as np
import jax
import jax.numpy as jnp
from jax import lax
from jax.experimental import pallas as pl
from jax.experimental.pallas import tpu as pltpu

D_MODEL = 1024
D_CONV = 512
CONV_K = 3
N_HEADS = 8
HEAD_DIM = 64
D_ATTN = N_HEADS * HEAD_DIM
N_IDX_HEADS = 8
IDX_DIM = 64
TOPK_MAX = 256
N_BUCKETS = 32
MAX_EXACT = 16
MAX_DISTANCE = 128
EPS = 1e-6

F32 = jnp.float32
BF16 = jnp.bfloat16
I32 = jnp.int32

INT_MIN = -(2 ** 31)
NEG = -1e30
LOG2E = math.log2(math.e)

TM1 = 512
TQ = 256
TK = 256
TM3 = 256
V_EXTRA = 16
QUAD = 4
LANES = 128
STAGE1_LEAD = 2
VMEM_LIMIT = 56 * 1024 * 1024


def _bucket_steps():
    n = np.arange(2 * MAX_DISTANCE)
    large = MAX_EXACT + (np.log(np.maximum(n, 1).astype(np.float32) / MAX_EXACT)
                         / math.log(MAX_DISTANCE / MAX_EXACT)
                         * (N_BUCKETS - MAX_EXACT)).astype(np.int32)
    large = np.minimum(large, N_BUCKETS - 1)
    b = np.where(n < MAX_EXACT, n, large)
    assert np.all(b[MAX_DISTANCE:] == N_BUCKETS - 1) and np.all(np.diff(b) >= 0)
    steps = [(int(i), int(b[i])) for i in range(1, len(b)) if b[i] != b[i - 1]]
    return int(b[0]), steps


def _fold_rows(a, rows):
    parts = [a[i:i + rows, :] for i in range(0, a.shape[0], rows)]
    while len(parts) > 1:
        parts = [parts[i] + parts[i + 1] for i in range(0, len(parts) - 1, 2)] + (
            [parts[-1]] if len(parts) % 2 else [])
    return parts[0]


def _bit_transpose32(a):
    a = list(a)
    j, m = 16, 0x0000FFFF
    while j:
        k = 0
        while k < 32:
            t = (a[k] ^ (a[k + j] >> j)) & (m if m < 2 ** 31 else m - 2 ** 32)
            a[k] = a[k] ^ t
            a[k + j] = a[k + j] ^ (t << j)
            k = (k + j + 1) & ~j
        j >>= 1
        m ^= (m << j) & 0xFFFFFFFF
    return a


def _rms(x, g):
    return x * lax.rsqrt(jnp.mean(x * x, axis=-1, keepdims=True) + EPS) * g


def _proj_kernel(x_ref, g_ref, wn_ref, wt_ref, cw_ref,
                 mix_ref, k_ref, ik_ref, qT_ref, iqT_ref, vT_ref, iwT_ref, ubuf_ref):
    i = pl.program_id(1)
    x = x_ref[0]
    h = _rms(x, g_ref[...]).astype(BF16)

    def nat(c0, c1):
        return jnp.dot(h, wn_ref[:, c0:c1], preferred_element_type=F32)

    def tr(r0, r1):
        return lax.dot_general(wt_ref[r0:r1, :], h, (((1,), (1,)), ((), ())),
                               preferred_element_type=F32)

    u = nat(D_CONV, 2 * D_CONV) * nat(2 * D_CONV, 3 * D_CONV)

    @pl.when(i == 0)
    def _():
        ubuf_ref[0:8, :] = jnp.zeros((8, D_CONV), F32)

    ubuf_ref[8:8 + TM1, :] = u
    u1 = ubuf_ref[7:7 + TM1, :]
    u2 = ubuf_ref[6:6 + TM1, :]
    cw = cw_ref[...]
    y = cw[0:1, :] * u2 + cw[1:2, :] * u1 + cw[2:3, :] * u
    mix_ref[0] = (nat(0, D_CONV) * y).astype(BF16)
    ubuf_ref[0:8, :] = u[TM1 - 8:TM1, :]

    kk = nat(3 * D_CONV, 3 * D_CONV + HEAD_DIM + IDX_DIM)
    k_ref[0] = kk[:, 0:HEAD_DIM].astype(BF16)
    ik_ref[0] = kk[:, HEAD_DIM:HEAD_DIM + IDX_DIM].astype(BF16)

    qT_ref[0] = tr(0, D_ATTN).astype(BF16)
    iqT_ref[0] = tr(D_ATTN, 2 * D_ATTN).astype(BF16)
    vw = tr(2 * D_ATTN, 2 * D_ATTN + HEAD_DIM + V_EXTRA)
    vT = vw[0:HEAD_DIM, :].astype(BF16)
    ones_row = (lax.broadcasted_iota(I32, (V_EXTRA, TM1), 0) == 0).astype(F32).astype(BF16)
    vT = jnp.concatenate([vT, ones_row], axis=0)
    for j in range(TM1 // TK):
        vT_ref[0, j] = vT[:, j * TK:(j + 1) * TK]
    iwT_ref[0] = vw[HEAD_DIM:HEAD_DIM + N_IDX_HEADS, :] * (N_IDX_HEADS ** -0.5 * IDX_DIM ** -0.5)


def _dsa_kernel(rb_ref, iqT_ref, iwT_ref, ik_ref, qT_ref, k_ref, vT_ref,
                o_ref, key_ref, planes_ref, alive_ref, add_ref, bias_ref, oT_ref, m_ref, l_ref, al_ref,
                lta_ref, ltb_ref, *, k_top):
    b = pl.program_id(0)
    qi = pl.program_id(1)
    nchunks = qi + 1
    nck = key_ref.shape[0]
    n_keys = nck * TK

    @pl.when((b == 0) & (qi == 0))
    def _():
        b0, steps = _bucket_steps()
        row = lax.broadcasted_iota(I32, (TK, TQ), 0)
        col = lax.broadcasted_iota(I32, (TK, TQ), 1)
        for off in range(2):
            dist = col - row + off * TK
            for hh in range(N_HEADS):
                val = jnp.full((TK, TQ), rb_ref[b0, hh], F32)
                for n0, bk in steps:
                    val = jnp.where(dist >= n0, rb_ref[bk, hh], val)
                bias_ref[off, hh] = (val - rb_ref[N_BUCKETS - 1, hh]) * LOG2E

    w = iwT_ref[0]
    sub = TK // 4

    def score_step(c1, c2, dst_ref, src_ref, diagonal=False):
        if c1 is not None:
            ikc = ik_ref[0, pl.ds(pl.multiple_of(c1 * TK, TK), TK), :]
        if c2 is not None:
            row0 = lax.broadcasted_iota(I32, (sub, TQ), 0)
            lane = lax.broadcasted_iota(I32, (sub, TQ), 1)
        for g in range(4):
            if c1 is not None:
                for hh in range(2 * g, 2 * g + 2):
                    dst_ref[hh] = jnp.dot(ikc, iqT_ref[0, hh * IDX_DIM:(hh + 1) * IDX_DIM, :],
                                          preferred_element_type=F32)
            if c2 is not None:
                rows = slice(g * sub, (g + 1) * sub)
                acc = jnp.zeros((sub, TQ), F32)
                for hh in range(N_IDX_HEADS):
                    acc = acc + w[hh:hh + 1, :] * jnp.maximum(src_ref[hh, rows, :], 0.0)
                zero = acc == 0.0
                bits = pltpu.bitcast(jnp.where(zero, 0.0, acc), I32)
                key = bits ^ ((bits >> 31) & 0x7FFFFFFF)
                first = c2 * TK + g * sub
                key = jnp.where(zero, (n_keys - 1 - first) - row0, key)
                if diagonal:
                    key = jnp.where(row0 + (g * sub) <= lane, key, INT_MIN)
                key_ref[c2, rows, :] = key
        if c2 is not None:
            for l0 in range(0, TQ, LANES):
                planes = _bit_transpose32([key_ref[c2, 8 * v:8 * v + 8, l0:l0 + LANES]
                                           for v in range(TK // 8)])
                for bit in range(32):
                    plane = planes[31 - bit]
                    planes_ref[c2, bit, :, l0:l0 + LANES] = ~plane if bit == 31 else plane

    def run_pipeline(step, n_steps):
        step(0, None, lta_ref, None)

        def two_steps(j, carry):
            c = 2 * j
            step(c + 1, c, ltb_ref, lta_ref)
            step(c + 2, c + 1, lta_ref, ltb_ref)
            return carry

        lax.fori_loop(0, n_steps // 2, two_steps, 0)

        @pl.when(n_steps % 2 == 1)
        def _():
            step(n_steps, n_steps - 1, ltb_ref, lta_ref)
            step(None, n_steps, None, ltb_ref, diagonal=True)

        @pl.when(n_steps % 2 == 0)
        def _():
            step(None, n_steps, None, lta_ref, diagonal=True)

    run_pipeline(score_step, nchunks - 1)

    def count(pred):
        def body(c, acc):
            m = pred(key_ref[c], c).astype(I32)
            return acc + jnp.sum(m.reshape(TK // 8, 8, TQ), axis=0)
        acc = lax.fori_loop(0, nchunks, body, jnp.zeros((8, TQ), I32))
        return jnp.sum(acc, axis=0, keepdims=True)

    n_quads = (nchunks + QUAD - 1) // QUAD

    def init_alive(c, carry):
        alive_ref[c] = jnp.broadcast_to(jnp.where(c < nchunks, -1, 0).astype(I32), (8, TQ))
        return carry

    lax.fori_loop(0, n_quads * QUAD, init_alive, 0)

    def clear_padding(c, carry):
        planes_ref[c] = jnp.zeros((32, 8, TQ), I32)
        return carry

    lax.fori_loop(nchunks, n_quads * QUAD, clear_padding, 0)

    def select_pass(bit, state, first):
        k_rem, alive_n, thr_u, flip = state

        def quad(j, cnt):
            for i in range(QUAD):
                c = QUAD * j + i
                a = alive_ref[c]
                if not first:
                    a = a & (planes_ref[c, bit + 1] ^ flip)
                    alive_ref[c] = a
                cnt = cnt + lax.population_count(a & planes_ref[c, bit])
            return cnt

        ones = jnp.sum(lax.fori_loop(0, n_quads, quad, jnp.zeros((8, TQ), I32)),
                       axis=0, keepdims=True)
        take = ones >= k_rem
        return (jnp.where(take, k_rem, k_rem - ones), jnp.where(take, ones, alive_n - ones),
                thr_u | jnp.where(take, jnp.left_shift(jnp.int32(1), bit), 0),
                jnp.broadcast_to(jnp.where(take, 0, -1), (8, TQ)))

    state = (jnp.full((1, TQ), k_top, I32), jnp.full((1, TQ), nchunks * TK, I32),
             jnp.zeros((1, TQ), I32), jnp.zeros((8, TQ), I32))
    state = select_pass(31, state, first=True)
    k_rem, alive_n, thr_u, _ = lax.fori_loop(
        1, 32, lambda p, st: select_pass(31 - p, st, first=False), state)
    thr = thr_u ^ INT_MIN
    any_excess = jnp.max(((alive_n > k_rem) & (thr > INT_MIN)).astype(I32)) > 0
    thr = jnp.maximum(thr, INT_MIN + 1)

    @pl.when(jnp.logical_not(any_excess))
    def _():
        def mask_chunk(c, carry):
            add_ref[c] = jnp.where(key_ref[c] >= thr, 0.0, NEG).astype(F32)
            return carry
        lax.fori_loop(0, nchunks, mask_chunk, 0)

    @pl.when(any_excess)
    def _():
        room = k_top - count(lambda kc, c: kc > thr)

        def key_index(c):
            return lax.broadcasted_iota(I32, (TK, TQ), 0) + c * TK

        def step(it, lim):
            cand = lim + jnp.left_shift(jnp.int32(1), 12 - it)
            cnt = count(lambda kc, c: (kc == thr) & (key_index(c) < cand))
            return jnp.where(cnt <= room, cand, lim)

        tie_lim = lax.fori_loop(0, 13, step, jnp.zeros((1, TQ), I32))

        def mask_chunk(c, carry):
            kc = key_ref[c]
            sel = (kc > thr) | ((kc == thr) & (key_index(c) < tie_lim))
            add_ref[c] = jnp.where(sel, 0.0, NEG).astype(F32)
            return carry
        lax.fori_loop(0, nchunks, mask_chunk, 0)

    for off in range(2):
        @pl.when(qi >= off)
        def _():
            nm = add_ref[qi - off]
            for hh in range(N_HEADS):
                add_ref[nck + off * N_HEADS + hh] = nm + bias_ref[off, hh]

    def pipe_step(c1, c2, dst_ref, src_ref, diagonal=False):
        m_cur = m_ref[...]
        m_rows, l_rows = [], []
        if c1 is not None:
            kc = k_ref[0, pl.ds(pl.multiple_of(c1 * TK, TK), TK), :]
            near = c1 >= qi - 1
        if c2 is not None:
            vc = vT_ref[0, c2]
            alpha = al_ref[...]
        for i in range(N_HEADS + STAGE1_LEAD):
            if c1 is not None and i < N_HEADS:
                hh = i
                rows = slice(hh * HEAD_DIM, (hh + 1) * HEAD_DIM)
                tile = jnp.where(near, nck + (qi - c1) * N_HEADS + hh, c1)
                lt = jnp.dot(kc, qT_ref[0, rows, :], preferred_element_type=F32) + add_ref[tile]
                dst_ref[hh] = lt
                m_rows.append(jnp.max(lt, axis=0, keepdims=True))
                if hh == N_HEADS - 1:
                    m_new = jnp.maximum(m_cur, jnp.concatenate(m_rows, axis=0))
                    al_ref[...] = jnp.exp2(m_cur - m_new)
                    m_ref[...] = m_new
            if c2 is not None and i >= STAGE1_LEAD:
                hh = i - STAGE1_LEAD
                rows = slice(hh * HEAD_DIM, (hh + 1) * HEAD_DIM)
                p = jnp.exp2(src_ref[hh] - m_cur[hh:hh + 1, :]).astype(BF16)
                pv = jnp.dot(vc, p, preferred_element_type=F32)
                oT_ref[rows, :] = alpha[hh:hh + 1, :] * oT_ref[rows, :] + pv[0:HEAD_DIM, :]
                l_rows.append(pv[HEAD_DIM:HEAD_DIM + 1, :])
        if c2 is not None:
            l_ref[...] = alpha * l_ref[...] + jnp.concatenate(l_rows, axis=0)

    m_ref[...] = jnp.full((N_HEADS, TQ), NEG, F32)
    l_ref[...] = jnp.zeros((N_HEADS, TQ), F32)
    oT_ref[...] = jnp.zeros((D_ATTN, TQ), F32)
    run_pipeline(pipe_step, nchunks - 1)
    for hh in range(N_HEADS):
        rows = slice(hh * HEAD_DIM, (hh + 1) * HEAD_DIM)
        oT_ref[rows, :] = oT_ref[rows, :] / l_ref[hh:hh + 1, :]

    o_ref[0] = oT_ref[...].T.astype(BF16)


def _merge_kernel(x_ref, mix_ref, att_ref, gmix_ref, wg_ref, bg_ref, wa_ref, wb_ref, wo_ref,
                  gffn_ref, wfg_ref, wfu_ref, wfd_ref, gfin_ref, o_ref, *, final_norm):
    x = x_ref[0]
    h = _rms(x, gmix_ref[...]).astype(BF16)
    gates = jax.nn.sigmoid(jnp.dot(h, wg_ref[...], preferred_element_type=F32) + bg_ref[...])
    y_a = jnp.dot(mix_ref[0], wa_ref[...], preferred_element_type=F32)
    y_b = jnp.dot(att_ref[0], wb_ref[...], preferred_element_type=F32)
    merged = gates[:, :D_MODEL] * y_a + gates[:, D_MODEL:] * y_b
    x1 = x + jnp.dot(merged.astype(BF16), wo_ref[...], preferred_element_type=F32)
    h2 = _rms(x1, gffn_ref[...]).astype(BF16)
    fg = jnp.dot(h2, wfg_ref[...], preferred_element_type=F32)
    fu = jnp.dot(h2, wfu_ref[...], preferred_element_type=F32)
    a = (jax.nn.silu(fg) * fu).astype(BF16)
    x2 = x1 + jnp.dot(a, wfd_ref[...], preferred_element_type=F32)
    o_ref[0] = _rms(x2, gfin_ref[...]) if final_norm else x2


def _const_spec(shape):
    nd = len(shape)
    return pl.BlockSpec(shape, lambda *_: (0,) * nd, pipeline_mode=pl.Buffered(1))


def kernel(x, g_mix, w_in, b_gate, conv_w, w_branch_a, w_branch_b, w_out, rel_bias, g_ffn,
           w_ffn_gate, w_ffn_up, w_ffn_down, g_final):
    bsz, seq, d = x.shape
    assert d == D_MODEL and seq % TM1 == 0 and seq % TQ == 0 and seq % TM3 == 0 and TQ == TK
    k_top = min(TOPK_MAX, seq // 4)
    assert TK == 32 * 8 and (seq // TK) % QUAD == 0
    assert seq < 2 ** 23
    nck = seq // TK
    d_ff = w_ffn_gate.shape[-1]
    cparams = functools.partial(pltpu.CompilerParams, vmem_limit_bytes=VMEM_LIMIT)

    for layer in range(g_mix.shape[0]):
        w = w_in[layer]
        o = np.cumsum([0, D_CONV, D_CONV, D_CONV, D_ATTN, HEAD_DIM, HEAD_DIM,
                       N_IDX_HEADS * IDX_DIM, IDX_DIM, N_IDX_HEADS, D_MODEL, D_MODEL])
        w_cb, w_cc, w_cx, w_q, w_k, w_v, w_iq, w_ik, w_iw, w_ga, w_gb = [
            w[:, int(o[j]):int(o[j + 1])] for j in range(11)]
        w_nat = jnp.concatenate([w_cb, w_cc, w_cx, w_k, w_ik], axis=1).astype(BF16)
        w_pad = jnp.zeros((D_MODEL, V_EXTRA - N_IDX_HEADS), w.dtype)
        w_tr = jnp.concatenate([w_q * (HEAD_DIM ** -0.5 * LOG2E), w_iq, w_v, w_iw, w_pad],
                               axis=1).T.astype(BF16)
        w_g = jnp.concatenate([w_ga, w_gb], axis=1).astype(BF16)
        n_nat, n_tr = w_nat.shape[1], w_tr.shape[0]

        mix, k_n, ik_n, qT, iqT, vT, iwT = pl.pallas_call(
            _proj_kernel,
            grid=(bsz, seq // TM1),
            in_specs=[
                pl.BlockSpec((1, TM1, D_MODEL), lambda b, i: (b, i, 0)),
                _const_spec((1, D_MODEL)),
                _const_spec((D_MODEL, n_nat)),
                _const_spec((n_tr, D_MODEL)),
                _const_spec((CONV_K, D_CONV)),
            ],
            out_specs=[
                pl.BlockSpec((1, TM1, D_CONV), lambda b, i: (b, i, 0)),
                pl.BlockSpec((1, TM1, HEAD_DIM), lambda b, i: (b, i, 0)),
                pl.BlockSpec((1, TM1, IDX_DIM), lambda b, i: (b, i, 0)),
                pl.BlockSpec((1, D_ATTN, TM1), lambda b, i: (b, 0, i)),
                pl.BlockSpec((1, N_IDX_HEADS * IDX_DIM, TM1), lambda b, i: (b, 0, i)),
                pl.BlockSpec((1, TM1 // TK, HEAD_DIM + V_EXTRA, TK), lambda b, i: (b, i, 0, 0)),
                pl.BlockSpec((1, N_IDX_HEADS, TM1), lambda b, i: (b, 0, i)),
            ],
            out_shape=[
                jax.ShapeDtypeStruct((bsz, seq, D_CONV), BF16),
                jax.ShapeDtypeStruct((bsz, seq, HEAD_DIM), BF16),
                jax.ShapeDtypeStruct((bsz, seq, IDX_DIM), BF16),
                jax.ShapeDtypeStruct((bsz, D_ATTN, seq), BF16),
                jax.ShapeDtypeStruct((bsz, N_IDX_HEADS * IDX_DIM, seq), BF16),
                jax.ShapeDtypeStruct((bsz, nck, HEAD_DIM + V_EXTRA, TK), BF16),
                jax.ShapeDtypeStruct((bsz, N_IDX_HEADS, seq), F32),
            ],
            scratch_shapes=[pltpu.VMEM((TM1 + 8, D_CONV), F32)],
            compiler_params=cparams(dimension_semantics=("arbitrary", "arbitrary")),
            name="proj",
        )(x, g_mix[layer][None, :], w_nat, w_tr, conv_w[layer])

        attn = pl.pallas_call(
            functools.partial(_dsa_kernel, k_top=k_top),
            grid=(bsz, seq // TQ),
            in_specs=[
                pl.BlockSpec(memory_space=pltpu.SMEM),
                pl.BlockSpec((1, N_IDX_HEADS * IDX_DIM, TQ), lambda b, i: (b, 0, i)),
                pl.BlockSpec((1, N_IDX_HEADS, TQ), lambda b, i: (b, 0, i)),
                pl.BlockSpec((1, seq, IDX_DIM), lambda b, i: (b, 0, 0)),
                pl.BlockSpec((1, D_ATTN, TQ), lambda b, i: (b, 0, i)),
                pl.BlockSpec((1, seq, HEAD_DIM), lambda b, i: (b, 0, 0)),
                pl.BlockSpec((1, nck, HEAD_DIM + V_EXTRA, TK), lambda b, i: (b, 0, 0, 0)),
            ],
            out_specs=pl.BlockSpec((1, TQ, D_ATTN), lambda b, i: (b, i, 0)),
            out_shape=jax.ShapeDtypeStruct((bsz, seq, D_ATTN), BF16),
            scratch_shapes=[
                pltpu.VMEM((nck, TK, TQ), I32),
                pltpu.VMEM((nck, 32, 8, TQ), I32),
                pltpu.VMEM((nck, 8, TQ), I32),
                pltpu.VMEM((nck + 2 * N_HEADS, TK, TQ), F32),
                pltpu.VMEM((2, N_HEADS, TK, TQ), F32),
                pltpu.VMEM((D_ATTN, TQ), F32),
                pltpu.VMEM((N_HEADS, TQ), F32),
                pltpu.VMEM((N_HEADS, TQ), F32),
                pltpu.VMEM((N_HEADS, TQ), F32),
                pltpu.VMEM((N_HEADS, TK, TQ), F32),
                pltpu.VMEM((N_HEADS, TK, TQ), F32),
            ],
            compiler_params=cparams(dimension_semantics=("arbitrary", "arbitrary")),
            name="dsa",
        )(rel_bias, iqT, iwT, ik_n, qT, k_n, vT)

        x = pl.pallas_call(
            functools.partial(_merge_kernel, final_norm=layer == g_mix.shape[0] - 1),
            grid=(bsz, seq // TM3),
            in_specs=[
                pl.BlockSpec((1, TM3, D_MODEL), lambda b, i: (b, i, 0)),
                pl.BlockSpec((1, TM3, D_CONV), lambda b, i: (b, i, 0)),
                pl.BlockSpec((1, TM3, D_ATTN), lambda b, i: (b, i, 0)),
                _const_spec((1, D_MODEL)),
                _const_spec((D_MODEL, 2 * D_MODEL)),
                _const_spec((1, 2 * D_MODEL)),
                _const_spec((D_CONV, D_MODEL)),
                _const_spec((D_ATTN, D_MODEL)),
                _const_spec((D_MODEL, D_MODEL)),
                _const_spec((1, D_MODEL)),
                _const_spec((D_MODEL, d_ff)),
                _const_spec((D_MODEL, d_ff)),
                _const_spec((d_ff, D_MODEL)),
                _const_spec((1, D_MODEL)),
            ],
            out_specs=pl.BlockSpec((1, TM3, D_MODEL), lambda b, i: (b, i, 0)),
            out_shape=jax.ShapeDtypeStruct((bsz, seq, D_MODEL), F32),
            compiler_params=cparams(dimension_semantics=("arbitrary", "arbitrary")),
            name="merge",
        )(x, mix, attn, g_mix[layer][None, :], w_g, b_gate[layer][None, :],
          w_branch_a[layer].astype(BF16), w_branch_b[layer].astype(BF16),
          w_out[layer].astype(BF16), g_ffn[layer][None, :],
          w_ffn_gate[layer].astype(BF16), w_ffn_up[layer].astype(BF16),
          w_ffn_down[layer].astype(BF16),
          g_final[None, :])
    return x
```

```python
import functools
import math

import numpy as np
import jax
import jax.numpy as jnp
from jax import lax
from jax.experimental import pallas as pl
from jax.experimental.pallas import tpu as pltpu

D_MODEL = 1024
D_CONV = 512
CONV_K = 3
N_HEADS = 8
HEAD_DIM = 64
D_ATTN = N_HEADS * HEAD_DIM
N_IDX_HEADS = 8
IDX_DIM = 64
TOPK_MAX = 256
N_BUCKETS = 32
MAX_EXACT = 16
MAX_DISTANCE = 128
EPS = 1e-6

F32 = jnp.float32
BF16 = jnp.bfloat16
I32 = jnp.int32

INT_MIN = -(2 ** 31)
NEG = -1e30
LOG2E = math.log2(math.e)

TM1 = 512
TQ = 256
TK = 256
TM3 = 256
V_EXTRA = 16
QUAD = 4
LANES = 128
STAGE1_LEAD = 2
VMEM_LIMIT = 56 * 1024 * 1024


def _bucket_steps():
    n = np.arange(2 * MAX_DISTANCE)
    large = MAX_EXACT + (np.log(np.maximum(n, 1).astype(np.float32) / MAX_EXACT)
                         / math.log(MAX_DISTANCE / MAX_EXACT)
                         * (N_BUCKETS - MAX_EXACT)).astype(np.int32)
    large = np.minimum(large, N_BUCKETS - 1)
    b = np.where(n < MAX_EXACT, n, large)
    assert np.all(b[MAX_DISTANCE:] == N_BUCKETS - 1) and np.all(np.diff(b) >= 0)
    steps = [(int(i), int(b[i])) for i in range(1, len(b)) if b[i] != b[i - 1]]
    return int(b[0]), steps


def _fold_rows(a, rows):
    parts = [a[i:i + rows, :] for i in range(0, a.shape[0], rows)]
    while len(parts) > 1:
        parts = [parts[i] + parts[i + 1] for i in range(0, len(parts) - 1, 2)] + (
            [parts[-1]] if len(parts) % 2 else [])
    return parts[0]


def _bit_transpose32(a):
    a = list(a)
    j, m = 16, 0x0000FFFF
    while j:
        k = 0
        while k < 32:
            t = (a[k] ^ (a[k + j] >> j)) & (m if m < 2 ** 31 else m - 2 ** 32)
            a[k] = a[k] ^ t
            a[k + j] = a[k + j] ^ (t << j)
            k = (k + j + 1) & ~j
        j >>= 1
        m ^= (m << j) & 0xFFFFFFFF
    return a


def _rms(x, g):
    return x * lax.rsqrt(jnp.mean(x * x, axis=-1, keepdims=True) + EPS) * g


def _proj_kernel(x_ref, g_ref, wn_ref, wt_ref, cw_ref,
                 mix_ref, k_ref, ik_ref, qT_ref, iqT_ref, vT_ref, iwT_ref, ubuf_ref):
    i = pl.program_id(1)
    parts = [slice(j * TK, (j + 1) * TK) for j in range(TM1 // TK)]
    hs = [_rms(x_ref[0, r, :], g_ref[...]).astype(BF16) for r in parts]

    def nat(h, c0, c1):
        return jnp.dot(h, wn_ref[:, c0:c1], preferred_element_type=F32)

    def tr(h, r0, r1):
        return lax.dot_general(wt_ref[r0:r1, :], h, (((1,), (1,)), ((), ())),
                               preferred_element_type=F32)

    @pl.when(i == 0)
    def _():
        ubuf_ref[0:8, :] = jnp.zeros((8, D_CONV), F32)

    us = [nat(h, D_CONV, 2 * D_CONV) * nat(h, 2 * D_CONV, 3 * D_CONV) for h in hs]
    for r, u in zip(parts, us):
        ubuf_ref[8 + r.start:8 + r.stop, :] = u
    cw = cw_ref[...]
    for r, h, u in zip(parts, hs, us):
        u1 = ubuf_ref[7 + r.start:7 + r.stop, :]
        u2 = ubuf_ref[6 + r.start:6 + r.stop, :]
        y = cw[0:1, :] * u2 + cw[1:2, :] * u1 + cw[2:3, :] * u
        mix_ref[0, r, :] = (nat(h, 0, D_CONV) * y).astype(BF16)
    ubuf_ref[0:8, :] = us[-1][TK - 8:TK, :]

    for r, h in zip(parts, hs):
        kk = nat(h, 3 * D_CONV, 3 * D_CONV + HEAD_DIM + IDX_DIM)
        k_ref[0, r, :] = kk[:, 0:HEAD_DIM].astype(BF16)
        ik_ref[0, r, :] = kk[:, HEAD_DIM:HEAD_DIM + IDX_DIM].astype(BF16)
    for r, h in zip(parts, hs):
        qT_ref[0, :, r] = tr(h, 0, D_ATTN).astype(BF16)
    for r, h in zip(parts, hs):
        iqT_ref[0, :, r] = tr(h, D_ATTN, 2 * D_ATTN).astype(BF16)
    ones_row = (lax.broadcasted_iota(I32, (V_EXTRA, TK), 0) == 0).astype(F32).astype(BF16)
    for j, (r, h) in enumerate(zip(parts, hs)):
        vw = tr(h, 2 * D_ATTN, 2 * D_ATTN + HEAD_DIM + V_EXTRA)
        vT = vw[0:HEAD_DIM, :].astype(BF16)
        vT_ref[0, j] = jnp.concatenate([vT, ones_row], axis=0)
        iwT_ref[0, :, r] = vw[HEAD_DIM:HEAD_DIM + N_IDX_HEADS, :] * (
            N_IDX_HEADS ** -0.5 * IDX_DIM ** -0.5)


def _dsa_kernel(rb_ref, iqT_ref, iwT_ref, ik_ref, qT_ref, k_ref, vT_ref,
                o_ref, key_ref, planes_ref, alive_ref, add_ref, bias_ref, oT_ref, m_ref, l_ref, al_ref,
                lta_ref, ltb_ref, *, k_top):
    b = pl.program_id(0)
    qi = pl.program_id(1)
    nchunks = qi + 1
    nck = key_ref.shape[0]
    n_keys = nck * TK

    @pl.when((b == 0) & (qi == 0))
    def _():
        b0, steps = _bucket_steps()
        row = lax.broadcasted_iota(I32, (TK, TQ), 0)
        col = lax.broadcasted_iota(I32, (TK, TQ), 1)
        for off in range(2):
            dist = col - row + off * TK
            for hh in range(N_HEADS):
                val = jnp.full((TK, TQ), rb_ref[b0, hh], F32)
                for n0, bk in steps:
                    val = jnp.where(dist >= n0, rb_ref[bk, hh], val)
                bias_ref[off, hh] = (val - rb_ref[N_BUCKETS - 1, hh]) * LOG2E

    w = iwT_ref[0]
    sub = TK // N_IDX_HEADS

    def score_step(c1, c2, dst_ref, src_ref, diagonal=False):
        if c1 is not None:
            ikc = ik_ref[0, pl.ds(pl.multiple_of(c1 * TK, TK), TK), :]
        if c2 is not None:
            wb = [jnp.broadcast_to(w[hh:hh + 1, :], (8, TQ)) for hh in range(N_IDX_HEADS)]
            row0 = lax.broadcasted_iota(I32, (sub, TQ), 0)
            lane = lax.broadcasted_iota(I32, (sub, TQ), 1)
        for g in range(N_IDX_HEADS):
            if c1 is not None:
                dst_ref[g] = jnp.dot(ikc, iqT_ref[0, g * IDX_DIM:(g + 1) * IDX_DIM, :],
                                     preferred_element_type=F32)
            if c2 is not None:
                rows = slice(g * sub, (g + 1) * sub)
                def term(hh):
                    r = jnp.maximum(src_ref[hh, rows, :], 0.0).reshape(sub // 8, 8, TQ)
                    return wb[hh][None] * r
                acc = term(0)
                for hh in range(1, N_IDX_HEADS):
                    acc = acc + term(hh)
                acc = acc.reshape(sub, TQ)
                zero = acc == 0.0
                bits = pltpu.bitcast(acc, I32)
                key = bits ^ ((bits >> 31) & 0x7FFFFFFF)
                first = c2 * TK + g * sub
                key = jnp.where(zero, (n_keys - 1 - first) - row0, key)
                if diagonal:
                    key = jnp.where(row0 + (g * sub) <= lane, key, INT_MIN)
                key_ref[c2, rows, :] = key
        if c2 is not None:
            for l0 in range(0, TQ, LANES):
                planes = _bit_transpose32([key_ref[c2, 8 * v:8 * v + 8, l0:l0 + LANES]
                                           for v in range(TK // 8)])
                for bit in range(32):
                    plane = planes[31 - bit]
                    planes_ref[c2, bit, :, l0:l0 + LANES] = ~plane if bit == 31 else plane

    def run_pipeline(step, n_steps):
        step(0, None, lta_ref, None)

        def two_steps(j, carry):
            c = 2 * j
            step(c + 1, c, ltb_ref, lta_ref)
            step(c + 2, c + 1, lta_ref, ltb_ref)
            return carry

        lax.fori_loop(0, n_steps // 2, two_steps, 0)

        @pl.when(n_steps % 2 == 1)
        def _():
            step(n_steps, n_steps - 1, ltb_ref, lta_ref)
            step(None, n_steps, None, ltb_ref, diagonal=True)

        @pl.when(n_steps % 2 == 0)
        def _():
            step(None, n_steps, None, lta_ref, diagonal=True)

    run_pipeline(score_step, nchunks - 1)

    def count(pred):
        def body(c, acc):
            m = pred(key_ref[c], c).astype(I32)
            return acc + jnp.sum(m.reshape(TK // 8, 8, TQ), axis=0)
        acc = lax.fori_loop(0, nchunks, body, jnp.zeros((8, TQ), I32))
        return jnp.sum(acc, axis=0, keepdims=True)

    n_quads = (nchunks + QUAD - 1) // QUAD

    def init_alive(c, carry):
        alive_ref[c] = jnp.broadcast_to(jnp.where(c < nchunks, -1, 0).astype(I32), (8, TQ))
        return carry

    lax.fori_loop(0, n_quads * QUAD, init_alive, 0)

    def clear_padding(c, carry):
        planes_ref[c] = jnp.zeros((32, 8, TQ), I32)
        return carry

    lax.fori_loop(nchunks, n_quads * QUAD, clear_padding, 0)

    def select_pass(bit, state, first):
        k_rem, alive_n, thr_u, flip = state

        def quad(j, cnt):
            for i in range(QUAD):
                c = QUAD * j + i
                a = alive_ref[c]
                if not first:
                    a = a & (planes_ref[c, bit + 1] ^ flip)
                    alive_ref[c] = a
                cnt = cnt + lax.population_count(a & planes_ref[c, bit])
            return cnt

        ones = jnp.sum(lax.fori_loop(0, n_quads, quad, jnp.zeros((8, TQ), I32)),
                       axis=0, keepdims=True)
        take = ones >= k_rem
        return (jnp.where(take, k_rem, k_rem - ones), jnp.where(take, ones, alive_n - ones),
                thr_u | jnp.where(take, jnp.left_shift(jnp.int32(1), bit), 0),
                jnp.broadcast_to(jnp.where(take, 0, -1), (8, TQ)))

    state = (jnp.full((1, TQ), k_top, I32), jnp.full((1, TQ), nchunks * TK, I32),
             jnp.zeros((1, TQ), I32), jnp.zeros((8, TQ), I32))
    state = select_pass(31, state, first=True)
    k_rem, alive_n, thr_u, _ = lax.fori_loop(
        1, 32, lambda p, st: select_pass(31 - p, st, first=False), state)
    thr = thr_u ^ INT_MIN
    any_excess = jnp.max(((alive_n > k_rem) & (thr > INT_MIN)).astype(I32)) > 0
    thr = jnp.maximum(thr, INT_MIN + 1)

    @pl.when(jnp.logical_not(any_excess))
    def _():
        def mask_chunk(c, carry):
            add_ref[c] = jnp.where(key_ref[c] >= thr, 0.0, NEG).astype(F32)
            return carry
        lax.fori_loop(0, nchunks, mask_chunk, 0)

    @pl.when(any_excess)
    def _():
        room = k_top - count(lambda kc, c: kc > thr)

        def key_index(c):
            return lax.broadcasted_iota(I32, (TK, TQ), 0) + c * TK

        def step(it, lim):
            cand = lim + jnp.left_shift(jnp.int32(1), 12 - it)
            cnt = count(lambda kc, c: (kc == thr) & (key_index(c) < cand))
            return jnp.where(cnt <= room, cand, lim)

        tie_lim = lax.fori_loop(0, 13, step, jnp.zeros((1, TQ), I32))

        def mask_chunk(c, carry):
            kc = key_ref[c]
            sel = (kc > thr) | ((kc == thr) & (key_index(c) < tie_lim))
            add_ref[c] = jnp.where(sel, 0.0, NEG).astype(F32)
            return carry
        lax.fori_loop(0, nchunks, mask_chunk, 0)

    for off in range(2):
        @pl.when(qi >= off)
        def _():
            nm = add_ref[qi - off]
            for hh in range(N_HEADS):
                add_ref[nck + off * N_HEADS + hh] = nm + bias_ref[off, hh]

    def pipe_step(c1, c2, dst_ref, src_ref, diagonal=False):
        m_cur = m_ref[...]
        m_rows, l_rows = [], []
        if c1 is not None:
            kc = k_ref[0, pl.ds(pl.multiple_of(c1 * TK, TK), TK), :]
            near = c1 >= qi - 1
        if c2 is not None:
            vc = vT_ref[0, c2]
            alpha = al_ref[...]
        for i in range(N_HEADS + STAGE1_LEAD):
            if c1 is not None and i < N_HEADS:
                hh = i
                rows = slice(hh * HEAD_DIM, (hh + 1) * HEAD_DIM)
                tile = jnp.where(near, nck + (qi - c1) * N_HEADS + hh, c1)
                lt = jnp.dot(kc, qT_ref[0, rows, :], preferred_element_type=F32) + add_ref[tile]
                dst_ref[hh] = lt
                m_rows.append(jnp.max(lt, axis=0, keepdims=True))
                if hh == N_HEADS - 1:
                    m_new = jnp.maximum(m_cur, jnp.concatenate(m_rows, axis=0))
                    al_ref[...] = jnp.exp2(m_cur - m_new)
                    m_ref[...] = m_new
            if c2 is not None and i >= STAGE1_LEAD:
                hh = i - STAGE1_LEAD
                rows = slice(hh * HEAD_DIM, (hh + 1) * HEAD_DIM)
                p = jnp.exp2(src_ref[hh] - m_cur[hh:hh + 1, :]).astype(BF16)
                pv = jnp.dot(vc, p, preferred_element_type=F32)
                oT_ref[rows, :] = alpha[hh:hh + 1, :] * oT_ref[rows, :] + pv[0:HEAD_DIM, :]
                l_rows.append(pv[HEAD_DIM:HEAD_DIM + 1, :])
        if c2 is not None:
            l_ref[...] = alpha * l_ref[...] + jnp.concatenate(l_rows, axis=0)

    m_ref[...] = jnp.full((N_HEADS, TQ), NEG, F32)
    l_ref[...] = jnp.zeros((N_HEADS, TQ), F32)
    oT_ref[...] = jnp.zeros((D_ATTN, TQ), F32)
    run_pipeline(pipe_step, nchunks - 1)
    for hh in range(N_HEADS):
        rows = slice(hh * HEAD_DIM, (hh + 1) * HEAD_DIM)
        oT_ref[rows, :] = oT_ref[rows, :] / l_ref[hh:hh + 1, :]

    o_ref[0] = oT_ref[...].T.astype(BF16)


def _merge_kernel(x_ref, mix_ref, att_ref, gmix_ref, wg_ref, bg_ref, wa_ref, wb_ref, wo_ref,
                  gffn_ref, wfg_ref, wfu_ref, wfd_ref, gfin_ref, o_ref, *, final_norm):
    parts = [slice(j * (TM3 // 2), (j + 1) * (TM3 // 2)) for j in range(2)]

    def dot(a, w_ref):
        return jnp.dot(a, w_ref[...], preferred_element_type=F32)

    xs = [x_ref[0, r, :] for r in parts]
    hs = [_rms(x, gmix_ref[...]).astype(BF16) for x in xs]
    gs = [jax.nn.sigmoid(dot(h, wg_ref) + bg_ref[...]) for h in hs]
    yas = [dot(mix_ref[0, r, :], wa_ref) for r in parts]
    ybs = [dot(att_ref[0, r, :], wb_ref) for r in parts]
    ms = [(g[:, :D_MODEL] * ya + g[:, D_MODEL:] * yb).astype(BF16) for g, ya, yb in zip(gs, yas, ybs)]
    x1s = [x + dot(m, wo_ref) for x, m in zip(xs, ms)]
    h2s = [_rms(x1, gffn_ref[...]).astype(BF16) for x1 in x1s]
    fgs = [dot(h2, wfg_ref) for h2 in h2s]
    fus = [dot(h2, wfu_ref) for h2 in h2s]
    acts = [(jax.nn.silu(fg) * fu).astype(BF16) for fg, fu in zip(fgs, fus)]
    x2s = [x1 + dot(a, wfd_ref) for x1, a in zip(x1s, acts)]
    for r, x2 in zip(parts, x2s):
        o_ref[0, r, :] = _rms(x2, gfin_ref[...]) if final_norm else x2


def _const_spec(shape):
    nd = len(shape)
    return pl.BlockSpec(shape, lambda *_: (0,) * nd, pipeline_mode=pl.Buffered(1))


def kernel(x, g_mix, w_in, b_gate, conv_w, w_branch_a, w_branch_b, w_out, rel_bias, g_ffn,
           w_ffn_gate, w_ffn_up, w_ffn_down, g_final):
    bsz, seq, d = x.shape
    assert d == D_MODEL and seq % TM1 == 0 and seq % TQ == 0 and seq % TM3 == 0 and TQ == TK
    k_top = min(TOPK_MAX, seq // 4)
    assert TK == 32 * 8 and (seq // TK) % QUAD == 0
    assert seq < 2 ** 23
    nck = seq // TK
    d_ff = w_ffn_gate.shape[-1]
    cparams = functools.partial(pltpu.CompilerParams, vmem_limit_bytes=VMEM_LIMIT)

    for layer in range(g_mix.shape[0]):
        w = w_in[layer]
        o = np.cumsum([0, D_CONV, D_CONV, D_CONV, D_ATTN, HEAD_DIM, HEAD_DIM,
                       N_IDX_HEADS * IDX_DIM, IDX_DIM, N_IDX_HEADS, D_MODEL, D_MODEL])
        w_cb, w_cc, w_cx, w_q, w_k, w_v, w_iq, w_ik, w_iw, w_ga, w_gb = [
            w[:, int(o[j]):int(o[j + 1])] for j in range(11)]
        w_nat = jnp.concatenate([w_cb, w_cc, w_cx, w_k, w_ik], axis=1).astype(BF16)
        w_pad = jnp.zeros((D_MODEL, V_EXTRA - N_IDX_HEADS), w.dtype)
        w_tr = jnp.concatenate([w_q * (HEAD_DIM ** -0.5 * LOG2E), w_iq, w_v, w_iw, w_pad],
                               axis=1).T.astype(BF16)
        w_g = jnp.concatenate([w_ga, w_gb], axis=1).astype(BF16)
        n_nat, n_tr = w_nat.shape[1], w_tr.shape[0]

        mix, k_n, ik_n, qT, iqT, vT, iwT = pl.pallas_call(
            _proj_kernel,
            grid=(bsz, seq // TM1),
            in_specs=[
                pl.BlockSpec((1, TM1, D_MODEL), lambda b, i: (b, i, 0)),
                _const_spec((1, D_MODEL)),
                _const_spec((D_MODEL, n_nat)),
                _const_spec((n_tr, D_MODEL)),
                _const_spec((CONV_K, D_CONV)),
            ],
            out_specs=[
                pl.BlockSpec((1, TM1, D_CONV), lambda b, i: (b, i, 0)),
                pl.BlockSpec((1, TM1, HEAD_DIM), lambda b, i: (b, i, 0)),
                pl.BlockSpec((1, TM1, IDX_DIM), lambda b, i: (b, i, 0)),
                pl.BlockSpec((1, D_ATTN, TM1), lambda b, i: (b, 0, i)),
                pl.BlockSpec((1, N_IDX_HEADS * IDX_DIM, TM1), lambda b, i: (b, 0, i)),
                pl.BlockSpec((1, TM1 // TK, HEAD_DIM + V_EXTRA, TK), lambda b, i: (b, i, 0, 0)),
                pl.BlockSpec((1, N_IDX_HEADS, TM1), lambda b, i: (b, 0, i)),
            ],
            out_shape=[
                jax.ShapeDtypeStruct((bsz, seq, D_CONV), BF16),
                jax.ShapeDtypeStruct((bsz, seq, HEAD_DIM), BF16),
                jax.ShapeDtypeStruct((bsz, seq, IDX_DIM), BF16),
                jax.ShapeDtypeStruct((bsz, D_ATTN, seq), BF16),
                jax.ShapeDtypeStruct((bsz, N_IDX_HEADS * IDX_DIM, seq), BF16),
                jax.ShapeDtypeStruct((bsz, nck, HEAD_DIM + V_EXTRA, TK), BF16),
                jax.ShapeDtypeStruct((bsz, N_IDX_HEADS, seq), F32),
            ],
            scratch_shapes=[pltpu.VMEM((TM1 + 8, D_CONV), F32)],
            compiler_params=cparams(dimension_semantics=("arbitrary", "arbitrary")),
            name="proj",
        )(x, g_mix[layer][None, :], w_nat, w_tr, conv_w[layer])

        attn = pl.pallas_call(
            functools.partial(_dsa_kernel, k_top=k_top),
            grid=(bsz, seq // TQ),
            in_specs=[
                pl.BlockSpec(memory_space=pltpu.SMEM),
                pl.BlockSpec((1, N_IDX_HEADS * IDX_DIM, TQ), lambda b, i: (b, 0, i)),
                pl.BlockSpec((1, N_IDX_HEADS, TQ), lambda b, i: (b, 0, i)),
                pl.BlockSpec((1, seq, IDX_DIM), lambda b, i: (b, 0, 0)),
                pl.BlockSpec((1, D_ATTN, TQ), lambda b, i: (b, 0, i)),
                pl.BlockSpec((1, seq, HEAD_DIM), lambda b, i: (b, 0, 0)),
                pl.BlockSpec((1, nck, HEAD_DIM + V_EXTRA, TK), lambda b, i: (b, 0, 0, 0)),
            ],
            out_specs=pl.BlockSpec((1, TQ, D_ATTN), lambda b, i: (b, i, 0)),
            out_shape=jax.ShapeDtypeStruct((bsz, seq, D_ATTN), BF16),
            scratch_shapes=[
                pltpu.VMEM((nck, TK, TQ), I32),
                pltpu.VMEM((nck, 32, 8, TQ), I32),
                pltpu.VMEM((nck, 8, TQ), I32),
                pltpu.VMEM((nck + 2 * N_HEADS, TK, TQ), F32),
                pltpu.VMEM((2, N_HEADS, TK, TQ), F32),
                pltpu.VMEM((D_ATTN, TQ), F32),
                pltpu.VMEM((N_HEADS, TQ), F32),
                pltpu.VMEM((N_HEADS, TQ), F32),
                pltpu.VMEM((N_HEADS, TQ), F32),
                pltpu.VMEM((N_HEADS, TK, TQ), F32),
                pltpu.VMEM((N_HEADS, TK, TQ), F32),
            ],
            compiler_params=cparams(dimension_semantics=("arbitrary", "arbitrary")),
            name="dsa",
        )(rel_bias, iqT, iwT, ik_n, qT, k_n, vT)

        x = pl.pallas_call(
            functools.partial(_merge_kernel, final_norm=layer == g_mix.shape[0] - 1),
            grid=(bsz, seq // TM3),
            in_specs=[
                pl.BlockSpec((1, TM3, D_MODEL), lambda b, i: (b, i, 0)),
                pl.BlockSpec((1, TM3, D_CONV), lambda b, i: (b, i, 0)),
                pl.BlockSpec((1, TM3, D_ATTN), lambda b, i: (b, i, 0)),
                _const_spec((1, D_MODEL)),
                _const_spec((D_MODEL, 2 * D_MODEL)),
                _const_spec((1, 2 * D_MODEL)),
                _const_spec((D_CONV, D_MODEL)),
                _const_spec((D_ATTN, D_MODEL)),
                _const_spec((D_MODEL, D_MODEL)),
                _const_spec((1, D_MODEL)),
                _const_spec((D_MODEL, d_ff)),
                _const_spec((D_MODEL, d_ff)),
                _const_spec((d_ff, D_MODEL)),
                _const_spec((1, D_MODEL)),
            ],
            out_specs=pl.BlockSpec((1, TM3, D_MODEL), lambda b, i: (b, i, 0)),
            out_shape=jax.ShapeDtypeStruct((bsz, seq, D_MODEL), F32),
            compiler_params=cparams(dimension_semantics=("arbitrary", "arbitrary")),
            name="merge",
        )(x, mix, attn, g_mix[layer][None, :], w_g, b_gate[layer][None, :],
          w_branch_a[layer].astype(BF16), w_branch_b[layer].astype(BF16),
          w_out[layer].astype(BF16), g_ffn[layer][None, :],
          w_ffn_gate[layer].astype(BF16), w_ffn_up[layer].astype(BF16),
          w_ffn_down[layer].astype(BF16),
          g_final[None, :])
    return x
```

```python
import functools
import math

import numpy as np
import jax
import jax.numpy as jnp
from jax import lax
from jax.experimental import pallas as pl
from jax.experimental.pallas import tpu as pltpu

D_MODEL = 1024
D_CONV = 512
CONV_K = 3
N_HEADS = 8
HEAD_DIM = 64
D_ATTN = N_HEADS * HEAD_DIM
N_IDX_HEADS = 8
IDX_DIM = 64
TOPK_MAX = 256
N_BUCKETS = 32
MAX_EXACT = 16
MAX_DISTANCE = 128
EPS = 1e-6

F32 = jnp.float32
BF16 = jnp.bfloat16
I32 = jnp.int32

INT_MIN = -(2 ** 31)
NEG = -1e30
LOG2E = math.log2(math.e)

TM1 = 512
TQ = 256
TK = 256
TM3 = 256
V_EXTRA = 16
QUAD = 4
LANES = 128
STAGE1_LEAD = 3
VMEM_LIMIT = 56 * 1024 * 1024


def _bucket_steps():
    n = np.arange(2 * MAX_DISTANCE)
    large = MAX_EXACT + (np.log(np.maximum(n, 1).astype(np.float32) / MAX_EXACT)
                         / math.log(MAX_DISTANCE / MAX_EXACT)
                         * (N_BUCKETS - MAX_EXACT)).astype(np.int32)
    large = np.minimum(large, N_BUCKETS - 1)
    b = np.where(n < MAX_EXACT, n, large)
    assert np.all(b[MAX_DISTANCE:] == N_BUCKETS - 1) and np.all(np.diff(b) >= 0)
    steps = [(int(i), int(b[i])) for i in range(1, len(b)) if b[i] != b[i - 1]]
    return int(b[0]), steps


def _fold_rows(a, rows):
    parts = [a[i:i + rows, :] for i in range(0, a.shape[0], rows)]
    while len(parts) > 1:
        parts = [parts[i] + parts[i + 1] for i in range(0, len(parts) - 1, 2)] + (
            [parts[-1]] if len(parts) % 2 else [])
    return parts[0]


def _bit_transpose32(a):
    a = list(a)
    j, m = 16, 0x0000FFFF
    while j:
        k = 0
        while k < 32:
            t = (a[k] ^ (a[k + j] >> j)) & (m if m < 2 ** 31 else m - 2 ** 32)
            a[k] = a[k] ^ t
            a[k + j] = a[k + j] ^ (t << j)
            k = (k + j + 1) & ~j
        j >>= 1
        m ^= (m << j) & 0xFFFFFFFF
    return a


def _rms(x, g):
    return x * lax.rsqrt(jnp.mean(x * x, axis=-1, keepdims=True) + EPS) * g


def _proj_kernel(x_ref, g_ref, wn_ref, wt_ref, cw_ref,
                 mix_ref, k_ref, ik_ref, qT_ref, iqT_ref, vT_ref, iwT_ref, ubuf_ref):
    i = pl.program_id(1)
    parts = [slice(j * TK, (j + 1) * TK) for j in range(TM1 // TK)]
    hs = [_rms(x_ref[0, r, :], g_ref[...]).astype(BF16) for r in parts]

    def nat(h, c0, c1):
        return jnp.dot(h, wn_ref[:, c0:c1], preferred_element_type=F32)

    def tr(h, r0, r1):
        return lax.dot_general(wt_ref[r0:r1, :], h, (((1,), (1,)), ((), ())),
                               preferred_element_type=F32)

    @pl.when(i == 0)
    def _():
        ubuf_ref[0:8, :] = jnp.zeros((8, D_CONV), F32)

    us = [nat(h, D_CONV, 2 * D_CONV) * nat(h, 2 * D_CONV, 3 * D_CONV) for h in hs]
    for r, u in zip(parts, us):
        ubuf_ref[8 + r.start:8 + r.stop, :] = u
    cw = cw_ref[...]
    for r, h, u in zip(parts, hs, us):
        u1 = ubuf_ref[7 + r.start:7 + r.stop, :]
        u2 = ubuf_ref[6 + r.start:6 + r.stop, :]
        y = cw[0:1, :] * u2 + cw[1:2, :] * u1 + cw[2:3, :] * u
        mix_ref[0, r, :] = (nat(h, 0, D_CONV) * y).astype(BF16)
    ubuf_ref[0:8, :] = us[-1][TK - 8:TK, :]

    for r, h in zip(parts, hs):
        kk = nat(h, 3 * D_CONV, 3 * D_CONV + HEAD_DIM + IDX_DIM)
        k_ref[0, r, :] = kk[:, 0:HEAD_DIM].astype(BF16)
        ik_ref[0, r, :] = kk[:, HEAD_DIM:HEAD_DIM + IDX_DIM].astype(BF16)
    for r, h in zip(parts, hs):
        qT_ref[0, :, r] = tr(h, 0, D_ATTN).astype(BF16)
    for r, h in zip(parts, hs):
        iqT_ref[0, :, r] = tr(h, D_ATTN, 2 * D_ATTN).astype(BF16)
    ones_row = (lax.broadcasted_iota(I32, (V_EXTRA, TK), 0) == 0).astype(F32).astype(BF16)
    for j, (r, h) in enumerate(zip(parts, hs)):
        vw = tr(h, 2 * D_ATTN, 2 * D_ATTN + HEAD_DIM + V_EXTRA)
        vT = vw[0:HEAD_DIM, :].astype(BF16)
        vT_ref[0, j] = jnp.concatenate([vT, ones_row], axis=0)
        iwT_ref[0, :, r] = vw[HEAD_DIM:HEAD_DIM + N_IDX_HEADS, :] * (
            N_IDX_HEADS ** -0.5 * IDX_DIM ** -0.5)


def _dsa_kernel(rb_ref, iqT_ref, iwT_ref, ik_ref, qT_ref, k_ref, vT_ref,
                o_ref, key_ref, planes_ref, alive_ref, add_ref, bias_ref, oT_ref, m_ref, l_ref, al_ref,
                lta_ref, ltb_ref, *, k_top):
    b = pl.program_id(0)
    qi = pl.program_id(1)
    nchunks = qi + 1
    nck = key_ref.shape[0]
    n_keys = nck * TK

    @pl.when((b == 0) & (qi == 0))
    def _():
        b0, steps = _bucket_steps()
        row = lax.broadcasted_iota(I32, (TK, TQ), 0)
        col = lax.broadcasted_iota(I32, (TK, TQ), 1)
        for off in range(2):
            dist = col - row + off * TK
            for hh in range(N_HEADS):
                val = jnp.full((TK, TQ), rb_ref[b0, hh], F32)
                for n0, bk in steps:
                    val = jnp.where(dist >= n0, rb_ref[bk, hh], val)
                bias_ref[off, hh] = (val - rb_ref[N_BUCKETS - 1, hh]) * LOG2E

    w = iwT_ref[0]
    sub = TK // N_IDX_HEADS

    def score_step(c1, c2, dst_ref, src_ref, diagonal=False):
        if c1 is not None:
            ikc = ik_ref[0, pl.ds(pl.multiple_of(c1 * TK, TK), TK), :]
        if c2 is not None:
            wb = [jnp.broadcast_to(w[hh:hh + 1, :], (8, TQ)) for hh in range(N_IDX_HEADS)]
            row0 = lax.broadcasted_iota(I32, (sub, TQ), 0)
            lane = lax.broadcasted_iota(I32, (sub, TQ), 1)
        for g in range(N_IDX_HEADS):
            if c1 is not None:
                dst_ref[g] = jnp.dot(ikc, iqT_ref[0, g * IDX_DIM:(g + 1) * IDX_DIM, :],
                                     preferred_element_type=F32)
            if c2 is not None:
                rows = slice(g * sub, (g + 1) * sub)
                def term(hh):
                    r = jnp.maximum(src_ref[hh, rows, :], 0.0).reshape(sub // 8, 8, TQ)
                    return wb[hh][None] * r
                acc = term(0)
                for hh in range(1, N_IDX_HEADS):
                    acc = acc + term(hh)
                acc = acc.reshape(sub, TQ)
                zero = acc == 0.0
                bits = pltpu.bitcast(acc, I32)
                key = bits ^ ((bits >> 31) & 0x7FFFFFFF)
                first = c2 * TK + g * sub
                key = jnp.where(zero, (n_keys - 1 - first) - row0, key)
                if diagonal:
                    key = jnp.where(row0 + (g * sub) <= lane, key, INT_MIN)
                key_ref[c2, rows, :] = key
        if c2 is not None:
            for l0 in range(0, TQ, LANES):
                planes = _bit_transpose32([key_ref[c2, 8 * v:8 * v + 8, l0:l0 + LANES]
                                           for v in range(TK // 8)])
                for bit in range(32):
                    plane = planes[31 - bit]
                    planes_ref[c2, bit, :, l0:l0 + LANES] = ~plane if bit == 31 else plane

    def run_pipeline(step, n_steps):
        step(0, None, lta_ref, None)

        def two_steps(j, carry):
            c = 2 * j
            step(c + 1, c, ltb_ref, lta_ref)
            step(c + 2, c + 1, lta_ref, ltb_ref)
            return carry

        lax.fori_loop(0, n_steps // 2, two_steps, 0)

        @pl.when(n_steps % 2 == 1)
        def _():
            step(n_steps, n_steps - 1, ltb_ref, lta_ref)
            step(None, n_steps, None, ltb_ref, diagonal=True)

        @pl.when(n_steps % 2 == 0)
        def _():
            step(None, n_steps, None, lta_ref, diagonal=True)

    run_pipeline(score_step, nchunks - 1)

    def count(pred):
        def body(c, acc):
            m = pred(key_ref[c], c).astype(I32)
            return acc + jnp.sum(m.reshape(TK // 8, 8, TQ), axis=0)
        acc = lax.fori_loop(0, nchunks, body, jnp.zeros((8, TQ), I32))
        return jnp.sum(acc, axis=0, keepdims=True)

    n_quads = (nchunks + QUAD - 1) // QUAD

    def init_alive(c, carry):
        alive_ref[c] = jnp.broadcast_to(jnp.where(c < nchunks, -1, 0).astype(I32), (8, TQ))
        return carry

    lax.fori_loop(0, n_quads * QUAD, init_alive, 0)

    def clear_padding(c, carry):
        planes_ref[c] = jnp.zeros((32, 8, TQ), I32)
        return carry

    lax.fori_loop(nchunks, n_quads * QUAD, clear_padding, 0)

    def select_pass(bit, state, first):
        k_rem, alive_n, thr_u, flip = state

        def quad(j, cnt):
            for i in range(QUAD):
                c = QUAD * j + i
                a = alive_ref[c]
                if not first:
                    a = a & (planes_ref[c, bit + 1] ^ flip)
                    alive_ref[c] = a
                cnt = cnt + lax.population_count(a & planes_ref[c, bit])
            return cnt

        ones = jnp.sum(lax.fori_loop(0, n_quads, quad, jnp.zeros((8, TQ), I32)),
                       axis=0, keepdims=True)
        take = ones >= k_rem
        return (jnp.where(take, k_rem, k_rem - ones), jnp.where(take, ones, alive_n - ones),
                thr_u | jnp.where(take, jnp.left_shift(jnp.int32(1), bit), 0),
                jnp.broadcast_to(jnp.where(take, 0, -1), (8, TQ)))

    state = (jnp.full((1, TQ), k_top, I32), jnp.full((1, TQ), nchunks * TK, I32),
             jnp.zeros((1, TQ), I32), jnp.zeros((8, TQ), I32))
    state = select_pass(31, state, first=True)
    k_rem, alive_n, thr_u, _ = lax.fori_loop(
        1, 32, lambda p, st: select_pass(31 - p, st, first=False), state)
    thr = thr_u ^ INT_MIN
    any_excess = jnp.max(((alive_n > k_rem) & (thr > INT_MIN)).astype(I32)) > 0
    thr = jnp.maximum(thr, INT_MIN + 1)

    @pl.when(jnp.logical_not(any_excess))
    def _():
        def mask_chunk(c, carry):
            add_ref[c] = jnp.where(key_ref[c] >= thr, 0.0, NEG).astype(F32)
            return carry
        lax.fori_loop(0, nchunks, mask_chunk, 0)

    @pl.when(any_excess)
    def _():
        room = k_top - count(lambda kc, c: kc > thr)

        def key_index(c):
            return lax.broadcasted_iota(I32, (TK, TQ), 0) + c * TK

        def step(it, lim):
            cand = lim + jnp.left_shift(jnp.int32(1), 12 - it)
            cnt = count(lambda kc, c: (kc == thr) & (key_index(c) < cand))
            return jnp.where(cnt <= room, cand, lim)

        tie_lim = lax.fori_loop(0, 13, step, jnp.zeros((1, TQ), I32))

        def mask_chunk(c, carry):
            kc = key_ref[c]
            sel = (kc > thr) | ((kc == thr) & (key_index(c) < tie_lim))
            add_ref[c] = jnp.where(sel, 0.0, NEG).astype(F32)
            return carry
        lax.fori_loop(0, nchunks, mask_chunk, 0)

    for off in range(2):
        @pl.when(qi >= off)
        def _():
            nm = add_ref[qi - off]
            for hh in range(N_HEADS):
                add_ref[nck + off * N_HEADS + hh] = nm + bias_ref[off, hh]

    def pipe_step(c1, c2, dst_ref, src_ref, diagonal=False):
        m_cur = m_ref[...]
        m_rows, l_rows = [], []
        if c1 is not None:
            kc = k_ref[0, pl.ds(pl.multiple_of(c1 * TK, TK), TK), :]
            near = c1 >= qi - 1
        if c2 is not None:
            vc = vT_ref[0, c2]
            alpha = al_ref[...]
        for i in range(N_HEADS + STAGE1_LEAD):
            if c1 is not None and i < N_HEADS:
                hh = i
                rows = slice(hh * HEAD_DIM, (hh + 1) * HEAD_DIM)
                tile = jnp.where(near, nck + (qi - c1) * N_HEADS + hh, c1)
                lt = jnp.dot(kc, qT_ref[0, rows, :], preferred_element_type=F32) + add_ref[tile]
                dst_ref[hh] = lt
                m_rows.append(jnp.max(lt, axis=0, keepdims=True))
                if hh == N_HEADS - 1:
                    m_new = jnp.maximum(m_cur, jnp.concatenate(m_rows, axis=0))
                    al_ref[...] = jnp.exp2(m_cur - m_new)
                    m_ref[...] = m_new
            if c2 is not None and i >= STAGE1_LEAD:
                hh = i - STAGE1_LEAD
                rows = slice(hh * HEAD_DIM, (hh + 1) * HEAD_DIM)
                p = jnp.exp2(src_ref[hh] - m_cur[hh:hh + 1, :]).astype(BF16)
                pv = jnp.dot(vc, p, preferred_element_type=F32)
                oT_ref[rows, :] = alpha[hh:hh + 1, :] * oT_ref[rows, :] + pv[0:HEAD_DIM, :]
                l_rows.append(pv[HEAD_DIM:HEAD_DIM + 1, :])
        if c2 is not None:
            l_ref[...] = alpha * l_ref[...] + jnp.concatenate(l_rows, axis=0)

    m_ref[...] = jnp.full((N_HEADS, TQ), NEG, F32)
    l_ref[...] = jnp.zeros((N_HEADS, TQ), F32)
    oT_ref[...] = jnp.zeros((D_ATTN, TQ), F32)
    run_pipeline(pipe_step, nchunks - 1)
    for hh in range(N_HEADS):
        rows = slice(hh * HEAD_DIM, (hh + 1) * HEAD_DIM)
        oT_ref[rows, :] = oT_ref[rows, :] / l_ref[hh:hh + 1, :]

    o_ref[0] = oT_ref[...].T.astype(BF16)


def _merge_kernel(x_ref, mix_ref, att_ref, gmix_ref, wg_ref, bg_ref, wa_ref, wb_ref, wo_ref,
                  gffn_ref, wfg_ref, wfu_ref, wfd_ref, gfin_ref, o_ref, *, final_norm):
    parts = [slice(j * (TM3 // 2), (j + 1) * (TM3 // 2)) for j in range(2)]

    def dot(a, w_ref):
        return jnp.dot(a, w_ref[...], preferred_element_type=F32)

    xs = [x_ref[0, r, :] for r in parts]
    hs = [_rms(x, gmix_ref[...]).astype(BF16) for x in xs]
    gs = [jax.nn.sigmoid(dot(h, wg_ref) + bg_ref[...]) for h in hs]
    yas = [dot(mix_ref[0, r, :], wa_ref) for r in parts]
    ybs = [dot(att_ref[0, r, :], wb_ref) for r in parts]
    ms = [(g[:, :D_MODEL] * ya + g[:, D_MODEL:] * yb).astype(BF16) for g, ya, yb in zip(gs, yas, ybs)]
    x1s = [x + dot(m, wo_ref) for x, m in zip(xs, ms)]
    h2s = [_rms(x1, gffn_ref[...]).astype(BF16) for x1 in x1s]
    fgs = [dot(h2, wfg_ref) for h2 in h2s]
    fus = [dot(h2, wfu_ref) for h2 in h2s]
    acts = [(jax.nn.silu(fg) * fu).astype(BF16) for fg, fu in zip(fgs, fus)]
    x2s = [x1 + dot(a, wfd_ref) for x1, a in zip(x1s, acts)]
    for r, x2 in zip(parts, x2s):
        o_ref[0, r, :] = _rms(x2, gfin_ref[...]) if final_norm else x2


def _const_spec(shape):
    nd = len(shape)
    return pl.BlockSpec(shape, lambda *_: (0,) * nd, pipeline_mode=pl.Buffered(1))


def kernel(x, g_mix, w_in, b_gate, conv_w, w_branch_a, w_branch_b, w_out, rel_bias, g_ffn,
           w_ffn_gate, w_ffn_up, w_ffn_down, g_final):
    bsz, seq, d = x.shape
    assert d == D_MODEL and seq % TM1 == 0 and seq % TQ == 0 and seq % TM3 == 0 and TQ == TK
    k_top = min(TOPK_MAX, seq // 4)
    assert TK == 32 * 8 and (seq // TK) % QUAD == 0
    assert seq < 2 ** 23
    nck = seq // TK
    d_ff = w_ffn_gate.shape[-1]
    cparams = functools.partial(pltpu.CompilerParams, vmem_limit_bytes=VMEM_LIMIT)

    for layer in range(g_mix.shape[0]):
        w = w_in[layer]
        o = np.cumsum([0, D_CONV, D_CONV, D_CONV, D_ATTN, HEAD_DIM, HEAD_DIM,
                       N_IDX_HEADS * IDX_DIM, IDX_DIM, N_IDX_HEADS, D_MODEL, D_MODEL])
        w_cb, w_cc, w_cx, w_q, w_k, w_v, w_iq, w_ik, w_iw, w_ga, w_gb = [
            w[:, int(o[j]):int(o[j + 1])] for j in range(11)]
        w_nat = jnp.concatenate([w_cb, w_cc, w_cx, w_k, w_ik], axis=1).astype(BF16)
        w_pad = jnp.zeros((D_MODEL, V_EXTRA - N_IDX_HEADS), w.dtype)
        w_tr = jnp.concatenate([w_q * (HEAD_DIM ** -0.5 * LOG2E), w_iq, w_v, w_iw, w_pad],
                               axis=1).T.astype(BF16)
        w_g = jnp.concatenate([w_ga, w_gb], axis=1).astype(BF16)
        n_nat, n_tr = w_nat.shape[1], w_tr.shape[0]

        mix, k_n, ik_n, qT, iqT, vT, iwT = pl.pallas_call(
            _proj_kernel,
            grid=(bsz, seq // TM1),
            in_specs=[
                pl.BlockSpec((1, TM1, D_MODEL), lambda b, i: (b, i, 0)),
                _const_spec((1, D_MODEL)),
                _const_spec((D_MODEL, n_nat)),
                _const_spec((n_tr, D_MODEL)),
                _const_spec((CONV_K, D_CONV)),
            ],
            out_specs=[
                pl.BlockSpec((1, TM1, D_CONV), lambda b, i: (b, i, 0)),
                pl.BlockSpec((1, TM1, HEAD_DIM), lambda b, i: (b, i, 0)),
                pl.BlockSpec((1, TM1, IDX_DIM), lambda b, i: (b, i, 0)),
                pl.BlockSpec((1, D_ATTN, TM1), lambda b, i: (b, 0, i)),
                pl.BlockSpec((1, N_IDX_HEADS * IDX_DIM, TM1), lambda b, i: (b, 0, i)),
                pl.BlockSpec((1, TM1 // TK, HEAD_DIM + V_EXTRA, TK), lambda b, i: (b, i, 0, 0)),
                pl.BlockSpec((1, N_IDX_HEADS, TM1), lambda b, i: (b, 0, i)),
            ],
            out_shape=[
                jax.ShapeDtypeStruct((bsz, seq, D_CONV), BF16),
                jax.ShapeDtypeStruct((bsz, seq, HEAD_DIM), BF16),
                jax.ShapeDtypeStruct((bsz, seq, IDX_DIM), BF16),
                jax.ShapeDtypeStruct((bsz, D_ATTN, seq), BF16),
                jax.ShapeDtypeStruct((bsz, N_IDX_HEADS * IDX_DIM, seq), BF16),
                jax.ShapeDtypeStruct((bsz, nck, HEAD_DIM + V_EXTRA, TK), BF16),
                jax.ShapeDtypeStruct((bsz, N_IDX_HEADS, seq), F32),
            ],
            scratch_shapes=[pltpu.VMEM((TM1 + 8, D_CONV), F32)],
            compiler_params=cparams(dimension_semantics=("arbitrary", "arbitrary")),
            name="proj",
        )(x, g_mix[layer][None, :], w_nat, w_tr, conv_w[layer])

        attn = pl.pallas_call(
            functools.partial(_dsa_kernel, k_top=k_top),
            grid=(bsz, seq // TQ),
            in_specs=[
                pl.BlockSpec(memory_space=pltpu.SMEM),
                pl.BlockSpec((1, N_IDX_HEADS * IDX_DIM, TQ), lambda b, i: (b, 0, i)),
                pl.BlockSpec((1, N_IDX_HEADS, TQ), lambda b, i: (b, 0, i)),
                pl.BlockSpec((1, seq, IDX_DIM), lambda b, i: (b, 0, 0)),
                pl.BlockSpec((1, D_ATTN, TQ), lambda b, i: (b, 0, i)),
                pl.BlockSpec((1, seq, HEAD_DIM), lambda b, i: (b, 0, 0)),
                pl.BlockSpec((1, nck, HEAD_DIM + V_EXTRA, TK), lambda b, i: (b, 0, 0, 0)),
            ],
            out_specs=pl.BlockSpec((1, TQ, D_ATTN), lambda b, i: (b, i, 0)),
            out_shape=jax.ShapeDtypeStruct((bsz, seq, D_ATTN), BF16),
            scratch_shapes=[
                pltpu.VMEM((nck, TK, TQ), I32),
                pltpu.VMEM((nck, 32, 8, TQ), I32),
                pltpu.VMEM((nck, 8, TQ), I32),
                pltpu.VMEM((nck + 2 * N_HEADS, TK, TQ), F32),
                pltpu.VMEM((2, N_HEADS, TK, TQ), F32),
                pltpu.VMEM((D_ATTN, TQ), F32),
                pltpu.VMEM((N_HEADS, TQ), F32),
                pltpu.VMEM((N_HEADS, TQ), F32),
                pltpu.VMEM((N_HEADS, TQ), F32),
                pltpu.VMEM((N_HEADS, TK, TQ), F32),
                pltpu.VMEM((N_HEADS, TK, TQ), F32),
            ],
            compiler_params=cparams(dimension_semantics=("arbitrary", "arbitrary")),
            name="dsa",
        )(rel_bias, iqT, iwT, ik_n, qT, k_n, vT)

        x = pl.pallas_call(
            functools.partial(_merge_kernel, final_norm=layer == g_mix.shape[0] - 1),
            grid=(bsz, seq // TM3),
            in_specs=[
                pl.BlockSpec((1, TM3, D_MODEL), lambda b, i: (b, i, 0)),
                pl.BlockSpec((1, TM3, D_CONV), lambda b, i: (b, i, 0)),
                pl.BlockSpec((1, TM3, D_ATTN), lambda b, i: (b, i, 0)),
                _const_spec((1, D_MODEL)),
                _const_spec((D_MODEL, 2 * D_MODEL)),
                _const_spec((1, 2 * D_MODEL)),
                _const_spec((D_CONV, D_MODEL)),
                _const_spec((D_ATTN, D_MODEL)),
                _const_spec((D_MODEL, D_MODEL)),
                _const_spec((1, D_MODEL)),
                _const_spec((D_MODEL, d_ff)),
                _const_spec((D_MODEL, d_ff)),
                _const_spec((d_ff, D_MODEL)),
                _const_spec((1, D_MODEL)),
            ],
            out_specs=pl.BlockSpec((1, TM3, D_MODEL), lambda b, i: (b, i, 0)),
            out_shape=jax.ShapeDtypeStruct((bsz, seq, D_MODEL), F32),
            compiler_params=cparams(dimension_semantics=("arbitrary", "arbitrary")),
            name="merge",
        )(x, mix, attn, g_mix[layer][None, :], w_g, b_gate[layer][None, :],
          w_branch_a[layer].astype(BF16), w_branch_b[layer].astype(BF16),
          w_out[layer].astype(BF16), g_ffn[layer][None, :],
          w_ffn_gate[layer].astype(BF16), w_ffn_up[layer].astype(BF16),
          w_ffn_down[layer].astype(BF16),
          g_final[None, :])
    return x
```

```python
import functools
import math

import numpy as np
import jax
import jax.numpy as jnp
from jax import lax
from jax.experimental import pallas as pl
from jax.experimental.pallas import tpu as pltpu

D_MODEL = 1024
D_CONV = 512
CONV_K = 3
N_HEADS = 8
HEAD_DIM = 64
D_ATTN = N_HEADS * HEAD_DIM
N_IDX_HEADS = 8
IDX_DIM = 64
TOPK_MAX = 256
N_BUCKETS = 32
MAX_EXACT = 16
MAX_DISTANCE = 128
EPS = 1e-6

F32 = jnp.float32
BF16 = jnp.bfloat16
I32 = jnp.int32

INT_MIN = -(2 ** 31)
NEG = -1e30
LOG2E = math.log2(math.e)

TM1 = 512
TQ = 256
TK = 256
TM3 = 256
V_EXTRA = 16
QUAD = 4
LANES = 128
STAGE1_LEAD = 3
VMEM_LIMIT = 56 * 1024 * 1024


def _bucket_steps():
    n = np.arange(2 * MAX_DISTANCE)
    large = MAX_EXACT + (np.log(np.maximum(n, 1).astype(np.float32) / MAX_EXACT)
                         / math.log(MAX_DISTANCE / MAX_EXACT)
                         * (N_BUCKETS - MAX_EXACT)).astype(np.int32)
    large = np.minimum(large, N_BUCKETS - 1)
    b = np.where(n < MAX_EXACT, n, large)
    assert np.all(b[MAX_DISTANCE:] == N_BUCKETS - 1) and np.all(np.diff(b) >= 0)
    steps = [(int(i), int(b[i])) for i in range(1, len(b)) if b[i] != b[i - 1]]
    return int(b[0]), steps


def _fold_rows(a, rows):
    parts = [a[i:i + rows, :] for i in range(0, a.shape[0], rows)]
    while len(parts) > 1:
        parts = [parts[i] + parts[i + 1] for i in range(0, len(parts) - 1, 2)] + (
            [parts[-1]] if len(parts) % 2 else [])
    return parts[0]


def _bit_transpose32(a):
    a = list(a)
    j, m = 16, 0x0000FFFF
    while j:
        k = 0
        while k < 32:
            t = (a[k] ^ (a[k + j] >> j)) & (m if m < 2 ** 31 else m - 2 ** 32)
            a[k] = a[k] ^ t
            a[k + j] = a[k + j] ^ (t << j)
            k = (k + j + 1) & ~j
        j >>= 1
        m ^= (m << j) & 0xFFFFFFFF
    return a


def _rms(x, g):
    return x * lax.rsqrt(jnp.mean(x * x, axis=-1, keepdims=True) + EPS) * g


def _proj_kernel(x_ref, g_ref, wn_ref, wt_ref, cw_ref,
                 mix_ref, k_ref, ik_ref, qT_ref, iqT_ref, vT_ref, iwT_ref, ubuf_ref):
    i = pl.program_id(1)
    parts = [slice(j * TK, (j + 1) * TK) for j in range(TM1 // TK)]
    hs = [_rms(x_ref[0, r, :], g_ref[...]).astype(BF16) for r in parts]

    def nat(h, c0, c1):
        return jnp.dot(h, wn_ref[:, c0:c1], preferred_element_type=F32)

    def tr(h, r0, r1):
        return lax.dot_general(wt_ref[r0:r1, :], h, (((1,), (1,)), ((), ())),
                               preferred_element_type=F32)

    @pl.when(i == 0)
    def _():
        ubuf_ref[0:8, :] = jnp.zeros((8, D_CONV), F32)

    us = [nat(h, D_CONV, 2 * D_CONV) * nat(h, 2 * D_CONV, 3 * D_CONV) for h in hs]
    for r, u in zip(parts, us):
        ubuf_ref[8 + r.start:8 + r.stop, :] = u
    cw = cw_ref[...]
    for r, h, u in zip(parts, hs, us):
        u1 = ubuf_ref[7 + r.start:7 + r.stop, :]
        u2 = ubuf_ref[6 + r.start:6 + r.stop, :]
        y = cw[0:1, :] * u2 + cw[1:2, :] * u1 + cw[2:3, :] * u
        mix_ref[0, r, :] = (nat(h, 0, D_CONV) * y).astype(BF16)
    ubuf_ref[0:8, :] = us[-1][TK - 8:TK, :]

    for r, h in zip(parts, hs):
        kk = nat(h, 3 * D_CONV, 3 * D_CONV + HEAD_DIM + IDX_DIM)
        k_ref[0, r, :] = kk[:, 0:HEAD_DIM].astype(BF16)
        ik_ref[0, r, :] = kk[:, HEAD_DIM:HEAD_DIM + IDX_DIM].astype(BF16)
    for r, h in zip(parts, hs):
        qT_ref[0, :, r] = tr(h, 0, D_ATTN).astype(BF16)
    for r, h in zip(parts, hs):
        iqT_ref[0, :, r] = tr(h, D_ATTN, 2 * D_ATTN).astype(BF16)
    ones_row = (lax.broadcasted_iota(I32, (V_EXTRA, TK), 0) == 0).astype(F32).astype(BF16)
    for j, (r, h) in enumerate(zip(parts, hs)):
        vw = tr(h, 2 * D_ATTN, 2 * D_ATTN + HEAD_DIM + V_EXTRA)
        vT = vw[0:HEAD_DIM, :].astype(BF16)
        vT_ref[0, j] = jnp.concatenate([vT, ones_row], axis=0)
        iwT_ref[0, :, r] = vw[HEAD_DIM:HEAD_DIM + N_IDX_HEADS, :] * (
            N_IDX_HEADS ** -0.5 * IDX_DIM ** -0.5)


def _dsa_kernel(rb_ref, iqT_ref, iwT_ref, ik_ref, qT_ref, k_ref, vT_ref,
                o_ref, key_ref, planes_ref, alive_ref, add_ref, bias_ref, oT_ref, m_ref, l_ref, al_ref,
                lta_ref, ltb_ref, *, k_top):
    b = pl.program_id(0)
    qi = pl.program_id(1)
    nchunks = qi + 1
    nck = key_ref.shape[0]
    n_keys = nck * TK

    @pl.when((b == 0) & (qi == 0))
    def _():
        b0, steps = _bucket_steps()
        row = lax.broadcasted_iota(I32, (TK, TQ), 0)
        col = lax.broadcasted_iota(I32, (TK, TQ), 1)
        for off in range(2):
            dist = col - row + off * TK
            for hh in range(N_HEADS):
                val = jnp.full((TK, TQ), rb_ref[b0, hh], F32)
                for n0, bk in steps:
                    val = jnp.where(dist >= n0, rb_ref[bk, hh], val)
                bias_ref[off, hh] = (val - rb_ref[N_BUCKETS - 1, hh]) * LOG2E

    w = iwT_ref[0]
    sub = TK // N_IDX_HEADS

    def score_step(c1, c2, dst_ref, src_ref, diagonal=False):
        if c1 is not None:
            ikc = ik_ref[0, pl.ds(pl.multiple_of(c1 * TK, TK), TK), :]
        if c2 is not None:
            wb = [jnp.broadcast_to(w[hh:hh + 1, :], (8, TQ)) for hh in range(N_IDX_HEADS)]
            row0 = lax.broadcasted_iota(I32, (sub, TQ), 0)
            lane = lax.broadcasted_iota(I32, (sub, TQ), 1)
        for g in range(N_IDX_HEADS):
            if c1 is not None:
                dst_ref[g] = jnp.dot(ikc, iqT_ref[0, g * IDX_DIM:(g + 1) * IDX_DIM, :],
                                     preferred_element_type=F32)
            if c2 is not None:
                rows = slice(g * sub, (g + 1) * sub)
                def term(hh):
                    r = jnp.maximum(src_ref[hh, rows, :], 0.0).reshape(sub // 8, 8, TQ)
                    return wb[hh][None] * r
                acc = term(0)
                for hh in range(1, N_IDX_HEADS):
                    acc = acc + term(hh)
                acc = acc.reshape(sub, TQ)
                zero = acc == 0.0
                bits = pltpu.bitcast(acc, I32)
                key = bits ^ ((bits >> 31) & 0x7FFFFFFF)
                first = c2 * TK + g * sub
                key = jnp.where(zero, (n_keys - 1 - first) - row0, key)
                if diagonal:
                    key = jnp.where(row0 + (g * sub) <= lane, key, INT_MIN)
                key_ref[c2, rows, :] = key
        if c2 is not None:
            for l0 in range(0, TQ, LANES):
                planes = _bit_transpose32([key_ref[c2, 8 * v:8 * v + 8, l0:l0 + LANES]
                                           for v in range(TK // 8)])
                for bit in range(32):
                    plane = planes[31 - bit]
                    planes_ref[c2, bit, :, l0:l0 + LANES] = ~plane if bit == 31 else plane

    def run_pipeline(step, n_steps):
        step(0, None, lta_ref, None)

        def two_steps(j, carry):
            c = 2 * j
            step(c + 1, c, ltb_ref, lta_ref)
            step(c + 2, c + 1, lta_ref, ltb_ref)
            return carry

        lax.fori_loop(0, n_steps // 2, two_steps, 0)

        @pl.when(n_steps % 2 == 1)
        def _():
            step(n_steps, n_steps - 1, ltb_ref, lta_ref)
            step(None, n_steps, None, ltb_ref, diagonal=True)

        @pl.when(n_steps % 2 == 0)
        def _():
            step(None, n_steps, None, lta_ref, diagonal=True)

    run_pipeline(score_step, nchunks - 1)

    def count(pred):
        def body(c, acc):
            m = pred(key_ref[c], c).astype(I32)
            return acc + jnp.sum(m.reshape(TK // 8, 8, TQ), axis=0)
        acc = lax.fori_loop(0, nchunks, body, jnp.zeros((8, TQ), I32))
        return jnp.sum(acc, axis=0, keepdims=True)

    n_quads = (nchunks + QUAD - 1) // QUAD

    def init_alive(c, carry):
        alive_ref[c] = jnp.broadcast_to(jnp.where(c < nchunks, -1, 0).astype(I32), (8, TQ))
        return carry

    lax.fori_loop(0, n_quads * QUAD, init_alive, 0)

    def clear_padding(c, carry):
        planes_ref[c] = jnp.zeros((32, 8, TQ), I32)
        return carry

    lax.fori_loop(nchunks, n_quads * QUAD, clear_padding, 0)

    def select_pass(bit, state, first):
        k_rem, alive_n, thr_u, flip_hi, flip_lo = state

        def quad(j, cnts):
            n11, n1x, n01 = cnts
            for i in range(QUAD):
                c = QUAD * j + i
                a = alive_ref[c]
                if not first:
                    a = a & (planes_ref[c, bit + 2] ^ flip_hi) & (planes_ref[c, bit + 1] ^ flip_lo)
                    alive_ref[c] = a
                hi = a & planes_ref[c, bit]
                lo = planes_ref[c, bit - 1]
                n11 = n11 + lax.population_count(hi & lo)
                n1x = n1x + lax.population_count(hi)
                n01 = n01 + lax.population_count((a ^ hi) & lo)
            return n11, n1x, n01

        zero = jnp.zeros((8, TQ), I32)
        n11, n1x, n01 = [jnp.sum(n, axis=0, keepdims=True)
                         for n in lax.fori_loop(0, n_quads, quad, (zero, zero, zero))]
        ge3, ge2, ge1 = n11, n1x, n1x + n01
        d3, d2, d1 = ge3 >= k_rem, ge2 >= k_rem, ge1 >= k_rem
        digit = d3.astype(I32) + d2.astype(I32) + d1.astype(I32)
        above = jnp.where(d3, 0, jnp.where(d2, ge3, jnp.where(d1, ge2, ge1)))
        same = jnp.where(d3, ge3, jnp.where(d2, ge2 - ge3, jnp.where(d1, ge1 - ge2, alive_n - ge1)))
        return (k_rem - above, same, thr_u | jnp.left_shift(digit, bit - 1),
                jnp.broadcast_to(jnp.where(digit >= 2, 0, -1), (8, TQ)),
                jnp.broadcast_to(jnp.where((digit & 1) == 1, 0, -1), (8, TQ)))

    state = (jnp.full((1, TQ), k_top, I32), jnp.full((1, TQ), nchunks * TK, I32),
             jnp.zeros((1, TQ), I32), jnp.zeros((8, TQ), I32), jnp.zeros((8, TQ), I32))
    state = select_pass(31, state, first=True)
    k_rem, alive_n, thr_u, _, _ = lax.fori_loop(
        1, 16, lambda p, st: select_pass(31 - 2 * p, st, first=False), state)
    thr = thr_u ^ INT_MIN
    any_excess = jnp.max(((alive_n > k_rem) & (thr > INT_MIN)).astype(I32)) > 0
    thr = jnp.maximum(thr, INT_MIN + 1)

    @pl.when(jnp.logical_not(any_excess))
    def _():
        def mask_chunk(c, carry):
            add_ref[c] = jnp.where(key_ref[c] >= thr, 0.0, NEG).astype(F32)
            return carry
        lax.fori_loop(0, nchunks, mask_chunk, 0)

    @pl.when(any_excess)
    def _():
        room = k_top - count(lambda kc, c: kc > thr)

        def key_index(c):
            return lax.broadcasted_iota(I32, (TK, TQ), 0) + c * TK

        def step(it, lim):
            cand = lim + jnp.left_shift(jnp.int32(1), 12 - it)
            cnt = count(lambda kc, c: (kc == thr) & (key_index(c) < cand))
            return jnp.where(cnt <= room, cand, lim)

        tie_lim = lax.fori_loop(0, 13, step, jnp.zeros((1, TQ), I32))

        def mask_chunk(c, carry):
            kc = key_ref[c]
            sel = (kc > thr) | ((kc == thr) & (key_index(c) < tie_lim))
            add_ref[c] = jnp.where(sel, 0.0, NEG).astype(F32)
            return carry
        lax.fori_loop(0, nchunks, mask_chunk, 0)

    for off in range(2):
        @pl.when(qi >= off)
        def _():
            nm = add_ref[qi - off]
            for hh in range(N_HEADS):
                add_ref[nck + off * N_HEADS + hh] = nm + bias_ref[off, hh]

    def pipe_step(c1, c2, dst_ref, src_ref, diagonal=False):
        m_cur = m_ref[...]
        m_rows, l_rows = [], []
        if c1 is not None:
            kc = k_ref[0, pl.ds(pl.multiple_of(c1 * TK, TK), TK), :]
            near = c1 >= qi - 1
        if c2 is not None:
            vc = vT_ref[0, c2]
            alpha = al_ref[...]
        for i in range(N_HEADS + STAGE1_LEAD):
            if c1 is not None and i < N_HEADS:
                hh = i
                rows = slice(hh * HEAD_DIM, (hh + 1) * HEAD_DIM)
                tile = jnp.where(near, nck + (qi - c1) * N_HEADS + hh, c1)
                lt = jnp.dot(kc, qT_ref[0, rows, :], preferred_element_type=F32) + add_ref[tile]
                dst_ref[hh] = lt
                m_rows.append(jnp.max(lt, axis=0, keepdims=True))
                if hh == N_HEADS - 1:
                    m_new = jnp.maximum(m_cur, jnp.concatenate(m_rows, axis=0))
                    al_ref[...] = jnp.exp2(m_cur - m_new)
                    m_ref[...] = m_new
            if c2 is not None and i >= STAGE1_LEAD:
                hh = i - STAGE1_LEAD
                rows = slice(hh * HEAD_DIM, (hh + 1) * HEAD_DIM)
                p = jnp.exp2(src_ref[hh] - m_cur[hh:hh + 1, :]).astype(BF16)
                pv = jnp.dot(vc, p, preferred_element_type=F32)
                oT_ref[rows, :] = alpha[hh:hh + 1, :] * oT_ref[rows, :] + pv[0:HEAD_DIM, :]
                l_rows.append(pv[HEAD_DIM:HEAD_DIM + 1, :])
        if c2 is not None:
            l_ref[...] = alpha * l_ref[...] + jnp.concatenate(l_rows, axis=0)

    m_ref[...] = jnp.full((N_HEADS, TQ), NEG, F32)
    l_ref[...] = jnp.zeros((N_HEADS, TQ), F32)
    oT_ref[...] = jnp.zeros((D_ATTN, TQ), F32)
    run_pipeline(pipe_step, nchunks - 1)
    for hh in range(N_HEADS):
        rows = slice(hh * HEAD_DIM, (hh + 1) * HEAD_DIM)
        oT_ref[rows, :] = oT_ref[rows, :] / l_ref[hh:hh + 1, :]

    o_ref[0] = oT_ref[...].T.astype(BF16)


def _merge_kernel(x_ref, mix_ref, att_ref, gmix_ref, wg_ref, bg_ref, wa_ref, wb_ref, wo_ref,
                  gffn_ref, wfg_ref, wfu_ref, wfd_ref, gfin_ref, o_ref, *, final_norm):
    parts = [slice(j * (TM3 // 2), (j + 1) * (TM3 // 2)) for j in range(2)]

    def dot(a, w_ref):
        return jnp.dot(a, w_ref[...], preferred_element_type=F32)

    xs = [x_ref[0, r, :] for r in parts]
    hs = [_rms(x, gmix_ref[...]).astype(BF16) for x in xs]
    gs = [jax.nn.sigmoid(dot(h, wg_ref) + bg_ref[...]) for h in hs]
    yas = [dot(mix_ref[0, r, :], wa_ref) for r in parts]
    ybs = [dot(att_ref[0, r, :], wb_ref) for r in parts]
    ms = [(g[:, :D_MODEL] * ya + g[:, D_MODEL:] * yb).astype(BF16) for g, ya, yb in zip(gs, yas, ybs)]
    x1s = [x + dot(m, wo_ref) for x, m in zip(xs, ms)]
    h2s = [_rms(x1, gffn_ref[...]).astype(BF16) for x1 in x1s]
    fgs = [dot(h2, wfg_ref) for h2 in h2s]
    fus = [dot(h2, wfu_ref) for h2 in h2s]
    acts = [(jax.nn.silu(fg) * fu).astype(BF16) for fg, fu in zip(fgs, fus)]
    x2s = [x1 + dot(a, wfd_ref) for x1, a in zip(x1s, acts)]
    for r, x2 in zip(parts, x2s):
        o_ref[0, r, :] = _rms(x2, gfin_ref[...]) if final_norm else x2


def _const_spec(shape):
    nd = len(shape)
    return pl.BlockSpec(shape, lambda *_: (0,) * nd, pipeline_mode=pl.Buffered(1))


def kernel(x, g_mix, w_in, b_gate, conv_w, w_branch_a, w_branch_b, w_out, rel_bias, g_ffn,
           w_ffn_gate, w_ffn_up, w_ffn_down, g_final):
    bsz, seq, d = x.shape
    assert d == D_MODEL and seq % TM1 == 0 and seq % TQ == 0 and seq % TM3 == 0 and TQ == TK
    k_top = min(TOPK_MAX, seq // 4)
    assert TK == 32 * 8 and (seq // TK) % QUAD == 0
    assert seq < 2 ** 23
    nck = seq // TK
    d_ff = w_ffn_gate.shape[-1]
    cparams = functools.partial(pltpu.CompilerParams, vmem_limit_bytes=VMEM_LIMIT)

    for layer in range(g_mix.shape[0]):
        w = w_in[layer]
        o = np.cumsum([0, D_CONV, D_CONV, D_CONV, D_ATTN, HEAD_DIM, HEAD_DIM,
                       N_IDX_HEADS * IDX_DIM, IDX_DIM, N_IDX_HEADS, D_MODEL, D_MODEL])
        w_cb, w_cc, w_cx, w_q, w_k, w_v, w_iq, w_ik, w_iw, w_ga, w_gb = [
            w[:, int(o[j]):int(o[j + 1])] for j in range(11)]
        w_nat = jnp.concatenate([w_cb, w_cc, w_cx, w_k, w_ik], axis=1).astype(BF16)
        w_pad = jnp.zeros((D_MODEL, V_EXTRA - N_IDX_HEADS), w.dtype)
        w_tr = jnp.concatenate([w_q * (HEAD_DIM ** -0.5 * LOG2E), w_iq, w_v, w_iw, w_pad],
                               axis=1).T.astype(BF16)
        w_g = jnp.concatenate([w_ga, w_gb], axis=1).astype(BF16)
        n_nat, n_tr = w_nat.shape[1], w_tr.shape[0]

        mix, k_n, ik_n, qT, iqT, vT, iwT = pl.pallas_call(
            _proj_kernel,
            grid=(bsz, seq // TM1),
            in_specs=[
                pl.BlockSpec((1, TM1, D_MODEL), lambda b, i: (b, i, 0)),
                _const_spec((1, D_MODEL)),
                _const_spec((D_MODEL, n_nat)),
                _const_spec((n_tr, D_MODEL)),
                _const_spec((CONV_K, D_CONV)),
            ],
            out_specs=[
                pl.BlockSpec((1, TM1, D_CONV), lambda b, i: (b, i, 0)),
                pl.BlockSpec((1, TM1, HEAD_DIM), lambda b, i: (b, i, 0)),
                pl.BlockSpec((1, TM1, IDX_DIM), lambda b, i: (b, i, 0)),
                pl.BlockSpec((1, D_ATTN, TM1), lambda b, i: (b, 0, i)),
                pl.BlockSpec((1, N_IDX_HEADS * IDX_DIM, TM1), lambda b, i: (b, 0, i)),
                pl.BlockSpec((1, TM1 // TK, HEAD_DIM + V_EXTRA, TK), lambda b, i: (b, i, 0, 0)),
                pl.BlockSpec((1, N_IDX_HEADS, TM1), lambda b, i: (b, 0, i)),
            ],
            out_shape=[
                jax.ShapeDtypeStruct((bsz, seq, D_CONV), BF16),
                jax.ShapeDtypeStruct((bsz, seq, HEAD_DIM), BF16),
                jax.ShapeDtypeStruct((bsz, seq, IDX_DIM), BF16),
                jax.ShapeDtypeStruct((bsz, D_ATTN, seq), BF16),
                jax.ShapeDtypeStruct((bsz, N_IDX_HEADS * IDX_DIM, seq), BF16),
                jax.ShapeDtypeStruct((bsz, nck, HEAD_DIM + V_EXTRA, TK), BF16),
                jax.ShapeDtypeStruct((bsz, N_IDX_HEADS, seq), F32),
            ],
            scratch_shapes=[pltpu.VMEM((TM1 + 8, D_CONV), F32)],
            compiler_params=cparams(dimension_semantics=("arbitrary", "arbitrary")),
            name="proj",
        )(x, g_mix[layer][None, :], w_nat, w_tr, conv_w[layer])

        attn = pl.pallas_call(
            functools.partial(_dsa_kernel, k_top=k_top),
            grid=(bsz, seq // TQ),
            in_specs=[
                pl.BlockSpec(memory_space=pltpu.SMEM),
                pl.BlockSpec((1, N_IDX_HEADS * IDX_DIM, TQ), lambda b, i: (b, 0, i)),
                pl.BlockSpec((1, N_IDX_HEADS, TQ), lambda b, i: (b, 0, i)),
                pl.BlockSpec((1, seq, IDX_DIM), lambda b, i: (b, 0, 0)),
                pl.BlockSpec((1, D_ATTN, TQ), lambda b, i: (b, 0, i)),
                pl.BlockSpec((1, seq, HEAD_DIM), lambda b, i: (b, 0, 0)),
                pl.BlockSpec((1, nck, HEAD_DIM + V_EXTRA, TK), lambda b, i: (b, 0, 0, 0)),
            ],
            out_specs=pl.BlockSpec((1, TQ, D_ATTN), lambda b, i: (b, i, 0)),
            out_shape=jax.ShapeDtypeStruct((bsz, seq, D_ATTN), BF16),
            scratch_shapes=[
                pltpu.VMEM((nck, TK, TQ), I32),
                pltpu.VMEM((nck, 32, 8, TQ), I32),
                pltpu.VMEM((nck, 8, TQ), I32),
                pltpu.VMEM((nck + 2 * N_HEADS, TK, TQ), F32),
                pltpu.VMEM((2, N_HEADS, TK, TQ), F32),
                pltpu.VMEM((D_ATTN, TQ), F32),
                pltpu.VMEM((N_HEADS, TQ), F32),
                pltpu.VMEM((N_HEADS, TQ), F32),
                pltpu.VMEM((N_HEADS, TQ), F32),
                pltpu.VMEM((N_HEADS, TK, TQ), F32),
                pltpu.VMEM((N_HEADS, TK, TQ), F32),
            ],
            compiler_params=cparams(dimension_semantics=("arbitrary", "arbitrary")),
            name="dsa",
        )(rel_bias, iqT, iwT, ik_n, qT, k_n, vT)

        x = pl.pallas_call(
            functools.partial(_merge_kernel, final_norm=layer == g_mix.shape[0] - 1),
            grid=(bsz, seq // TM3),
            in_specs=[
                pl.BlockSpec((1, TM3, D_MODEL), lambda b, i: (b, i, 0)),
                pl.BlockSpec((1, TM3, D_CONV), lambda b, i: (b, i, 0)),
                pl.BlockSpec((1, TM3, D_ATTN), lambda b, i: (b, i, 0)),
                _const_spec((1, D_MODEL)),
                _const_spec((D_MODEL, 2 * D_MODEL)),
                _const_spec((1, 2 * D_MODEL)),
                _const_spec((D_CONV, D_MODEL)),
                _const_spec((D_ATTN, D_MODEL)),
                _const_spec((D_MODEL, D_MODEL)),
                _const_spec((1, D_MODEL)),
                _const_spec((D_MODEL, d_ff)),
                _const_spec((D_MODEL, d_ff)),
                _const_spec((d_ff, D_MODEL)),
                _const_spec((1, D_MODEL)),
            ],
            out_specs=pl.BlockSpec((1, TM3, D_MODEL), lambda b, i: (b, i, 0)),
            out_shape=jax.ShapeDtypeStruct((bsz, seq, D_MODEL), F32),
            compiler_params=cparams(dimension_semantics=("arbitrary", "arbitrary")),
            name="merge",
        )(x, mix, attn, g_mix[layer][None, :], w_g, b_gate[layer][None, :],
          w_branch_a[layer].astype(BF16), w_branch_b[layer].astype(BF16),
          w_out[layer].astype(BF16), g_ffn[layer][None, :],
          w_ffn_gate[layer].astype(BF16), w_ffn_up[layer].astype(BF16),
          w_ffn_down[layer].astype(BF16),
          g_final[None, :])
    return x
```

```python
import functools
import math

import numpy as np
import jax
import jax.numpy as jnp
from jax import lax
from jax.experimental import pallas as pl
from jax.experimental.pallas import tpu as pltpu

D_MODEL = 1024
D_CONV = 512
CONV_K = 3
N_HEADS = 8
HEAD_DIM = 64
D_ATTN = N_HEADS * HEAD_DIM
N_IDX_HEADS = 8
IDX_DIM = 64
TOPK_MAX = 256
N_BUCKETS = 32
MAX_EXACT = 16
MAX_DISTANCE = 128
EPS = 1e-6

F32 = jnp.float32
BF16 = jnp.bfloat16
I32 = jnp.int32

INT_MIN = -(2 ** 31)
NEG = -1e30
LOG2E = math.log2(math.e)

TM1 = 512
TQ = 256
TK = 256
TM3 = 256
V_EXTRA = 16
QUAD = 4
LANES = 128
NEAR_ROWS = 64
STAGE1_LEAD = 3
VMEM_LIMIT = 56 * 1024 * 1024


def _bucket_steps():
    n = np.arange(2 * MAX_DISTANCE)
    large = MAX_EXACT + (np.log(np.maximum(n, 1).astype(np.float32) / MAX_EXACT)
                         / math.log(MAX_DISTANCE / MAX_EXACT)
                         * (N_BUCKETS - MAX_EXACT)).astype(np.int32)
    large = np.minimum(large, N_BUCKETS - 1)
    b = np.where(n < MAX_EXACT, n, large)
    assert np.all(b[MAX_DISTANCE:] == N_BUCKETS - 1) and np.all(np.diff(b) >= 0)
    steps = [(int(i), int(b[i])) for i in range(1, len(b)) if b[i] != b[i - 1]]
    return int(b[0]), steps


def _fold_rows(a, rows):
    parts = [a[i:i + rows, :] for i in range(0, a.shape[0], rows)]
    while len(parts) > 1:
        parts = [parts[i] + parts[i + 1] for i in range(0, len(parts) - 1, 2)] + (
            [parts[-1]] if len(parts) % 2 else [])
    return parts[0]


def _bit_transpose32(a):
    a = list(a)
    j, m = 16, 0x0000FFFF
    while j:
        k = 0
        while k < 32:
            t = (a[k] ^ (a[k + j] >> j)) & (m if m < 2 ** 31 else m - 2 ** 32)
            a[k] = a[k] ^ t
            a[k + j] = a[k + j] ^ (t << j)
            k = (k + j + 1) & ~j
        j >>= 1
        m ^= (m << j) & 0xFFFFFFFF
    return a


def _rms(x, g):
    return x * lax.rsqrt(jnp.mean(x * x, axis=-1, keepdims=True) + EPS) * g


def _proj_kernel(x_ref, g_ref, wn_ref, wt_ref, cw_ref,
                 mix_ref, k_ref, ik_ref, qT_ref, iqT_ref, vT_ref, iwT_ref, ubuf_ref):
    i = pl.program_id(1)
    parts = [slice(j * TK, (j + 1) * TK) for j in range(TM1 // TK)]
    hs = [_rms(x_ref[0, r, :], g_ref[...]).astype(BF16) for r in parts]

    def nat(h, c0, c1):
        return jnp.dot(h, wn_ref[:, c0:c1], preferred_element_type=F32)

    def tr(h, r0, r1):
        return lax.dot_general(wt_ref[r0:r1, :], h, (((1,), (1,)), ((), ())),
                               preferred_element_type=F32)

    @pl.when(i == 0)
    def _():
        ubuf_ref[0:8, :] = jnp.zeros((8, D_CONV), F32)

    us = [nat(h, D_CONV, 2 * D_CONV) * nat(h, 2 * D_CONV, 3 * D_CONV) for h in hs]
    for r, u in zip(parts, us):
        ubuf_ref[8 + r.start:8 + r.stop, :] = u
    cw = cw_ref[...]
    for r, h, u in zip(parts, hs, us):
        u1 = ubuf_ref[7 + r.start:7 + r.stop, :]
        u2 = ubuf_ref[6 + r.start:6 + r.stop, :]
        y = cw[0:1, :] * u2 + cw[1:2, :] * u1 + cw[2:3, :] * u
        mix_ref[0, r, :] = (nat(h, 0, D_CONV) * y).astype(BF16)
    ubuf_ref[0:8, :] = us[-1][TK - 8:TK, :]

    for r, h in zip(parts, hs):
        kk = nat(h, 3 * D_CONV, 3 * D_CONV + HEAD_DIM + IDX_DIM)
        k_ref[0, r, :] = kk[:, 0:HEAD_DIM].astype(BF16)
        ik_ref[0, r, :] = kk[:, HEAD_DIM:HEAD_DIM + IDX_DIM].astype(BF16)
    for r, h in zip(parts, hs):
        qT_ref[0, :, r] = tr(h, 0, D_ATTN).astype(BF16)
    for r, h in zip(parts, hs):
        iqT_ref[0, :, r] = tr(h, D_ATTN, 2 * D_ATTN).astype(BF16)
    ones_row = (lax.broadcasted_iota(I32, (V_EXTRA, TK), 0) == 0).astype(F32).astype(BF16)
    for j, (r, h) in enumerate(zip(parts, hs)):
        vw = tr(h, 2 * D_ATTN, 2 * D_ATTN + HEAD_DIM + V_EXTRA)
        vT = vw[0:HEAD_DIM, :].astype(BF16)
        vT_ref[0, j] = jnp.concatenate([vT, ones_row], axis=0)
        iwT_ref[0, :, r] = vw[HEAD_DIM:HEAD_DIM + N_IDX_HEADS, :] * (
            N_IDX_HEADS ** -0.5 * IDX_DIM ** -0.5)


def _dsa_kernel(rb_ref, iqT_ref, iwT_ref, ik_ref, qT_ref, k_ref, vT_ref,
                o_ref, key_ref, planes_ref, alive_ref, add_ref, bias_ref, oT_ref, m_ref, l_ref, al_ref,
                lta_ref, ltb_ref, *, k_top):
    b = pl.program_id(0)
    qi = pl.program_id(1)
    nchunks = qi + 1
    nck = key_ref.shape[0]
    n_keys = nck * TK

    @pl.when((b == 0) & (qi == 0))
    def _():
        b0, steps = _bucket_steps()
        row = lax.broadcasted_iota(I32, (TK, TQ), 0)
        col = lax.broadcasted_iota(I32, (TK, TQ), 1)
        for off in range(2):
            dist = col - row + off * TK
            for hh in range(N_HEADS):
                val = jnp.full((TK, TQ), rb_ref[b0, hh], F32)
                for n0, bk in steps:
                    val = jnp.where(dist >= n0, rb_ref[bk, hh], val)
                bias_ref[off, hh] = (val - rb_ref[N_BUCKETS - 1, hh]) * LOG2E

    w = iwT_ref[0]
    sub = TK // N_IDX_HEADS

    def score_step(c1, c2, dst_ref, src_ref, diagonal=False):
        if c1 is not None:
            ikc = ik_ref[0, pl.ds(pl.multiple_of(c1 * TK, TK), TK), :]
        if c2 is not None:
            wb = [jnp.broadcast_to(w[hh:hh + 1, :], (8, TQ)) for hh in range(N_IDX_HEADS)]
            row0 = lax.broadcasted_iota(I32, (sub, TQ), 0)
            lane = lax.broadcasted_iota(I32, (sub, TQ), 1)
        for g in range(N_IDX_HEADS):
            if c1 is not None:
                dst_ref[g] = jnp.dot(ikc, iqT_ref[0, g * IDX_DIM:(g + 1) * IDX_DIM, :],
                                     preferred_element_type=F32)
            if c2 is not None:
                rows = slice(g * sub, (g + 1) * sub)
                def term(hh):
                    r = jnp.maximum(src_ref[hh, rows, :], 0.0).reshape(sub // 8, 8, TQ)
                    return wb[hh][None] * r
                acc = term(0)
                for hh in range(1, N_IDX_HEADS):
                    acc = acc + term(hh)
                acc = acc.reshape(sub, TQ)
                zero = acc == 0.0
                bits = pltpu.bitcast(acc, I32)
                key = bits ^ ((bits >> 31) & 0x7FFFFFFF)
                first = c2 * TK + g * sub
                key = jnp.where(zero, (n_keys - 1 - first) - row0, key)
                if diagonal:
                    key = jnp.where(row0 + (g * sub) <= lane, key, INT_MIN)
                key_ref[c2, rows, :] = key
        if c2 is not None:
            for l0 in range(0, TQ, LANES):
                planes = _bit_transpose32([key_ref[c2, 8 * v:8 * v + 8, l0:l0 + LANES]
                                           for v in range(TK // 8)])
                for bit in range(32):
                    plane = planes[31 - bit]
                    planes_ref[c2, bit, :, l0:l0 + LANES] = ~plane if bit == 31 else plane

    def run_pipeline(step, n_steps):
        step(0, None, lta_ref, None)

        def two_steps(j, carry):
            c = 2 * j
            step(c + 1, c, ltb_ref, lta_ref)
            step(c + 2, c + 1, lta_ref, ltb_ref)
            return carry

        lax.fori_loop(0, n_steps // 2, two_steps, 0)

        @pl.when(n_steps % 2 == 1)
        def _():
            step(n_steps, n_steps - 1, ltb_ref, lta_ref)
            step(None, n_steps, None, ltb_ref, diagonal=True)

        @pl.when(n_steps % 2 == 0)
        def _():
            step(None, n_steps, None, lta_ref, diagonal=True)

    run_pipeline(score_step, nchunks - 1)

    def count(pred):
        def body(c, acc):
            m = pred(key_ref[c], c).astype(I32)
            return acc + jnp.sum(m.reshape(TK // 8, 8, TQ), axis=0)
        acc = lax.fori_loop(0, nchunks, body, jnp.zeros((8, TQ), I32))
        return jnp.sum(acc, axis=0, keepdims=True)

    n_quads = (nchunks + QUAD - 1) // QUAD

    def init_alive(c, carry):
        alive_ref[c] = jnp.broadcast_to(jnp.where(c < nchunks, -1, 0).astype(I32), (8, TQ))
        return carry

    lax.fori_loop(0, n_quads * QUAD, init_alive, 0)

    def clear_padding(c, carry):
        planes_ref[c] = jnp.zeros((32, 8, TQ), I32)
        return carry

    lax.fori_loop(nchunks, n_quads * QUAD, clear_padding, 0)

    def select_pass(bit, state, first):
        k_rem, alive_n, thr_u, flip = state

        def quad(j, cnt):
            for i in range(QUAD):
                c = QUAD * j + i
                a = alive_ref[c]
                if not first:
                    a = a & (planes_ref[c, bit + 1] ^ flip)
                    alive_ref[c] = a
                cnt = cnt + lax.population_count(a & planes_ref[c, bit])
            return cnt

        ones = jnp.sum(lax.fori_loop(0, n_quads, quad, jnp.zeros((8, TQ), I32)),
                       axis=0, keepdims=True)
        take = ones >= k_rem
        return (jnp.where(take, k_rem, k_rem - ones), jnp.where(take, ones, alive_n - ones),
                thr_u | jnp.where(take, jnp.left_shift(jnp.int32(1), bit), 0),
                jnp.broadcast_to(jnp.where(take, 0, -1), (8, TQ)))

    state = (jnp.full((1, TQ), k_top, I32), jnp.full((1, TQ), nchunks * TK, I32),
             jnp.zeros((1, TQ), I32), jnp.zeros((8, TQ), I32))
    state = select_pass(31, state, first=True)
    k_rem, alive_n, thr_u, _ = lax.fori_loop(
        1, 32, lambda p, st: select_pass(31 - p, st, first=False), state)
    thr = thr_u ^ INT_MIN
    any_excess = jnp.max(((alive_n > k_rem) & (thr > INT_MIN)).astype(I32)) > 0
    thr = jnp.maximum(thr, INT_MIN + 1)

    @pl.when(jnp.logical_not(any_excess))
    def _():
        def mask_chunk(c, carry):
            add_ref[c] = jnp.where(key_ref[c] >= thr, 0.0, NEG).astype(F32)
            return carry
        lax.fori_loop(0, nchunks, mask_chunk, 0)

    @pl.when(any_excess)
    def _():
        room = k_top - count(lambda kc, c: kc > thr)

        def key_index(c):
            return lax.broadcasted_iota(I32, (TK, TQ), 0) + c * TK

        def step(it, lim):
            cand = lim + jnp.left_shift(jnp.int32(1), 12 - it)
            cnt = count(lambda kc, c: (kc == thr) & (key_index(c) < cand))
            return jnp.where(cnt <= room, cand, lim)

        tie_lim = lax.fori_loop(0, 13, step, jnp.zeros((1, TQ), I32))

        def mask_chunk(c, carry):
            kc = key_ref[c]
            sel = (kc > thr) | ((kc == thr) & (key_index(c) < tie_lim))
            add_ref[c] = jnp.where(sel, 0.0, NEG).astype(F32)
            return carry
        lax.fori_loop(0, nchunks, mask_chunk, 0)

    for off in range(2):
        @pl.when(qi >= off)
        def _():
            for r0 in range(0, TK, NEAR_ROWS):
                rows = slice(r0, r0 + NEAR_ROWS)
                nm = add_ref[qi - off, rows, :]
                for hh in range(N_HEADS):
                    add_ref[nck + off * N_HEADS + hh, rows, :] = nm + bias_ref[off, hh, rows, :]

    def pipe_step(c1, c2, dst_ref, src_ref, diagonal=False):
        m_cur = m_ref[...]
        m_rows, l_rows = [], []
        if c1 is not None:
            kc = k_ref[0, pl.ds(pl.multiple_of(c1 * TK, TK), TK), :]
            near = c1 >= qi - 1
        if c2 is not None:
            vc = vT_ref[0, c2]
            alpha = al_ref[...]
        for i in range(N_HEADS + STAGE1_LEAD):
            if c1 is not None and i < N_HEADS:
                hh = i
                rows = slice(hh * HEAD_DIM, (hh + 1) * HEAD_DIM)
                tile = jnp.where(near, nck + (qi - c1) * N_HEADS + hh, c1)
                lt = jnp.dot(kc, qT_ref[0, rows, :], preferred_element_type=F32) + add_ref[tile]
                dst_ref[hh] = lt
                m_rows.append(jnp.max(lt, axis=0, keepdims=True))
                if hh == N_HEADS - 1:
                    m_new = jnp.maximum(m_cur, jnp.concatenate(m_rows, axis=0))
                    al_ref[...] = jnp.exp2(m_cur - m_new)
                    m_ref[...] = m_new
            if c2 is not None and i >= STAGE1_LEAD:
                hh = i - STAGE1_LEAD
                rows = slice(hh * HEAD_DIM, (hh + 1) * HEAD_DIM)
                p = jnp.exp2(src_ref[hh] - m_cur[hh:hh + 1, :]).astype(BF16)
                pv = jnp.dot(vc, p, preferred_element_type=F32)
                oT_ref[rows, :] = alpha[hh:hh + 1, :] * oT_ref[rows, :] + pv[0:HEAD_DIM, :]
                l_rows.append(pv[HEAD_DIM:HEAD_DIM + 1, :])
        if c2 is not None:
            l_ref[...] = alpha * l_ref[...] + jnp.concatenate(l_rows, axis=0)

    m_ref[...] = jnp.full((N_HEADS, TQ), NEG, F32)
    l_ref[...] = jnp.zeros((N_HEADS, TQ), F32)
    oT_ref[...] = jnp.zeros((D_ATTN, TQ), F32)
    run_pipeline(pipe_step, nchunks - 1)
    for hh in range(N_HEADS):
        rows = slice(hh * HEAD_DIM, (hh + 1) * HEAD_DIM)
        oT_ref[rows, :] = oT_ref[rows, :] / l_ref[hh:hh + 1, :]

    o_ref[0] = oT_ref[...].T.astype(BF16)


def _merge_kernel(x_ref, mix_ref, att_ref, gmix_ref, wg_ref, bg_ref, wa_ref, wb_ref, wo_ref,
                  gffn_ref, wfg_ref, wfu_ref, wfd_ref, gfin_ref, o_ref, *, final_norm):
    parts = [slice(j * (TM3 // 2), (j + 1) * (TM3 // 2)) for j in range(2)]

    def dot(a, w_ref):
        return jnp.dot(a, w_ref[...], preferred_element_type=F32)

    xs = [x_ref[0, r, :] for r in parts]
    hs = [_rms(x, gmix_ref[...]).astype(BF16) for x in xs]
    gs = [jax.nn.sigmoid(dot(h, wg_ref) + bg_ref[...]) for h in hs]
    yas = [dot(mix_ref[0, r, :], wa_ref) for r in parts]
    ybs = [dot(att_ref[0, r, :], wb_ref) for r in parts]
    ms = [(g[:, :D_MODEL] * ya + g[:, D_MODEL:] * yb).astype(BF16) for g, ya, yb in zip(gs, yas, ybs)]
    x1s = [x + dot(m, wo_ref) for x, m in zip(xs, ms)]
    h2s = [_rms(x1, gffn_ref[...]).astype(BF16) for x1 in x1s]
    fgs = [dot(h2, wfg_ref) for h2 in h2s]
    fus = [dot(h2, wfu_ref) for h2 in h2s]
    acts = [(jax.nn.silu(fg) * fu).astype(BF16) for fg, fu in zip(fgs, fus)]
    x2s = [x1 + dot(a, wfd_ref) for x1, a in zip(x1s, acts)]
    for r, x2 in zip(parts, x2s):
        o_ref[0, r, :] = _rms(x2, gfin_ref[...]) if final_norm else x2


def _const_spec(shape):
    nd = len(shape)
    return pl.BlockSpec(shape, lambda *_: (0,) * nd, pipeline_mode=pl.Buffered(1))


def kernel(x, g_mix, w_in, b_gate, conv_w, w_branch_a, w_branch_b, w_out, rel_bias, g_ffn,
           w_ffn_gate, w_ffn_up, w_ffn_down, g_final):
    bsz, seq, d = x.shape
    assert d == D_MODEL and seq % TM1 == 0 and seq % TQ == 0 and seq % TM3 == 0 and TQ == TK
    k_top = min(TOPK_MAX, seq // 4)
    assert TK == 32 * 8 and (seq // TK) % QUAD == 0
    assert seq < 2 ** 23
    nck = seq // TK
    d_ff = w_ffn_gate.shape[-1]
    cparams = functools.partial(pltpu.CompilerParams, vmem_limit_bytes=VMEM_LIMIT)

    for layer in range(g_mix.shape[0]):
        w = w_in[layer]
        o = np.cumsum([0, D_CONV, D_CONV, D_CONV, D_ATTN, HEAD_DIM, HEAD_DIM,
                       N_IDX_HEADS * IDX_DIM, IDX_DIM, N_IDX_HEADS, D_MODEL, D_MODEL])
        w_cb, w_cc, w_cx, w_q, w_k, w_v, w_iq, w_ik, w_iw, w_ga, w_gb = [
            w[:, int(o[j]):int(o[j + 1])] for j in range(11)]
        w_nat = jnp.concatenate([w_cb, w_cc, w_cx, w_k, w_ik], axis=1).astype(BF16)
        w_pad = jnp.zeros((D_MODEL, V_EXTRA - N_IDX_HEADS), w.dtype)
        w_tr = jnp.concatenate([w_q * (HEAD_DIM ** -0.5 * LOG2E), w_iq, w_v, w_iw, w_pad],
                               axis=1).T.astype(BF16)
        w_g = jnp.concatenate([w_ga, w_gb], axis=1).astype(BF16)
        n_nat, n_tr = w_nat.shape[1], w_tr.shape[0]

        mix, k_n, ik_n, qT, iqT, vT, iwT = pl.pallas_call(
            _proj_kernel,
            grid=(bsz, seq // TM1),
            in_specs=[
                pl.BlockSpec((1, TM1, D_MODEL), lambda b, i: (b, i, 0)),
                _const_spec((1, D_MODEL)),
                _const_spec((D_MODEL, n_nat)),
                _const_spec((n_tr, D_MODEL)),
                _const_spec((CONV_K, D_CONV)),
            ],
            out_specs=[
                pl.BlockSpec((1, TM1, D_CONV), lambda b, i: (b, i, 0)),
                pl.BlockSpec((1, TM1, HEAD_DIM), lambda b, i: (b, i, 0)),
                pl.BlockSpec((1, TM1, IDX_DIM), lambda b, i: (b, i, 0)),
                pl.BlockSpec((1, D_ATTN, TM1), lambda b, i: (b, 0, i)),
                pl.BlockSpec((1, N_IDX_HEADS * IDX_DIM, TM1), lambda b, i: (b, 0, i)),
                pl.BlockSpec((1, TM1 // TK, HEAD_DIM + V_EXTRA, TK), lambda b, i: (b, i, 0, 0)),
                pl.BlockSpec((1, N_IDX_HEADS, TM1), lambda b, i: (b, 0, i)),
            ],
            out_shape=[
                jax.ShapeDtypeStruct((bsz, seq, D_CONV), BF16),
                jax.ShapeDtypeStruct((bsz, seq, HEAD_DIM), BF16),
                jax.ShapeDtypeStruct((bsz, seq, IDX_DIM), BF16),
                jax.ShapeDtypeStruct((bsz, D_ATTN, seq), BF16),
                jax.ShapeDtypeStruct((bsz, N_IDX_HEADS * IDX_DIM, seq), BF16),
                jax.ShapeDtypeStruct((bsz, nck, HEAD_DIM + V_EXTRA, TK), BF16),
                jax.ShapeDtypeStruct((bsz, N_IDX_HEADS, seq), F32),
            ],
            scratch_shapes=[pltpu.VMEM((TM1 + 8, D_CONV), F32)],
            compiler_params=cparams(dimension_semantics=("arbitrary", "arbitrary")),
            name="proj",
        )(x, g_mix[layer][None, :], w_nat, w_tr, conv_w[layer])

        attn = pl.pallas_call(
            functools.partial(_dsa_kernel, k_top=k_top),
            grid=(bsz, seq // TQ),
            in_specs=[
                pl.BlockSpec(memory_space=pltpu.SMEM),
                pl.BlockSpec((1, N_IDX_HEADS * IDX_DIM, TQ), lambda b, i: (b, 0, i)),
                pl.BlockSpec((1, N_IDX_HEADS, TQ), lambda b, i: (b, 0, i)),
                pl.BlockSpec((1, seq, IDX_DIM), lambda b, i: (b, 0, 0)),
                pl.BlockSpec((1, D_ATTN, TQ), lambda b, i: (b, 0, i)),
                pl.BlockSpec((1, seq, HEAD_DIM), lambda b, i: (b, 0, 0)),
                pl.BlockSpec((1, nck, HEAD_DIM + V_EXTRA, TK), lambda b, i: (b, 0, 0, 0)),
            ],
            out_specs=pl.BlockSpec((1, TQ, D_ATTN), lambda b, i: (b, i, 0)),
            out_shape=jax.ShapeDtypeStruct((bsz, seq, D_ATTN), BF16),
            scratch_shapes=[
                pltpu.VMEM((nck, TK, TQ), I32),
                pltpu.VMEM((nck, 32, 8, TQ), I32),
                pltpu.VMEM((nck, 8, TQ), I32),
                pltpu.VMEM((nck + 2 * N_HEADS, TK, TQ), F32),
                pltpu.VMEM((2, N_HEADS, TK, TQ), F32),
                pltpu.VMEM((D_ATTN, TQ), F32),
                pltpu.VMEM((N_HEADS, TQ), F32),
                pltpu.VMEM((N_HEADS, TQ), F32),
                pltpu.VMEM((N_HEADS, TQ), F32),
                pltpu.VMEM((N_HEADS, TK, TQ), F32),
                pltpu.VMEM((N_HEADS, TK, TQ), F32),
            ],
            compiler_params=cparams(dimension_semantics=("arbitrary", "arbitrary")),
            name="dsa",
        )(rel_bias, iqT, iwT, ik_n, qT, k_n, vT)

        x = pl.pallas_call(
            functools.partial(_merge_kernel, final_norm=layer == g_mix.shape[0] - 1),
            grid=(bsz, seq // TM3),
            in_specs=[
                pl.BlockSpec((1, TM3, D_MODEL), lambda b, i: (b, i, 0)),
                pl.BlockSpec((1, TM3, D_CONV), lambda b, i: (b, i, 0)),
                pl.BlockSpec((1, TM3, D_ATTN), lambda b, i: (b, i, 0)),
                _const_spec((1, D_MODEL)),
                _const_spec((D_MODEL, 2 * D_MODEL)),
                _const_spec((1, 2 * D_MODEL)),
                _const_spec((D_CONV, D_MODEL)),
                _const_spec((D_ATTN, D_MODEL)),
                _const_spec((D_MODEL, D_MODEL)),
                _const_spec((1, D_MODEL)),
                _const_spec((D_MODEL, d_ff)),
                _const_spec((D_MODEL, d_ff)),
                _const_spec((d_ff, D_MODEL)),
                _const_spec((1, D_MODEL)),
            ],
            out_specs=pl.BlockSpec((1, TM3, D_MODEL), lambda b, i: (b, i, 0)),
            out_shape=jax.ShapeDtypeStruct((bsz, seq, D_MODEL), F32),
            compiler_params=cparams(dimension_semantics=("arbitrary", "arbitrary")),
            name="merge",
        )(x, mix, attn, g_mix[layer][None, :], w_g, b_gate[layer][None, :],
          w_branch_a[layer].astype(BF16), w_branch_b[layer].astype(BF16),
          w_out[layer].astype(BF16), g_ffn[layer][None, :],
          w_ffn_gate[layer].astype(BF16), w_ffn_up[layer].astype(BF16),
          w_ffn_down[layer].astype(BF16),
          g_final[None, :])
    return x
```

```python
import functools
import math

import numpy as np
import jax
import jax.numpy as jnp
from jax import lax
from jax.experimental import pallas as pl
from jax.experimental.pallas import tpu as pltpu

D_MODEL = 1024
D_CONV = 512
CONV_K = 3
N_HEADS = 8
HEAD_DIM = 64
D_ATTN = N_HEADS * HEAD_DIM
N_IDX_HEADS = 8
IDX_DIM = 64
TOPK_MAX = 256
N_BUCKETS = 32
MAX_EXACT = 16
MAX_DISTANCE = 128
EPS = 1e-6

F32 = jnp.float32
BF16 = jnp.bfloat16
I32 = jnp.int32

INT_MIN = -(2 ** 31)
NEG = -1e30
LOG2E = math.log2(math.e)

TM1 = 512
TQ = 256
TK = 256
TM3 = 512
V_EXTRA = 16
QUAD = 4
LANES = 128
STAGE1_LEAD = 3
VMEM_LIMIT = 56 * 1024 * 1024


def _bucket_steps():
    n = np.arange(2 * MAX_DISTANCE)
    large = MAX_EXACT + (np.log(np.maximum(n, 1).astype(np.float32) / MAX_EXACT)
                         / math.log(MAX_DISTANCE / MAX_EXACT)
                         * (N_BUCKETS - MAX_EXACT)).astype(np.int32)
    large = np.minimum(large, N_BUCKETS - 1)
    b = np.where(n < MAX_EXACT, n, large)
    assert np.all(b[MAX_DISTANCE:] == N_BUCKETS - 1) and np.all(np.diff(b) >= 0)
    steps = [(int(i), int(b[i])) for i in range(1, len(b)) if b[i] != b[i - 1]]
    return int(b[0]), steps


def _fold_rows(a, rows):
    parts = [a[i:i + rows, :] for i in range(0, a.shape[0], rows)]
    while len(parts) > 1:
        parts = [parts[i] + parts[i + 1] for i in range(0, len(parts) - 1, 2)] + (
            [parts[-1]] if len(parts) % 2 else [])
    return parts[0]


def _bit_transpose32(a):
    a = list(a)
    j, m = 16, 0x0000FFFF
    while j:
        k = 0
        while k < 32:
            t = (a[k] ^ (a[k + j] >> j)) & (m if m < 2 ** 31 else m - 2 ** 32)
            a[k] = a[k] ^ t
            a[k + j] = a[k + j] ^ (t << j)
            k = (k + j + 1) & ~j
        j >>= 1
        m ^= (m << j) & 0xFFFFFFFF
    return a


def _rms(x, g):
    return x * lax.rsqrt(jnp.mean(x * x, axis=-1, keepdims=True) + EPS) * g


def _proj_kernel(x_ref, g_ref, wn_ref, wt_ref, cw_ref,
                 mix_ref, k_ref, ik_ref, qT_ref, iqT_ref, vT_ref, iwT_ref, ubuf_ref):
    i = pl.program_id(1)
    parts = [slice(j * TK, (j + 1) * TK) for j in range(TM1 // TK)]
    hs = [_rms(x_ref[0, r, :], g_ref[...]).astype(BF16) for r in parts]

    def nat(h, c0, c1):
        return jnp.dot(h, wn_ref[:, c0:c1], preferred_element_type=F32)

    def tr(h, r0, r1):
        return lax.dot_general(wt_ref[r0:r1, :], h, (((1,), (1,)), ((), ())),
                               preferred_element_type=F32)

    @pl.when(i == 0)
    def _():
        ubuf_ref[0:8, :] = jnp.zeros((8, D_CONV), F32)

    us = [nat(h, D_CONV, 2 * D_CONV) * nat(h, 2 * D_CONV, 3 * D_CONV) for h in hs]
    for r, u in zip(parts, us):
        ubuf_ref[8 + r.start:8 + r.stop, :] = u
    cw = cw_ref[...]
    for r, h, u in zip(parts, hs, us):
        u1 = ubuf_ref[7 + r.start:7 + r.stop, :]
        u2 = ubuf_ref[6 + r.start:6 + r.stop, :]
        y = cw[0:1, :] * u2 + cw[1:2, :] * u1 + cw[2:3, :] * u
        mix_ref[0, r, :] = (nat(h, 0, D_CONV) * y).astype(BF16)
    ubuf_ref[0:8, :] = us[-1][TK - 8:TK, :]

    for r, h in zip(parts, hs):
        kk = nat(h, 3 * D_CONV, 3 * D_CONV + HEAD_DIM + IDX_DIM)
        k_ref[0, r, :] = kk[:, 0:HEAD_DIM].astype(BF16)
        ik_ref[0, r, :] = kk[:, HEAD_DIM:HEAD_DIM + IDX_DIM].astype(BF16)
    for r, h in zip(parts, hs):
        qT_ref[0, :, r] = tr(h, 0, D_ATTN).astype(BF16)
    for r, h in zip(parts, hs):
        iqT_ref[0, :, r] = tr(h, D_ATTN, 2 * D_ATTN).astype(BF16)
    ones_row = (lax.broadcasted_iota(I32, (V_EXTRA, TK), 0) == 0).astype(F32).astype(BF16)
    for j, (r, h) in enumerate(zip(parts, hs)):
        vw = tr(h, 2 * D_ATTN, 2 * D_ATTN + HEAD_DIM + V_EXTRA)
        vT = vw[0:HEAD_DIM, :].astype(BF16)
        vT_ref[0, j] = jnp.concatenate([vT, ones_row], axis=0)
        iwT_ref[0, :, r] = vw[HEAD_DIM:HEAD_DIM + N_IDX_HEADS, :] * (
            N_IDX_HEADS ** -0.5 * IDX_DIM ** -0.5)


def _dsa_kernel(rb_ref, iqT_ref, iwT_ref, ik_ref, qT_ref, k_ref, vT_ref,
                o_ref, key_ref, planes_ref, alive_ref, add_ref, bias_ref, oT_ref, m_ref, l_ref, al_ref,
                lta_ref, ltb_ref, *, k_top):
    b = pl.program_id(0)
    qi = pl.program_id(1)
    nchunks = qi + 1
    nck = key_ref.shape[0]
    n_keys = nck * TK

    @pl.when((b == 0) & (qi == 0))
    def _():
        b0, steps = _bucket_steps()
        row = lax.broadcasted_iota(I32, (TK, TQ), 0)
        col = lax.broadcasted_iota(I32, (TK, TQ), 1)
        for off in range(2):
            dist = col - row + off * TK
            for hh in range(N_HEADS):
                val = jnp.full((TK, TQ), rb_ref[b0, hh], F32)
                for n0, bk in steps:
                    val = jnp.where(dist >= n0, rb_ref[bk, hh], val)
                bias_ref[off, hh] = (val - rb_ref[N_BUCKETS - 1, hh]) * LOG2E

    w = iwT_ref[0]
    sub = TK // N_IDX_HEADS

    def score_step(c1, c2, dst_ref, src_ref, diagonal=False):
        if c1 is not None:
            ikc = ik_ref[0, pl.ds(pl.multiple_of(c1 * TK, TK), TK), :]
        if c2 is not None:
            wb = [jnp.broadcast_to(w[hh:hh + 1, :], (8, TQ)) for hh in range(N_IDX_HEADS)]
            row0 = lax.broadcasted_iota(I32, (sub, TQ), 0)
            lane = lax.broadcasted_iota(I32, (sub, TQ), 1)
        for g in range(N_IDX_HEADS):
            if c1 is not None:
                dst_ref[g] = jnp.dot(ikc, iqT_ref[0, g * IDX_DIM:(g + 1) * IDX_DIM, :],
                                     preferred_element_type=F32)
            if c2 is not None:
                rows = slice(g * sub, (g + 1) * sub)
                def term(hh):
                    r = jnp.maximum(src_ref[hh, rows, :], 0.0).reshape(sub // 8, 8, TQ)
                    return wb[hh][None] * r
                acc = term(0)
                for hh in range(1, N_IDX_HEADS):
                    acc = acc + term(hh)
                acc = acc.reshape(sub, TQ)
                zero = acc == 0.0
                bits = pltpu.bitcast(acc, I32)
                key = bits ^ ((bits >> 31) & 0x7FFFFFFF)
                first = c2 * TK + g * sub
                key = jnp.where(zero, (n_keys - 1 - first) - row0, key)
                if diagonal:
                    key = jnp.where(row0 + (g * sub) <= lane, key, INT_MIN)
                key_ref[c2, rows, :] = key
        if c2 is not None:
            for l0 in range(0, TQ, LANES):
                planes = _bit_transpose32([key_ref[c2, 8 * v:8 * v + 8, l0:l0 + LANES]
                                           for v in range(TK // 8)])
                for bit in range(32):
                    plane = planes[31 - bit]
                    planes_ref[c2, bit, :, l0:l0 + LANES] = ~plane if bit == 31 else plane

    def run_pipeline(step, n_steps):
        step(0, None, lta_ref, None)

        def two_steps(j, carry):
            c = 2 * j
            step(c + 1, c, ltb_ref, lta_ref)
            step(c + 2, c + 1, lta_ref, ltb_ref)
            return carry

        lax.fori_loop(0, n_steps // 2, two_steps, 0)

        @pl.when(n_steps % 2 == 1)
        def _():
            step(n_steps, n_steps - 1, ltb_ref, lta_ref)
            step(None, n_steps, None, ltb_ref, diagonal=True)

        @pl.when(n_steps % 2 == 0)
        def _():
            step(None, n_steps, None, lta_ref, diagonal=True)

    run_pipeline(score_step, nchunks - 1)

    def count(pred):
        def body(c, acc):
            m = pred(key_ref[c], c).astype(I32)
            return acc + jnp.sum(m.reshape(TK // 8, 8, TQ), axis=0)
        acc = lax.fori_loop(0, nchunks, body, jnp.zeros((8, TQ), I32))
        return jnp.sum(acc, axis=0, keepdims=True)

    n_quads = (nchunks + QUAD - 1) // QUAD

    def init_alive(c, carry):
        alive_ref[c] = jnp.broadcast_to(jnp.where(c < nchunks, -1, 0).astype(I32), (8, TQ))
        return carry

    lax.fori_loop(0, n_quads * QUAD, init_alive, 0)

    def clear_padding(c, carry):
        planes_ref[c] = jnp.zeros((32, 8, TQ), I32)
        return carry

    lax.fori_loop(nchunks, n_quads * QUAD, clear_padding, 0)

    def select_pass(bit, state, first):
        k_rem, alive_n, thr_u, flip = state

        def quad(j, cnt):
            for i in range(QUAD):
                c = QUAD * j + i
                a = alive_ref[c]
                if not first:
                    a = a & (planes_ref[c, bit + 1] ^ flip)
                    alive_ref[c] = a
                cnt = cnt + lax.population_count(a & planes_ref[c, bit])
            return cnt

        ones = jnp.sum(lax.fori_loop(0, n_quads, quad, jnp.zeros((8, TQ), I32)),
                       axis=0, keepdims=True)
        take = ones >= k_rem
        return (jnp.where(take, k_rem, k_rem - ones), jnp.where(take, ones, alive_n - ones),
                thr_u | jnp.where(take, jnp.left_shift(jnp.int32(1), bit), 0),
                jnp.broadcast_to(jnp.where(take, 0, -1), (8, TQ)))

    state = (jnp.full((1, TQ), k_top, I32), jnp.full((1, TQ), nchunks * TK, I32),
             jnp.zeros((1, TQ), I32), jnp.zeros((8, TQ), I32))
    state = select_pass(31, state, first=True)
    k_rem, alive_n, thr_u, _ = lax.fori_loop(
        1, 32, lambda p, st: select_pass(31 - p, st, first=False), state)
    thr = thr_u ^ INT_MIN
    any_excess = jnp.max(((alive_n > k_rem) & (thr > INT_MIN)).astype(I32)) > 0
    thr = jnp.maximum(thr, INT_MIN + 1)

    @pl.when(jnp.logical_not(any_excess))
    def _():
        def mask_chunk(c, carry):
            add_ref[c] = jnp.where(key_ref[c] >= thr, 0.0, NEG).astype(F32)
            return carry
        lax.fori_loop(0, nchunks, mask_chunk, 0)

    @pl.when(any_excess)
    def _():
        room = k_top - count(lambda kc, c: kc > thr)

        def key_index(c):
            return lax.broadcasted_iota(I32, (TK, TQ), 0) + c * TK

        def step(it, lim):
            cand = lim + jnp.left_shift(jnp.int32(1), 12 - it)
            cnt = count(lambda kc, c: (kc == thr) & (key_index(c) < cand))
            return jnp.where(cnt <= room, cand, lim)

        tie_lim = lax.fori_loop(0, 13, step, jnp.zeros((1, TQ), I32))

        def mask_chunk(c, carry):
            kc = key_ref[c]
            sel = (kc > thr) | ((kc == thr) & (key_index(c) < tie_lim))
            add_ref[c] = jnp.where(sel, 0.0, NEG).astype(F32)
            return carry
        lax.fori_loop(0, nchunks, mask_chunk, 0)

    for off in range(2):
        @pl.when(qi >= off)
        def _():
            nm = add_ref[qi - off]
            for hh in range(N_HEADS):
                add_ref[nck + off * N_HEADS + hh] = nm + bias_ref[off, hh]

    def pipe_step(c1, c2, dst_ref, src_ref, diagonal=False):
        m_cur = m_ref[...]
        m_rows, l_rows = [], []
        if c1 is not None:
            kc = k_ref[0, pl.ds(pl.multiple_of(c1 * TK, TK), TK), :]
            near = c1 >= qi - 1
        if c2 is not None:
            vc = vT_ref[0, c2]
            alpha = al_ref[...]
        for i in range(N_HEADS + STAGE1_LEAD):
            if c1 is not None and i < N_HEADS:
                hh = i
                rows = slice(hh * HEAD_DIM, (hh + 1) * HEAD_DIM)
                tile = jnp.where(near, nck + (qi - c1) * N_HEADS + hh, c1)
                lt = jnp.dot(kc, qT_ref[0, rows, :], preferred_element_type=F32) + add_ref[tile]
                dst_ref[hh] = lt
                m_rows.append(jnp.max(lt, axis=0, keepdims=True))
                if hh == N_HEADS - 1:
                    m_new = jnp.maximum(m_cur, jnp.concatenate(m_rows, axis=0))
                    al_ref[...] = jnp.exp2(m_cur - m_new)
                    m_ref[...] = m_new
            if c2 is not None and i >= STAGE1_LEAD:
                hh = i - STAGE1_LEAD
                rows = slice(hh * HEAD_DIM, (hh + 1) * HEAD_DIM)
                p = jnp.exp2(src_ref[hh] - m_cur[hh:hh + 1, :]).astype(BF16)
                pv = jnp.dot(vc, p, preferred_element_type=F32)
                oT_ref[rows, :] = alpha[hh:hh + 1, :] * oT_ref[rows, :] + pv[0:HEAD_DIM, :]
                l_rows.append(pv[HEAD_DIM:HEAD_DIM + 1, :])
        if c2 is not None:
            l_ref[...] = alpha * l_ref[...] + jnp.concatenate(l_rows, axis=0)

    m_ref[...] = jnp.full((N_HEADS, TQ), NEG, F32)
    l_ref[...] = jnp.zeros((N_HEADS, TQ), F32)
    oT_ref[...] = jnp.zeros((D_ATTN, TQ), F32)
    run_pipeline(pipe_step, nchunks - 1)
    for hh in range(N_HEADS):
        rows = slice(hh * HEAD_DIM, (hh + 1) * HEAD_DIM)
        oT_ref[rows, :] = oT_ref[rows, :] / l_ref[hh:hh + 1, :]

    o_ref[0] = oT_ref[...].T.astype(BF16)


def _merge_kernel(x_ref, mix_ref, att_ref, gmix_ref, wg_ref, bg_ref, wa_ref, wb_ref, wo_ref,
                  gffn_ref, wfg_ref, wfu_ref, wfd_ref, gfin_ref, o_ref, *, final_norm):
    parts = [slice(j * (TM3 // 2), (j + 1) * (TM3 // 2)) for j in range(2)]

    def dot(a, w_ref):
        return jnp.dot(a, w_ref[...], preferred_element_type=F32)

    xs = [x_ref[0, r, :] for r in parts]
    hs = [_rms(x, gmix_ref[...]).astype(BF16) for x in xs]
    gs = [jax.nn.sigmoid(dot(h, wg_ref) + bg_ref[...]) for h in hs]
    yas = [dot(mix_ref[0, r, :], wa_ref) for r in parts]
    ybs = [dot(att_ref[0, r, :], wb_ref) for r in parts]
    ms = [(g[:, :D_MODEL] * ya + g[:, D_MODEL:] * yb).astype(BF16) for g, ya, yb in zip(gs, yas, ybs)]
    x1s = [x + dot(m, wo_ref) for x, m in zip(xs, ms)]
    h2s = [_rms(x1, gffn_ref[...]).astype(BF16) for x1 in x1s]
    fgs = [dot(h2, wfg_ref) for h2 in h2s]
    fus = [dot(h2, wfu_ref) for h2 in h2s]
    acts = [(jax.nn.silu(fg) * fu).astype(BF16) for fg, fu in zip(fgs, fus)]
    x2s = [x1 + dot(a, wfd_ref) for x1, a in zip(x1s, acts)]
    for r, x2 in zip(parts, x2s):
        o_ref[0, r, :] = _rms(x2, gfin_ref[...]) if final_norm else x2


def _const_spec(shape):
    nd = len(shape)
    return pl.BlockSpec(shape, lambda *_: (0,) * nd, pipeline_mode=pl.Buffered(1))


def kernel(x, g_mix, w_in, b_gate, conv_w, w_branch_a, w_branch_b, w_out, rel_bias, g_ffn,
           w_ffn_gate, w_ffn_up, w_ffn_down, g_final):
    bsz, seq, d = x.shape
    assert d == D_MODEL and seq % TM1 == 0 and seq % TQ == 0 and seq % TM3 == 0 and TQ == TK
    k_top = min(TOPK_MAX, seq // 4)
    assert TK == 32 * 8 and (seq // TK) % QUAD == 0
    assert seq < 2 ** 23
    nck = seq // TK
    d_ff = w_ffn_gate.shape[-1]
    cparams = functools.partial(pltpu.CompilerParams, vmem_limit_bytes=VMEM_LIMIT)

    for layer in range(g_mix.shape[0]):
        w = w_in[layer]
        o = np.cumsum([0, D_CONV, D_CONV, D_CONV, D_ATTN, HEAD_DIM, HEAD_DIM,
                       N_IDX_HEADS * IDX_DIM, IDX_DIM, N_IDX_HEADS, D_MODEL, D_MODEL])
        w_cb, w_cc, w_cx, w_q, w_k, w_v, w_iq, w_ik, w_iw, w_ga, w_gb = [
            w[:, int(o[j]):int(o[j + 1])] for j in range(11)]
        w_nat = jnp.concatenate([w_cb, w_cc, w_cx, w_k, w_ik], axis=1).astype(BF16)
        w_pad = jnp.zeros((D_MODEL, V_EXTRA - N_IDX_HEADS), w.dtype)
        w_tr = jnp.concatenate([w_q * (HEAD_DIM ** -0.5 * LOG2E), w_iq, w_v, w_iw, w_pad],
                               axis=1).T.astype(BF16)
        w_g = jnp.concatenate([w_ga, w_gb], axis=1).astype(BF16)
        n_nat, n_tr = w_nat.shape[1], w_tr.shape[0]

        mix, k_n, ik_n, qT, iqT, vT, iwT = pl.pallas_call(
            _proj_kernel,
            grid=(bsz, seq // TM1),
            in_specs=[
                pl.BlockSpec((1, TM1, D_MODEL), lambda b, i: (b, i, 0)),
                _const_spec((1, D_MODEL)),
                _const_spec((D_MODEL, n_nat)),
                _const_spec((n_tr, D_MODEL)),
                _const_spec((CONV_K, D_CONV)),
            ],
            out_specs=[
                pl.BlockSpec((1, TM1, D_CONV), lambda b, i: (b, i, 0)),
                pl.BlockSpec((1, TM1, HEAD_DIM), lambda b, i: (b, i, 0)),
                pl.BlockSpec((1, TM1, IDX_DIM), lambda b, i: (b, i, 0)),
                pl.BlockSpec((1, D_ATTN, TM1), lambda b, i: (b, 0, i)),
                pl.BlockSpec((1, N_IDX_HEADS * IDX_DIM, TM1), lambda b, i: (b, 0, i)),
                pl.BlockSpec((1, TM1 // TK, HEAD_DIM + V_EXTRA, TK), lambda b, i: (b, i, 0, 0)),
                pl.BlockSpec((1, N_IDX_HEADS, TM1), lambda b, i: (b, 0, i)),
            ],
            out_shape=[
                jax.ShapeDtypeStruct((bsz, seq, D_CONV), BF16),
                jax.ShapeDtypeStruct((bsz, seq, HEAD_DIM), BF16),
                jax.ShapeDtypeStruct((bsz, seq, IDX_DIM), BF16),
                jax.ShapeDtypeStruct((bsz, D_ATTN, seq), BF16),
                jax.ShapeDtypeStruct((bsz, N_IDX_HEADS * IDX_DIM, seq), BF16),
                jax.ShapeDtypeStruct((bsz, nck, HEAD_DIM + V_EXTRA, TK), BF16),
                jax.ShapeDtypeStruct((bsz, N_IDX_HEADS, seq), F32),
            ],
            scratch_shapes=[pltpu.VMEM((TM1 + 8, D_CONV), F32)],
            compiler_params=cparams(dimension_semantics=("arbitrary", "arbitrary")),
            name="proj",
        )(x, g_mix[layer][None, :], w_nat, w_tr, conv_w[layer])

        attn = pl.pallas_call(
            functools.partial(_dsa_kernel, k_top=k_top),
            grid=(bsz, seq // TQ),
            in_specs=[
                pl.BlockSpec(memory_space=pltpu.SMEM),
                pl.BlockSpec((1, N_IDX_HEADS * IDX_DIM, TQ), lambda b, i: (b, 0, i)),
                pl.BlockSpec((1, N_IDX_HEADS, TQ), lambda b, i: (b, 0, i)),
                pl.BlockSpec((1, seq, IDX_DIM), lambda b, i: (b, 0, 0)),
                pl.BlockSpec((1, D_ATTN, TQ), lambda b, i: (b, 0, i)),
                pl.BlockSpec((1, seq, HEAD_DIM), lambda b, i: (b, 0, 0)),
                pl.BlockSpec((1, nck, HEAD_DIM + V_EXTRA, TK), lambda b, i: (b, 0, 0, 0)),
            ],
            out_specs=pl.BlockSpec((1, TQ, D_ATTN), lambda b, i: (b, i, 0)),
            out_shape=jax.ShapeDtypeStruct((bsz, seq, D_ATTN), BF16),
            scratch_shapes=[
                pltpu.VMEM((nck, TK, TQ), I32),
                pltpu.VMEM((nck, 32, 8, TQ), I32),
                pltpu.VMEM((nck, 8, TQ), I32),
                pltpu.VMEM((nck + 2 * N_HEADS, TK, TQ), F32),
                pltpu.VMEM((2, N_HEADS, TK, TQ), F32),
                pltpu.VMEM((D_ATTN, TQ), F32),
                pltpu.VMEM((N_HEADS, TQ), F32),
                pltpu.VMEM((N_HEADS, TQ), F32),
                pltpu.VMEM((N_HEADS, TQ), F32),
                pltpu.VMEM((N_HEADS, TK, TQ), F32),
                pltpu.VMEM((N_HEADS, TK, TQ), F32),
            ],
            compiler_params=cparams(dimension_semantics=("arbitrary", "arbitrary")),
            name="dsa",
        )(rel_bias, iqT, iwT, ik_n, qT, k_n, vT)

        x = pl.pallas_call(
            functools.partial(_merge_kernel, final_norm=layer == g_mix.shape[0] - 1),
            grid=(bsz, seq // TM3),
            in_specs=[
                pl.BlockSpec((1, TM3, D_MODEL), lambda b, i: (b, i, 0)),
                pl.BlockSpec((1, TM3, D_CONV), lambda b, i: (b, i, 0)),
                pl.BlockSpec((1, TM3, D_ATTN), lambda b, i: (b, i, 0)),
                _const_spec((1, D_MODEL)),
                _const_spec((D_MODEL, 2 * D_MODEL)),
                _const_spec((1, 2 * D_MODEL)),
                _const_spec((D_CONV, D_MODEL)),
                _const_spec((D_ATTN, D_MODEL)),
                _const_spec((D_MODEL, D_MODEL)),
                _const_spec((1, D_MODEL)),
                _const_spec((D_MODEL, d_ff)),
                _const_spec((D_MODEL, d_ff)),
                _const_spec((d_ff, D_MODEL)),
                _const_spec((1, D_MODEL)),
            ],
            out_specs=pl.BlockSpec((1, TM3, D_MODEL), lambda b, i: (b, i, 0)),
            out_shape=jax.ShapeDtypeStruct((bsz, seq, D_MODEL), F32),
            compiler_params=cparams(dimension_semantics=("arbitrary", "arbitrary")),
            name="merge",
        )(x, mix, attn, g_mix[layer][None, :], w_g, b_gate[layer][None, :],
          w_branch_a[layer].astype(BF16), w_branch_b[layer].astype(BF16),
          w_out[layer].astype(BF16), g_ffn[layer][None, :],
          w_ffn_gate[layer].astype(BF16), w_ffn_up[layer].astype(BF16),
          w_ffn_down[layer].astype(BF16),
          g_final[None, :])
    return x
```
